```python
import math
import jax
import jax.numpy as jnp
from jax import lax
import numpy as np

D_MODEL = 1024
BATCH = 2
SEQ = 8192
DEPTH = 2
DEC_BATCH = 128
DEC_SEQ = 4
PAST_LEN = 2048
PAGE_SIZE = 128

HEAD_DIM = 64
ROPE_THETA = 10000.0
NORM_EPS = 1e-6
Q_CHUNK = 128
D_FF = 2816
A_WIDTH = 512
A_CONV = 3
NSA_HEADS = 8
NSA_KV_HEADS = 2
NSA_BLOCK = 64
NSA_TOPK = 16
NSA_WINDOW = 512
DIFF_HEADS = 4
D_WIDTH = 512
D_CONV = 31
N_BRANCH = 4
N_ADA = 9

NSA_Q_W = NSA_HEADS * HEAD_DIM
NSA_KV_W = 6 * NSA_KV_HEADS * HEAD_DIM
NSA_GATE_W = 3 * NSA_HEADS
DIFF_W = DIFF_HEADS * 2 * HEAD_DIM
MIX_SPLITS = (A_WIDTH, A_WIDTH, A_WIDTH, NSA_Q_W, NSA_KV_W, NSA_GATE_W,
              DIFF_W, DIFF_W, DIFF_W, 2 * D_WIDTH, N_BRANCH * D_MODEL)
MIX_IN_W = sum(MIX_SPLITS)

kernel_name = "hybrid_gated_nsa_diffattn_conv_decoder_step"


def rms_norm(x, g):
    xf = x.astype(jnp.float32)
    y = xf * lax.rsqrt(jnp.mean(xf * xf, axis=-1, keepdims=True) + NORM_EPS)
    return (y * g.astype(jnp.float32)).astype(x.dtype)


def layer_norm(x, g, b):
    xf = x.astype(jnp.float32)
    mu = jnp.mean(xf, axis=-1, keepdims=True)
    var = jnp.mean(jnp.square(xf - mu), axis=-1, keepdims=True)
    y = (xf - mu) * lax.rsqrt(var + NORM_EPS)
    return (y * g.astype(jnp.float32) + b.astype(jnp.float32)).astype(x.dtype)


def rope(x, pos):
    half = x.shape[-1] // 2
    inv = ROPE_THETA ** (-jnp.arange(half, dtype=jnp.float32) / half)
    ang = pos.astype(jnp.float32)[:, None] * inv[None, :]
    shp = (pos.shape[0],) + (1,) * (x.ndim - 3) + (half,)
    cos, sin = jnp.cos(ang).reshape(shp), jnp.sin(ang).reshape(shp)
    xf = x.astype(jnp.float32)
    x1, x2 = xf[..., :half], xf[..., half:]
    return jnp.concatenate([x1 * cos - x2 * sin, x2 * cos + x1 * sin], axis=-1).astype(x.dtype)


def masked_softmax(s, mask):
    s = jnp.where(mask, s.astype(jnp.float32), -1e30)
    p = jnp.exp(s - jnp.max(s, axis=-1, keepdims=True))
    p = jnp.where(mask, p, 0.0)
    return p / jnp.maximum(jnp.sum(p, axis=-1, keepdims=True), 1e-30)


def causal_dwconv(u, past, w, b):
    full = jnp.concatenate([past, u], axis=1)
    out = lax.conv_general_dilated(full, w[:, None, :], window_strides=(1,), padding="VALID",
                                   dimension_numbers=("NWC", "WIO", "NWC"),
                                   feature_group_count=u.shape[-1])
    return out + b, full[:, -(w.shape[0] - 1):]


def swiglu(h, w_up, w_down):
    a, v = jnp.split(h @ w_up, 2, axis=-1)
    return (jax.nn.silu(a) * v) @ w_down


def to_chunks(x, n):
    return jnp.moveaxis(x.reshape((x.shape[0], n, x.shape[1] // n) + x.shape[2:]), 1, 0)


def from_chunks(y):
    y = jnp.moveaxis(y, 0, 1)
    return y.reshape((y.shape[0], y.shape[1] * y.shape[2]) + y.shape[3:])


def gather_pages(cache, page_table):
    g = cache[page_table]
    return g.reshape((g.shape[0], g.shape[1] * g.shape[2]) + g.shape[3:])


def split_cols(z):
    parts, start = [], 0
    for w in MIX_SPLITS:
        parts.append(z[..., start:start + w])
        start += w
    return parts


def window_attn(q, k, v, qpos, kpos):
    B, C, H, d = q.shape
    hk = k.shape[2]
    mask = ((kpos[None, :] >= 0) & (kpos[None, :] <= qpos[:, None])
            & (qpos[:, None] - kpos[None, :] <= NSA_WINDOW))
    s = jnp.einsum("bqhgd,bkhd->bhgqk", q.reshape(B, C, hk, H // hk, d), k).astype(jnp.float32) * (d ** -0.5)
    pw = masked_softmax(s, mask)
    return jnp.einsum("bhgqk,bkhd->bqhgd", pw.astype(v.dtype), v).reshape(B, C, H, d)


def nsa_mixer(q_raw, kv_raw, gate_raw, pos, p, past_rows, past_win):
    B, S, _ = q_raw.shape
    hk, grp = NSA_KV_HEADS, NSA_HEADS // NSA_KV_HEADS
    scale = HEAD_DIM ** -0.5
    g = p["nsa_qk_g"]
    q = rms_norm(q_raw.reshape(B, S, NSA_HEADS, HEAD_DIM), g[0])
    q_rot = rope(q, pos)
    kv = kv_raw.reshape(B, S, 6, hk, HEAD_DIM)
    k_cmp = rms_norm(kv[:, :, 0], g[1])
    k_slc = rope(rms_norm(kv[:, :, 2], g[2]), pos)
    k_win = rope(rms_norm(kv[:, :, 4], g[3]), pos)
    rows = jnp.stack([k_cmp, kv[:, :, 1], k_slc, kv[:, :, 3]], axis=2)
    win_rows = jnp.stack([k_win, kv[:, :, 5]], axis=2)

    full = rows if past_rows is None else jnp.concatenate([past_rows, rows], axis=1)
    t0 = full.shape[1]
    t = -(-t0 // NSA_BLOCK) * NSA_BLOCK
    full = jnp.pad(full, ((0, 0), (0, t - t0), (0, 0), (0, 0), (0, 0)))
    nb = t // NSA_BLOCK
    blk = full.reshape(B, nb, NSA_BLOCK, 4, hk, HEAD_DIM)
    kc_blk = jnp.einsum("bnlhd,ld->bnhd", blk[:, :, :, 0], p["nsa_cmp_w"][0])
    vc_blk = jnp.einsum("bnlhd,ld->bnhd", blk[:, :, :, 1], p["nsa_cmp_w"][1])
    ks_blk = jnp.moveaxis(blk[:, :, :, 2], 3, 1)
    vs_blk = jnp.moveaxis(blk[:, :, :, 3], 3, 1)
    n_sel = min(NSA_TOPK, nb)
    blk_ids = jnp.arange(nb)
    bi = jnp.arange(B)[:, None, None, None]
    hi = jnp.arange(hk)[None, :, None, None]

    def cmp_slc_chunk(args):
        qn, qr, qpos = args
        c = qn.shape[1]
        s = jnp.einsum("bqhgd,bnhd->bhgqn", qn.reshape(B, c, hk, grp, HEAD_DIM), kc_blk).astype(jnp.float32) * scale
        complete = (blk_ids[None, :] + 1) * NSA_BLOCK - 1 <= qpos[:, None]
        pc = masked_softmax(s, complete)
        o_cmp = jnp.einsum("bhgqn,bnhd->bqhgd", pc.astype(vc_blk.dtype), vc_blk)
        cur = qpos // NSA_BLOCK
        imp = jnp.sum(pc, axis=2)
        forced = (blk_ids[None, :] == cur[:, None]) | (blk_ids[None, :] == 0)
        imp = jnp.where(forced, grp + 1.0, imp)
        imp = jnp.where(blk_ids[None, :] <= cur[:, None], imp, -1.0)
        _, idx = lax.top_k(imp, n_sel)
        kg = ks_blk[bi, hi, idx]
        vg = vs_blk[bi, hi, idx]
        tok = idx[..., None] * NSA_BLOCK + jnp.arange(NSA_BLOCK)
        sel_mask = (tok <= qpos[None, None, :, None, None]).reshape(B, hk, c, n_sel * NSA_BLOCK)[:, :, None]
        s2 = jnp.einsum("bqhgd,bhqkld->bhgqkl", qr.reshape(B, c, hk, grp, HEAD_DIM), kg)
        s2 = s2.reshape(B, hk, grp, c, n_sel * NSA_BLOCK).astype(jnp.float32) * scale
        ps = masked_softmax(s2, sel_mask).reshape(B, hk, grp, c, n_sel, NSA_BLOCK)
        o_slc = jnp.einsum("bhgqkl,bhqkld->bqhgd", ps.astype(vg.dtype), vg)
        return o_cmp.reshape(B, c, NSA_HEADS, HEAD_DIM), o_slc.reshape(B, c, NSA_HEADS, HEAD_DIM)

    c = min(Q_CHUNK, S)
    n = S // c
    o_cmp, o_slc = lax.map(cmp_slc_chunk, (to_chunks(q, n), to_chunks(q_rot, n), pos.reshape(n, c)))
    o_cmp, o_slc = from_chunks(o_cmp), from_chunks(o_slc)

    if past_win is None:
        pad = jnp.pad(win_rows, ((0, 0), (NSA_WINDOW, 0), (0, 0), (0, 0), (0, 0)))

        def win_chunk(args):
            qr, qpos, i = args
            kw = lax.dynamic_slice_in_dim(pad, i * c, c + NSA_WINDOW, axis=1)
            kpos = i * c - NSA_WINDOW + jnp.arange(c + NSA_WINDOW)
            return window_attn(qr, kw[:, :, 0], kw[:, :, 1], qpos, kpos)

        o_win = from_chunks(lax.map(win_chunk, (to_chunks(q_rot, n), pos.reshape(n, c), jnp.arange(n))))
        new_win = win_rows[:, -min(NSA_WINDOW, S):]
    else:
        w_buf = past_win.shape[1]
        kw = jnp.concatenate([past_win, win_rows], axis=1)
        kpos = pos[0] - w_buf + jnp.arange(w_buf + S)
        o_win = window_attn(q_rot, kw[:, :, 0], kw[:, :, 1], pos, kpos)
        new_win = kw[:, -w_buf:]

    gates = jax.nn.sigmoid(gate_raw.reshape(B, S, NSA_HEADS, 3))
    o = gates[..., 0:1] * o_cmp + gates[..., 1:2] * o_slc + gates[..., 2:3] * o_win
    return o.reshape(B, S, NSA_Q_W) @ p["nsa_out"], rows, new_win


def diff_mixer(q_raw, k_raw, v_raw, pos, l, p, past_rows):
    B, S, _ = q_raw.shape
    scale = HEAD_DIM ** -0.5
    lam_init = 0.8 - 0.6 * math.exp(-0.3 * l)
    lp = p["diff_lambda"].astype(jnp.float32)
    lam = jnp.exp(jnp.sum(lp[0] * lp[1])) - jnp.exp(jnp.sum(lp[2] * lp[3])) + lam_init
    q = rope(rms_norm(q_raw.reshape(B, S, DIFF_HEADS, 2, HEAD_DIM), p["diff_qk_g"][0]), pos)
    k = rope(rms_norm(k_raw.reshape(B, S, DIFF_HEADS, 2, HEAD_DIM), p["diff_qk_g"][1]), pos)
    v = v_raw.reshape(B, S, DIFF_HEADS, 2 * HEAD_DIM)
    rows = jnp.stack([k.reshape(B, S, DIFF_HEADS, 2 * HEAD_DIM), v], axis=2)
    keys = rows if past_rows is None else jnp.concatenate([past_rows, rows], axis=1)
    t = keys.shape[1]
    kk = keys[:, :, 0].reshape(B, t, DIFF_HEADS, 2, HEAD_DIM)
    k1, k2, vv = kk[..., 0, :], kk[..., 1, :], keys[:, :, 1]
    kpos = jnp.arange(t)

    def chunk(args):
        qc, qpos = args
        mask = kpos[None, :] <= qpos[:, None]
        s1 = jnp.einsum("bqhd,bkhd->bhqk", qc[..., 0, :], k1).astype(jnp.float32) * scale
        s2 = jnp.einsum("bqhd,bkhd->bhqk", qc[..., 1, :], k2).astype(jnp.float32) * scale
        a = masked_softmax(s1, mask) - lam * masked_softmax(s2, mask)
        return jnp.einsum("bhqk,bkhd->bqhd", a.astype(vv.dtype), vv)

    c = min(Q_CHUNK, S)
    n = S // c
    o = from_chunks(lax.map(chunk, (to_chunks(q, n), pos.reshape(n, c))))
    o = rms_norm(o, p["diff_out_g"]) * (1.0 - lam_init)
    return o.reshape(B, S, DIFF_W) @ p["diff_out"], rows


def token_mixer(h, pos, l, p, past):
    B, S, _ = h.shape
    z = h @ p["w_mix_in"]
    a_b, a_c, a_x, nsa_q, nsa_kv, nsa_g, d_q, d_k, d_v, glu_in, merge_in = split_cols(z)
    pa = jnp.zeros((B, A_CONV - 1, A_WIDTH), h.dtype) if past is None else past["conv_a"]
    ya, conv_a_state = causal_dwconv(a_c * a_x, pa, p["conv_a_w"], p["conv_a_b"])
    y_a = (a_b * ya) @ p["a_out"]
    y_b, nsa_rows, win_state = nsa_mixer(nsa_q, nsa_kv, nsa_g, pos, p,
                                         None if past is None else past["nsa"],
                                         None if past is None else past["win"])
    y_c, diff_rows = diff_mixer(d_q, d_k, d_v, pos, l, p, None if past is None else past["diff"])
    glu = glu_in[..., :D_WIDTH] * jax.nn.sigmoid(glu_in[..., D_WIDTH:])
    pd = jnp.zeros((B, D_CONV - 1, D_WIDTH), h.dtype) if past is None else past["conv_d"]
    yd, conv_d_state = causal_dwconv(glu, pd, p["conv_d_w"], p["conv_d_b"])
    y_d = jax.nn.silu(layer_norm(yd, p["d_norm_g"], p["d_norm_b"])) @ p["d_out"]
    g = jax.nn.sigmoid(merge_in.reshape(B, S, N_BRANCH, D_MODEL))
    merged = g[:, :, 0] * y_a + g[:, :, 1] * y_b + g[:, :, 2] * y_c + g[:, :, 3] * y_d
    return merged @ p["w_out"], (nsa_rows, diff_rows, win_state, conv_a_state, conv_d_state)


def modulate(x, g, shift, scale):
    return rms_norm(x, g) * (1.0 + scale) + shift


def layer_forward(x, c, pos, l, p, past):
    ada = (jax.nn.silu(c) @ p["ada_w"] + p["ada_b"]).reshape(c.shape[0], N_ADA, 1, D_MODEL)
    h = modulate(x, p["norm_g"][0], ada[:, 0], ada[:, 1])
    x = x + 0.5 * ada[:, 2] * swiglu(h, p["ffn1_up"], p["ffn1_down"])
    h = modulate(x, p["norm_g"][1], ada[:, 3], ada[:, 4])
    y, states = token_mixer(h, pos, l, p, past)
    x = x + ada[:, 5] * y
    h = modulate(x, p["norm_g"][2], ada[:, 6], ada[:, 7])
    x = x + 0.5 * ada[:, 8] * swiglu(h, p["ffn2_up"], p["ffn2_down"])
    return x, states


def setup_inputs(seed: int = 0) -> dict:
    key = jax.random.key(seed)
    ks = list(jax.random.split(key, 48))

    def nrm(shape, scale):
        return jax.random.normal(ks.pop(), shape, jnp.float32) * scale

    d = D_MODEL
    n_pages = PAST_LEN // PAGE_SIZE
    n_used = DEC_BATCH * n_pages
    n_phys = n_used + n_used // 4
    w_buf = min(NSA_WINDOW, PAST_LEN)
    page_table = jax.random.permutation(ks.pop(), n_phys)[:n_used].reshape(DEC_BATCH, n_pages).astype(jnp.int32)
    return {
        "x_prompt": nrm((BATCH, SEQ, d), 1.0),
        "x_sample": nrm((DEC_BATCH, DEC_SEQ, d), 1.0),
        "cache_nsa": nrm((DEPTH, n_phys, PAGE_SIZE, 4, NSA_KV_HEADS, HEAD_DIM), 1.0),
        "cache_diff": nrm((DEPTH, n_phys, PAGE_SIZE, 2, DIFF_HEADS, 2 * HEAD_DIM), 1.0),
        "state_nsa_win": nrm((DEPTH, DEC_BATCH, w_buf, 2, NSA_KV_HEADS, HEAD_DIM), 1.0),
        "state_conv_a": nrm((DEPTH, DEC_BATCH, A_CONV - 1, A_WIDTH), 1.0),
        "state_conv_d": nrm((DEPTH, DEC_BATCH, D_CONV - 1, D_WIDTH), 0.5),
        "page_table": page_table,
        "c_prompt": nrm((BATCH, d), 1.0),
        "c_sample": nrm((DEC_BATCH, d), 1.0),
        "ada_w": nrm((DEPTH, d, N_ADA * d), 0.3 * d ** -0.5),
        "ada_b": nrm((DEPTH, N_ADA * d), 0.02),
        "norm_g": 1.0 + nrm((DEPTH, 3, d), 0.05),
        "ffn1_up": nrm((DEPTH, d, 2 * D_FF), d ** -0.5),
        "ffn1_down": nrm((DEPTH, D_FF, d), D_FF ** -0.5),
        "ffn2_up": nrm((DEPTH, d, 2 * D_FF), d ** -0.5),
        "ffn2_down": nrm((DEPTH, D_FF, d), D_FF ** -0.5),
        "w_mix_in": nrm((DEPTH, d, MIX_IN_W), d ** -0.5),
        "conv_a_w": nrm((DEPTH, A_CONV, A_WIDTH), A_CONV ** -0.5),
        "conv_a_b": nrm((DEPTH, A_WIDTH), 0.02),
        "a_out": nrm((DEPTH, A_WIDTH, d), A_WIDTH ** -0.5),
        "nsa_qk_g": 1.0 + nrm((DEPTH, 4, HEAD_DIM), 0.05),
        "nsa_cmp_w": (1.0 + nrm((DEPTH, 2, NSA_BLOCK, HEAD_DIM), 0.1)) / NSA_BLOCK,
        "nsa_out": nrm((DEPTH, NSA_Q_W, d), NSA_Q_W ** -0.5),
        "diff_qk_g": 1.0 + nrm((DEPTH, 2, HEAD_DIM), 0.05),
        "diff_lambda": nrm((DEPTH, 4, HEAD_DIM), 0.1),
        "diff_out_g": 1.0 + nrm((DEPTH, 2 * HEAD_DIM), 0.05),
        "diff_out": nrm((DEPTH, DIFF_W, d), DIFF_W ** -0.5),
        "conv_d_w": nrm((DEPTH, D_CONV, D_WIDTH), D_CONV ** -0.5),
        "conv_d_b": nrm((DEPTH, D_WIDTH), 0.02),
        "d_norm_g": 1.0 + nrm((DEPTH, D_WIDTH), 0.05),
        "d_norm_b": nrm((DEPTH, D_WIDTH), 0.02),
        "d_out": nrm((DEPTH, D_WIDTH, d), D_WIDTH ** -0.5),
        "w_out": nrm((DEPTH, d, d), d ** -0.5),
    }


def reference(x_prompt, x_sample, cache_nsa, cache_diff, state_nsa_win, state_conv_a, state_conv_d,
              page_table, c_prompt, c_sample, ada_w, ada_b, norm_g, ffn1_up, ffn1_down, ffn2_up, ffn2_down,
              w_mix_in, conv_a_w, conv_a_b, a_out, nsa_qk_g, nsa_cmp_w, nsa_out, diff_qk_g, diff_lambda,
              diff_out_g, diff_out, conv_d_w, conv_d_b, d_norm_g, d_norm_b, d_out, w_out):
    pos_p = jnp.arange(x_prompt.shape[1], dtype=jnp.int32)
    pos_s = PAST_LEN + jnp.arange(x_sample.shape[1], dtype=jnp.int32)
    xp, xs = x_prompt, x_sample
    st_p, st_s = [], []
    for l in range(DEPTH):
        p = {
            "ada_w": ada_w[l], "ada_b": ada_b[l], "norm_g": norm_g[l],
            "ffn1_up": ffn1_up[l], "ffn1_down": ffn1_down[l], "ffn2_up": ffn2_up[l], "ffn2_down": ffn2_down[l],
            "w_mix_in": w_mix_in[l], "conv_a_w": conv_a_w[l], "conv_a_b": conv_a_b[l], "a_out": a_out[l],
            "nsa_qk_g": nsa_qk_g[l], "nsa_cmp_w": nsa_cmp_w[l], "nsa_out": nsa_out[l],
            "diff_qk_g": diff_qk_g[l], "diff_lambda": diff_lambda[l], "diff_out_g": diff_out_g[l],
            "diff_out": diff_out[l], "conv_d_w": conv_d_w[l], "conv_d_b": conv_d_b[l],
            "d_norm_g": d_norm_g[l], "d_norm_b": d_norm_b[l], "d_out": d_out[l], "w_out": w_out[l],
        }
        past = {
            "nsa": gather_pages(cache_nsa[l], page_table),
            "diff": gather_pages(cache_diff[l], page_table),
            "win": state_nsa_win[l],
            "conv_a": state_conv_a[l],
            "conv_d": state_conv_d[l],
        }
        xp, sp = layer_forward(xp, c_prompt, pos_p, l, p, None)
        xs, ss = layer_forward(xs, c_sample, pos_s, l, p, past)
        st_p.append(sp)
        st_s.append(ss)

    def stk(states, i):
        return jnp.stack([s[i] for s in states], axis=0)

    return (xp, xs,
            stk(st_p, 0), stk(st_s, 0),
            stk(st_p, 1), stk(st_s, 1),
            stk(st_p, 2), stk(st_s, 2),
            stk(st_p, 3), stk(st_s, 3),
            stk(st_p, 4), stk(st_s, 4))
```

```python
import functools
import math

import jax
import jax.numpy as jnp
from jax import lax
from jax.experimental import pallas as pl
from jax.experimental.pallas import tpu as pltpu

F32 = jnp.float32
BF16 = jnp.bfloat16

HEAD_DIM = 64
ROPE_THETA = 10000.0
NORM_EPS = 1e-6
A_WIDTH = 512
A_CONV = 3
NSA_HEADS = 8
NSA_KV_HEADS = 2
NSA_GROUP = NSA_HEADS // NSA_KV_HEADS
NSA_BLOCK = 64
NSA_TOPK = 16
NSA_WINDOW = 512
DIFF_HEADS = 4
D_WIDTH = 512
D_CONV = 31
N_BRANCH = 4
N_ADA = 9
LANES = 128
NEG = -1e30
ATT_SCALE = HEAD_DIM ** -0.5
VMEM_LIMIT = 56 * 1024 * 1024

C_KV = 0
C_MERGE = 1024
C_GLU = 5120
C_AB = 6144
C_AC = 6656
C_AX = 7168
C_NQ = 7680
C_DQ = 8192
C_DK = 8704
C_DV = 9216
C_GC = 9728
C_GS = 10240
C_GW = 10752
MIX_W = 11264


def _cparams(sem):
    return pltpu.CompilerParams(dimension_semantics=sem, vmem_limit_bytes=VMEM_LIMIT)


def _sigmoid(x):
    return 1.0 / (1.0 + jnp.exp(-x))


def _silu(x):
    return x * _sigmoid(x)


def _rows_to(m, tm):
    r = m.shape[0]
    if r == 1 or r == tm:
        return m
    return jnp.concatenate([m] * (tm // r), axis=0)


def _modulate(x, g, shift, scale):
    ms = jnp.mean(x * x, axis=-1, keepdims=True)
    y = x * lax.rsqrt(ms + NORM_EPS) * g
    return y * (1.0 + scale) + shift


def _dot(a, b):
    return jnp.dot(a, b, preferred_element_type=F32)


def _dot_nt(a, b, precision=None):
    return lax.dot_general(a, b, (((1,), (1,)), ((), ())), preferred_element_type=F32, precision=precision)


def _lane_lo(shape):
    return (lax.broadcasted_iota(jnp.int32, shape, len(shape) - 1) % LANES) < HEAD_DIM


def _dup_halves(x, lo):
    xr = pltpu.roll(x, HEAD_DIM, 1)
    return jnp.where(lo, x, xr), jnp.where(lo, xr, x)


def _ada_kernel(c_ref, w_ref, b_ref, o_ref):
    s = _silu(c_ref[...])
    o_ref[...] = _dot(s.astype(BF16), w_ref[...].astype(BF16)) + b_ref[...]


def ada_proj(c_all, ada_w, ada_b):
    r, d = c_all.shape
    nl = ada_w.shape[0]
    return pl.pallas_call(
        _ada_kernel,
        grid=(nl, N_ADA),
        in_specs=[
            pl.BlockSpec((r, d), lambda l, k: (0, 0)),
            pl.BlockSpec((None, d, d), lambda l, k: (l, 0, k)),
            pl.BlockSpec((None, None, 1, d), lambda l, k: (l, k, 0, 0)),
        ],
        out_specs=pl.BlockSpec((None, None, r, d), lambda l, k: (l, k, 0, 0)),
        out_shape=jax.ShapeDtypeStruct((nl, N_ADA, r, d), F32),
        compiler_params=_cparams(("arbitrary", "arbitrary")),
        name="ada_proj",
    )(c_all, ada_w, ada_b.reshape(nl, N_ADA, 1, d))


def _mod_spec(mod, k, tiles_per_group, ngrid):
    _, _, r, d = mod.shape
    if ngrid == 1:
        return pl.BlockSpec((None, None, r, d), lambda i: (k, i // tiles_per_group, 0, 0))
    return pl.BlockSpec((None, None, r, d), lambda i, j: (k, i // tiles_per_group, 0, 0))


def _ffn_kernel(x_ref, sh_ref, sc_ref, gt_ref, g_ref, wa_ref, wv_ref, wd_ref, o_ref, h_ref, acc_ref, *, nj):
    j = pl.program_id(1)
    tm = x_ref.shape[0]

    @pl.when(j == 0)
    def _():
        h = _modulate(x_ref[...], g_ref[...], _rows_to(sh_ref[...], tm), _rows_to(sc_ref[...], tm))
        h_ref[...] = h.astype(BF16)
        acc_ref[...] = jnp.zeros_like(acc_ref)

    h = h_ref[...]
    a = _dot(h, wa_ref[...])
    v = _dot(h, wv_ref[...])
    act = (_silu(a) * v).astype(BF16)
    acc_ref[...] += _dot(act, wd_ref[...])

    @pl.when(j == nj - 1)
    def _():
        o_ref[...] = x_ref[...] + 0.5 * _rows_to(gt_ref[...], tm) * acc_ref[...]


def ffn(x, mod, k0, g, w_up, w_down, tm, tiles_per_group):
    t, d = x.shape
    f = w_down.shape[0]
    nj = 2
    tf = f // nj
    return pl.pallas_call(
        functools.partial(_ffn_kernel, nj=nj),
        grid=(t // tm, nj),
        in_specs=[
            pl.BlockSpec((tm, d), lambda i, j: (i, 0)),
            _mod_spec(mod, k0, tiles_per_group, 2),
            _mod_spec(mod, k0 + 1, tiles_per_group, 2),
            _mod_spec(mod, k0 + 2, tiles_per_group, 2),
            pl.BlockSpec((1, d), lambda i, j: (0, 0)),
            pl.BlockSpec((d, tf), lambda i, j: (0, j)),
            pl.BlockSpec((d, tf), lambda i, j: (0, nj + j)),
            pl.BlockSpec((tf, d), lambda i, j: (j, 0)),
        ],
        out_specs=pl.BlockSpec((tm, d), lambda i, j: (i, 0)),
        out_shape=jax.ShapeDtypeStruct((t, d), F32),
        scratch_shapes=[pltpu.VMEM((tm, d), BF16), pltpu.VMEM((tm, d), F32)],
        compiler_params=_cparams(("arbitrary", "arbitrary")),
        name="ffn",
    )(x, mod, mod, mod, g, w_up, w_up, w_down)


def _mix_kernel(x_ref, sh_ref, sc_ref, g_ref, w_ref, o_ref, h_ref):
    tm = x_ref.shape[0]

    @pl.when(pl.program_id(1) == 0)
    def _():
        h = _modulate(x_ref[...], g_ref[...], _rows_to(sh_ref[...], tm), _rows_to(sc_ref[...], tm))
        h_ref[...] = h.astype(BF16)

    o_ref[...] = _dot(h_ref[...], w_ref[...])


def mix_in(x, mod, g, w_mix, tm, tiles_per_group, tn=1024):
    t, d = x.shape
    nw = w_mix.shape[1]
    return pl.pallas_call(
        _mix_kernel,
        grid=(t // tm, nw // tn),
        in_specs=[
            pl.BlockSpec((tm, d), lambda i, j: (i, 0)),
            _mod_spec(mod, 3, tiles_per_group, 2),
            _mod_spec(mod, 4, tiles_per_group, 2),
            pl.BlockSpec((1, d), lambda i, j: (0, 0)),
            pl.BlockSpec((d, tn), lambda i, j: (0, j)),
        ],
        out_specs=pl.BlockSpec((tm, tn), lambda i, j: (i, j)),
        out_shape=jax.ShapeDtypeStruct((t, nw), F32),
        scratch_shapes=[pltpu.VMEM((tm, d), BF16)],
        compiler_params=_cparams(("arbitrary", "arbitrary")),
        name="mix_in",
    )(x, mod, mod, g, w_mix)


def _group_norm(x, gain, gmat):
    x2 = x * x
    hi = x2.astype(BF16)
    lo = (x2 - hi.astype(F32)).astype(BF16)
    ss = _dot(hi, gmat) + _dot(lo, gmat)
    return x * lax.rsqrt(ss * (1.0 / HEAD_DIM) + NORM_EPS) * gain


def _rope(x, cos, sin_signed, lo):
    sw = jnp.where(lo, pltpu.roll(x, LANES - HEAD_DIM // 2, 1), pltpu.roll(x, HEAD_DIM // 2, 1))
    return x * cos + sw * sin_signed


def _prep_kernel(kv_ref, nq_ref, dq_ref, dk_ref, dv_ref, cos_ref, sin_ref, gain_ref, gmat_ref,
                 qn_ref, qr_ref, rows_ref, win_ref, dqo_ref, drows_ref):
    cos = cos_ref[...]
    sin = sin_ref[...]
    gmat = gmat_ref[...]
    lo32 = (lax.broadcasted_iota(jnp.int32, cos.shape, 1) % HEAD_DIM) < HEAD_DIM // 2

    def gain(r):
        return gain_ref[r:r + 1, :]

    def piece(ref, p):
        return ref[:, p * LANES:(p + 1) * LANES]

    for p in range(4):
        qn = _group_norm(piece(nq_ref, p), gain(0), gmat)
        qn_ref[:, p * LANES:(p + 1) * LANES] = qn
        qr_ref[:, p * LANES:(p + 1) * LANES] = _rope(qn, cos, sin, lo32)
    rows_ref[:, 0:LANES] = _group_norm(piece(kv_ref, 0), gain(1), gmat)
    rows_ref[:, LANES:2 * LANES] = piece(kv_ref, 1)
    rows_ref[:, 2 * LANES:3 * LANES] = _rope(_group_norm(piece(kv_ref, 2), gain(2), gmat), cos, sin, lo32)
    rows_ref[:, 3 * LANES:4 * LANES] = piece(kv_ref, 3)
    win_ref[:, 0:LANES] = _rope(_group_norm(piece(kv_ref, 4), gain(3), gmat), cos, sin, lo32)
    win_ref[:, LANES:2 * LANES] = piece(kv_ref, 5)
    for p in range(4):
        dqo_ref[:, p * LANES:(p + 1) * LANES] = _rope(_group_norm(piece(dq_ref, p), gain(4), gmat), cos, sin, lo32)
        drows_ref[:, p * LANES:(p + 1) * LANES] = _rope(_group_norm(piece(dk_ref, p), gain(5), gmat), cos, sin, lo32)
    drows_ref[:, 4 * LANES:8 * LANES] = dv_ref[...]


def qk_prep(z, cos, sin, gains, gmat, tm, tiles_per_seq):
    t = z.shape[0]

    def zs(width, col):
        return pl.BlockSpec((tm, width), lambda i: (i, col // width))

    def os_(width):
        return pl.BlockSpec((tm, width), lambda i: (i, 0))

    tab = pl.BlockSpec((tm, LANES), lambda i: (i % tiles_per_seq, 0))
    return pl.pallas_call(
        _prep_kernel,
        grid=(t // tm,),
        in_specs=[zs(768, C_KV), zs(512, C_NQ), zs(512, C_DQ), zs(512, C_DK), zs(512, C_DV), tab, tab,
                  pl.BlockSpec((8, LANES), lambda i: (0, 0)), pl.BlockSpec((LANES, LANES), lambda i: (0, 0))],
        out_specs=[os_(512), os_(512), os_(512), os_(256), os_(512), os_(1024)],
        out_shape=[jax.ShapeDtypeStruct((t, w), F32) for w in (512, 512, 512, 256, 512, 1024)],
        compiler_params=_cparams(("arbitrary",)),
        name="qk_prep",
    )(z, z, z, z, z, cos, sin, gains, gmat)


def _conv_kernel(ab_ref, ac_ref, ax_ref, glu_ref, pa_ref, pd_ref, wa_ref, ba_ref, wd_ref, bd_ref, lg_ref, lb_ref,
                 ua_ref, ud_ref, sa_ref, sd_ref, bufa, bufd, *, tm, ha, hd, stride, nt):
    i = pl.program_id(1)

    @pl.when(i == 0)
    def _():
        bufa[0:ha, :] = pa_ref[...]
        bufd[0:hd, :] = pd_ref[...]

    if nt > 1:
        @pl.when(i > 0)
        def _():
            bufa[0:ha, :] = bufa[tm:tm + ha, :]
            bufd[0:hd, :] = bufd[tm:tm + hd, :]

    bufa[ha:ha + tm, :] = ac_ref[...] * ax_ref[...]
    ya = ba_ref[...]
    for k in range(A_CONV):
        ya = ya + wa_ref[k:k + 1, :] * bufa[pl.ds(ha - (A_CONV - 1 - k) * stride, tm), :]
    ua_ref[...] = (ab_ref[...] * ya).astype(ua_ref.dtype)

    glu = glu_ref[...]
    bufd[hd:hd + tm, :] = glu[:, :D_WIDTH] * _sigmoid(glu[:, D_WIDTH:])
    yd = bd_ref[...]
    for k in range(D_CONV):
        yd = yd + wd_ref[k:k + 1, :] * bufd[pl.ds(hd - (D_CONV - 1 - k) * stride, tm), :]
    mu = jnp.mean(yd, axis=-1, keepdims=True)
    var = jnp.mean(jnp.square(yd - mu), axis=-1, keepdims=True)
    y = (yd - mu) * lax.rsqrt(var + NORM_EPS) * lg_ref[...] + lb_ref[...]
    ud_ref[...] = _silu(y).astype(ud_ref.dtype)

    @pl.when(i == nt - 1)
    def _():
        sa_ref[...] = bufa[ha + tm - (A_CONV - 1) * stride:ha + tm, :]
        sd_ref[...] = bufd[hd + tm - (D_CONV - 1) * stride:hd + tm, :]


def conv_branches(z, past_a, past_d, wa, ba, wd, bd, lg, lb, nseq, tm, stride):
    t = z.shape[0]
    nt = t // nseq // tm
    ha, hd = past_a.shape[1], past_d.shape[1]
    if nt > 1:
        assert tm >= ha and tm >= hd

    def zs(width, col):
        return pl.BlockSpec((tm, width), lambda s, i: (s * nt + i, col // width))

    def cs(shape):
        return pl.BlockSpec(shape, lambda s, i: (0, 0))

    sa_rows, sd_rows = (A_CONV - 1) * stride, (D_CONV - 1) * stride
    return pl.pallas_call(
        functools.partial(_conv_kernel, tm=tm, ha=ha, hd=hd, stride=stride, nt=nt),
        grid=(nseq, nt),
        in_specs=[zs(512, C_AB), zs(512, C_AC), zs(512, C_AX), zs(1024, C_GLU),
                  pl.BlockSpec((None, ha, A_WIDTH), lambda s, i: (s, 0, 0)),
                  pl.BlockSpec((None, hd, D_WIDTH), lambda s, i: (s, 0, 0)),
                  cs((A_CONV, A_WIDTH)), cs((1, A_WIDTH)), cs((D_CONV, D_WIDTH)), cs((1, D_WIDTH)),
                  cs((1, D_WIDTH)), cs((1, D_WIDTH))],
        out_specs=[pl.BlockSpec((tm, A_WIDTH), lambda s, i: (s * nt + i, 0)),
                   pl.BlockSpec((tm, D_WIDTH), lambda s, i: (s * nt + i, 0)),
                   pl.BlockSpec((None, sa_rows, A_WIDTH), lambda s, i: (s, 0, 0)),
                   pl.BlockSpec((None, sd_rows, D_WIDTH), lambda s, i: (s, 0, 0))],
        out_shape=[jax.ShapeDtypeStruct((t, A_WIDTH), BF16), jax.ShapeDtypeStruct((t, D_WIDTH), BF16),
                   jax.ShapeDtypeStruct((nseq, sa_rows, A_WIDTH), F32),
                   jax.ShapeDtypeStruct((nseq, sd_rows, D_WIDTH), F32)],
        scratch_shapes=[pltpu.VMEM((ha + tm, A_WIDTH), F32), pltpu.VMEM((hd + tm, D_WIDTH), F32)],
        compiler_params=_cparams(("arbitrary", "arbitrary")),
        name="conv_branches",
    )(z, z, z, z, past_a, past_d, wa, ba, wd, bd, lg, lb)


def _merge_kernel(x_ref, gt_ref, ua_ref, uc_ref, ud_ref, oc_ref, os_ref, ow_ref, gc_ref, gs_ref, gw_ref,
                  m0_ref, m1_ref, m2_ref, m3_ref, wa_ref, wb_ref, wc_ref, wd_ref, wo_ref, o_ref):
    tm = x_ref.shape[0]
    ub = (_sigmoid(gc_ref[...]) * oc_ref[...] + _sigmoid(gs_ref[...]) * os_ref[...]
          + _sigmoid(gw_ref[...]) * ow_ref[...]).astype(BF16)
    merged = _sigmoid(m0_ref[...]) * _dot(ua_ref[...], wa_ref[...])
    merged += _sigmoid(m1_ref[...]) * _dot(ub, wb_ref[...])
    merged += _sigmoid(m2_ref[...]) * _dot(uc_ref[...], wc_ref[...])
    merged += _sigmoid(m3_ref[...]) * _dot(ud_ref[...], wd_ref[...])
    y = _dot(merged.astype(BF16), wo_ref[...])
    o_ref[...] = x_ref[...] + _rows_to(gt_ref[...], tm) * y


def merge_out(x, mod, z, ua, uc, ud, o_cmp, o_slc, o_win, w_a, w_b, w_c, w_d, w_o, tm, tiles_per_group):
    t, d = x.shape

    def ts(width):
        return pl.BlockSpec((tm, width), lambda i: (i, 0))

    def zs(width, col):
        return pl.BlockSpec((tm, width), lambda i: (i, col // width))

    def ws(shape):
        return pl.BlockSpec(shape, lambda i: (0, 0))

    return pl.pallas_call(
        _merge_kernel,
        grid=(t // tm,),
        in_specs=[ts(d), _mod_spec(mod, 5, tiles_per_group, 1), ts(512), ts(512), ts(512), ts(512), ts(512), ts(512),
                  zs(512, C_GC), zs(512, C_GS), zs(512, C_GW),
                  zs(1024, C_MERGE), zs(1024, C_MERGE + 1024), zs(1024, C_MERGE + 2048), zs(1024, C_MERGE + 3072),
                  ws((512, d)), ws((512, d)), ws((512, d)), ws((512, d)), ws((d, d))],
        out_specs=ts(d),
        out_shape=jax.ShapeDtypeStruct((t, d), F32),
        compiler_params=_cparams(("arbitrary",)),
        name="merge_out",
    )(x, mod, ua, uc, ud, o_cmp, o_slc, o_win, z, z, z, z, z, z, z, w_a, w_b, w_c, w_d, w_o)


def _compress_kernel(r_ref, w_ref, o_ref):
    x = r_ref[...]
    nb = x.shape[0] // NSA_BLOCK
    o_ref[...] = jnp.sum(x.reshape(nb, NSA_BLOCK, x.shape[1]) * w_ref[...][None], axis=1)


def nsa_compress(rows, w2, tm=512):
    t = rows.shape[0]
    return pl.pallas_call(
        _compress_kernel,
        grid=(t // tm,),
        in_specs=[pl.BlockSpec((tm, 256), lambda i: (i, 0)), pl.BlockSpec((NSA_BLOCK, 256), lambda i: (0, 0))],
        out_specs=pl.BlockSpec((tm // NSA_BLOCK, 256), lambda i: (i, 0)),
        out_shape=jax.ShapeDtypeStruct((t // NSA_BLOCK, 256), F32),
        compiler_params=_cparams(("arbitrary",)),
        name="nsa_compress",
    )(rows, w2)


def _masked_softmax(s, mask):
    s = jnp.where(mask, s, NEG)
    p = jnp.exp(s - jnp.max(s, axis=-1, keepdims=True))
    p = jnp.where(mask, p, 0.0)
    return p / jnp.maximum(jnp.sum(p, axis=-1, keepdims=True), 1e-30)


def _top_blocks(imp, n_sel):
    idx = lax.broadcasted_iota(jnp.int32, imp.shape, 1)
    big = imp.shape[1]
    sel = jnp.zeros(imp.shape, F32)
    for _ in range(n_sel):
        m = jnp.max(imp, axis=-1, keepdims=True)
        first = jnp.min(jnp.where(imp == m, idx, big), axis=-1, keepdims=True)
        hit = idx == first
        sel = jnp.where(hit, 1.0, sel)
        imp = jnp.where(hit, -3.0, imp)
    return sel


def _cmp_select(q, cmp, qpos, n_real, n_sel):
    tq = q.shape[0]
    nb = cmp.shape[0]
    lo = _lane_lo((nb, LANES))
    kdup = _dup_halves(cmp[:, 0:LANES], lo)
    vdup = _dup_halves(cmp[:, LANES:2 * LANES], lo)
    qlo = _lane_lo((tq, LANES))
    blk = lax.broadcasted_iota(jnp.int32, (1, nb), 1)
    complete = ((blk + 1) * NSA_BLOCK - 1 <= qpos) & (blk < n_real)
    cur = qpos // NSA_BLOCK
    outs, sels = [], []
    for hk in range(NSA_KV_HEADS):
        imp = jnp.zeros((tq, nb), F32)
        vd = vdup[hk].astype(BF16)
        for a in (2 * hk, 2 * hk + 1):
            qp = q[:, a * LANES:(a + 1) * LANES]
            pair = []
            for par in range(2):
                qm = jnp.where(qlo if par == 0 else ~qlo, qp, 0.0)
                s = _dot_nt(qm, kdup[hk], precision=lax.Precision.HIGHEST) * ATT_SCALE
                pc = _masked_softmax(s, complete)
                imp = imp + pc
                pair.append(_dot(pc.astype(BF16), vd))
            outs.append(jnp.where(qlo, pair[0], pair[1]))
        forced = (blk == cur) | (blk == 0)
        imp = jnp.where(forced, NSA_GROUP + 1.0, imp)
        imp = jnp.where(blk <= cur, imp, -1.0)
        imp = jnp.where(blk < n_real, imp, -2.0)
        sels.append(_top_blocks(imp, n_sel))
    return jnp.concatenate(outs, axis=1), sels


def _cmpsel_kernel(q_ref, c_ref, o_ref, sel_ref, *, tq, n_sel):
    qpos = pl.program_id(1) * tq + lax.broadcasted_iota(jnp.int32, (tq, 1), 0)
    nb = c_ref.shape[0]
    o, sels = _cmp_select(q_ref[...], c_ref[...], qpos, nb, n_sel)
    o_ref[...] = o
    sel_ref[...] = jnp.concatenate(sels, axis=1).astype(sel_ref.dtype)


def nsa_cmp_select(q_n, cmp, nseq, tq=256):
    t = q_n.shape[0]
    nq = t // nseq // tq
    nb = cmp.shape[0] // nseq
    return pl.pallas_call(
        functools.partial(_cmpsel_kernel, tq=tq, n_sel=min(NSA_TOPK, nb)),
        grid=(nseq, nq),
        in_specs=[pl.BlockSpec((tq, 512), lambda b, i: (b * nq + i, 0)),
                  pl.BlockSpec((nb, 256), lambda b, i: (b, 0))],
        out_specs=[pl.BlockSpec((tq, 512), lambda b, i: (b * nq + i, 0)),
                   pl.BlockSpec((tq, 2 * nb), lambda b, i: (b * nq + i, 0))],
        out_shape=[jax.ShapeDtypeStruct((t, 512), F32), jax.ShapeDtypeStruct((t, 2 * nb), BF16)],
        compiler_params=_cparams(("arbitrary", "arbitrary")),
        name="nsa_cmp_select",
    )(q_n, cmp)


def _flash_update(s, mask, v, m_ref, l_ref, acc_ref):
    if mask is not None:
        s = jnp.where(mask, s, NEG)
    m_prev = m_ref[...]
    m_new = jnp.maximum(m_prev, jnp.max(s, axis=-1, keepdims=True))
    p = jnp.exp(s - m_new)
    if mask is not None:
        p = jnp.where(mask, p, 0.0)
    alpha = jnp.exp(m_prev - m_new)
    l_ref[...] = alpha * l_ref[...] + jnp.sum(p, axis=-1, keepdims=True)
    acc_ref[...] = alpha * acc_ref[...] + _dot(p.astype(BF16), v)
    m_ref[...] = m_new


def _slc_kernel(q_ref, k_ref, v_ref, sel_ref, e_ref, o_ref, m_ref, l_ref, acc_ref, *, tq, tk, nb):
    qi, kj = pl.program_id(1), pl.program_id(2)
    last = (qi * tq + tq - 1) // tk

    @pl.when(kj == 0)
    def _():
        m_ref[...] = jnp.full(m_ref.shape, NEG, F32)
        l_ref[...] = jnp.zeros_like(l_ref)
        acc_ref[...] = jnp.zeros_like(acc_ref)

    @pl.when(kj <= last)
    def _():
        lo = _lane_lo((tk, LANES))
        kdup = _dup_halves(k_ref[...], lo)
        vdup = _dup_halves(v_ref[...], lo)
        qlo = _lane_lo((tq, LANES))
        qpos = qi * tq + lax.broadcasted_iota(jnp.int32, (tq, 1), 0)
        kpos = kj * tk + lax.broadcasted_iota(jnp.int32, (1, tk), 1)
        causal = kpos <= qpos
        e = e_ref[...]
        for hk in range(NSA_KV_HEADS):
            mask = (_dot(sel_ref[:, hk * nb:(hk + 1) * nb], e) > 0.5) & causal
            kd = kdup[hk].astype(BF16)
            vd = vdup[hk].astype(BF16)
            for c in range(NSA_GROUP):
                h = hk * NSA_GROUP + c
                a, par = h // 2, h % 2
                qm = jnp.where(qlo if par == 0 else ~qlo, q_ref[:, a * LANES:(a + 1) * LANES], 0.0).astype(BF16)
                s = _dot_nt(qm, kd) * ATT_SCALE
                _flash_update(s, mask, vd, m_ref.at[h], l_ref.at[h], acc_ref.at[h])

    @pl.when(kj == last)
    def _():
        qlo = _lane_lo((tq, LANES))
        for a in range(NSA_HEADS // 2):
            o0 = acc_ref[2 * a] / jnp.maximum(l_ref[2 * a], 1e-30)
            o1 = acc_ref[2 * a + 1] / jnp.maximum(l_ref[2 * a + 1], 1e-30)
            o_ref[:, a * LANES:(a + 1) * LANES] = jnp.where(qlo, o0, o1)


def nsa_slc(q_rot, rows, sel, emat, nseq, tq=256, tk=512):
    t = q_rot.shape[0]
    s = t // nseq
    nq, nk = s // tq, s // tk
    nb = sel.shape[1] // 2

    def kidx(b, i, j):
        return b * nk + jnp.minimum(j, (i * tq + tq - 1) // tk)

    return pl.pallas_call(
        functools.partial(_slc_kernel, tq=tq, tk=tk, nb=nb),
        grid=(nseq, nq, nk),
        in_specs=[pl.BlockSpec((tq, 512), lambda b, i, j: (b * nq + i, 0)),
                  pl.BlockSpec((tk, LANES), lambda b, i, j: (kidx(b, i, j), 2)),
                  pl.BlockSpec((tk, LANES), lambda b, i, j: (kidx(b, i, j), 3)),
                  pl.BlockSpec((tq, 2 * nb), lambda b, i, j: (b * nq + i, 0)),
                  pl.BlockSpec((nb, tk), lambda b, i, j: (0, jnp.minimum(j, (i * tq + tq - 1) // tk)))],
        out_specs=pl.BlockSpec((tq, 512), lambda b, i, j: (b * nq + i, 0)),
        out_shape=jax.ShapeDtypeStruct((t, 512), F32),
        scratch_shapes=[pltpu.VMEM((NSA_HEADS, tq, 1), F32), pltpu.VMEM((NSA_HEADS, tq, 1), F32),
                        pltpu.VMEM((NSA_HEADS, tq, LANES), F32)],
        compiler_params=_cparams(("arbitrary", "arbitrary", "arbitrary")),
        name="nsa_slc",
    )(q_rot, rows, rows, sel, emat)


def _win_kernel(q_ref, wp_ref, wc_ref, o_ref, *, tq):
    i = pl.program_id(1)
    lo = _lane_lo((tq, LANES))
    qpos = i * tq + lax.broadcasted_iota(jnp.int32, (tq, 1), 0)
    chunks = []
    for off, ref in ((-tq, wp_ref), (0, wc_ref)):
        kpos = i * tq + off + lax.broadcasted_iota(jnp.int32, (1, tq), 1)
        mask = (kpos >= 0) & (kpos <= qpos) & (qpos - kpos <= NSA_WINDOW)
        chunks.append((mask, _dup_halves(ref[:, 0:LANES], lo), _dup_halves(ref[:, LANES:2 * LANES], lo)))
    for a in range(NSA_HEADS // 2):
        pair = []
        for par in range(2):
            hk = (2 * a + par) // NSA_GROUP
            qm = jnp.where(lo if par == 0 else ~lo, q_ref[:, a * LANES:(a + 1) * LANES], 0.0).astype(BF16)
            ss = [jnp.where(mask, _dot_nt(qm, kd[hk].astype(BF16)) * ATT_SCALE, NEG) for mask, kd, _ in chunks]
            m = jnp.maximum(jnp.max(ss[0], axis=-1, keepdims=True), jnp.max(ss[1], axis=-1, keepdims=True))
            ps = [jnp.where(ch[0], jnp.exp(s - m), 0.0) for s, ch in zip(ss, chunks)]
            l = jnp.sum(ps[0], axis=-1, keepdims=True) + jnp.sum(ps[1], axis=-1, keepdims=True)
            inv = 1.0 / jnp.maximum(l, 1e-30)
            o = _dot((ps[0] * inv).astype(BF16), chunks[0][2][hk].astype(BF16))
            o += _dot((ps[1] * inv).astype(BF16), chunks[1][2][hk].astype(BF16))
            pair.append(o)
        o_ref[:, a * LANES:(a + 1) * LANES] = jnp.where(lo, pair[0], pair[1])


def nsa_win(q_rot, win_rows, nseq, tq=NSA_WINDOW):
    t = q_rot.shape[0]
    nq = t // nseq // tq
    return pl.pallas_call(
        functools.partial(_win_kernel, tq=tq),
        grid=(nseq, nq),
        in_specs=[pl.BlockSpec((tq, 512), lambda b, i: (b * nq + i, 0)),
                  pl.BlockSpec((tq, 256), lambda b, i: (b * nq + jnp.maximum(i - 1, 0), 0)),
                  pl.BlockSpec((tq, 256), lambda b, i: (b * nq + i, 0))],
        out_specs=pl.BlockSpec((tq, 512), lambda b, i: (b * nq + i, 0)),
        out_shape=jax.ShapeDtypeStruct((t, 512), F32),
        compiler_params=_cparams(("arbitrary", "arbitrary")),
        name="nsa_win",
    )(q_rot, win_rows, win_rows)


def _diff_lambda(lam_ref, lam_init):
    lp = lam_ref[...]
    a = jnp.sum(lp[0:1] * lp[1:2], axis=-1, keepdims=True)
    b = jnp.sum(lp[2:3] * lp[3:4], axis=-1, keepdims=True)
    return jnp.exp(a) - jnp.exp(b) + lam_init


def _diff_finish(o1, o2, lam, g, lam_init):
    o = o1 - lam * o2
    ms = jnp.mean(o * o, axis=-1, keepdims=True)
    return o * lax.rsqrt(ms + NORM_EPS) * g * (1.0 - lam_init)


def _diff_kernel(q_ref, k_ref, v_ref, lam_ref, g_ref, o_ref, m_ref, l_ref, acc_ref, *, tq, lam_init):
    qi, kj = pl.program_id(2), pl.program_id(3)

    @pl.when(kj == 0)
    def _():
        m_ref[...] = jnp.full(m_ref.shape, NEG, F32)
        l_ref[...] = jnp.zeros_like(l_ref)
        acc_ref[...] = jnp.zeros_like(acc_ref)

    def step(mask):
        lo = _lane_lo((tq, LANES))
        q = q_ref[...]
        k = k_ref[...].astype(BF16)
        v = v_ref[...].astype(BF16)
        for par in range(2):
            qm = jnp.where(lo if par == 0 else ~lo, q, 0.0).astype(BF16)
            s = _dot_nt(qm, k) * ATT_SCALE
            _flash_update(s, mask, v, m_ref.at[par], l_ref.at[par], acc_ref.at[par])

    @pl.when(kj < qi)
    def _():
        step(None)

    @pl.when(kj == qi)
    def _():
        r = lax.broadcasted_iota(jnp.int32, (tq, 1), 0)
        c = lax.broadcasted_iota(jnp.int32, (1, tq), 1)
        step(c <= r)
        o1 = acc_ref[0] / jnp.maximum(l_ref[0], 1e-30)
        o2 = acc_ref[1] / jnp.maximum(l_ref[1], 1e-30)
        lam = _diff_lambda(lam_ref, lam_init)
        o_ref[...] = _diff_finish(o1, o2, lam, g_ref[...], lam_init).astype(o_ref.dtype)


def diff_attn(dq, drows, lam_p, out_g, nseq, lam_init, tq=512):
    t = dq.shape[0]
    nq = t // nseq // tq
    return pl.pallas_call(
        functools.partial(_diff_kernel, tq=tq, lam_init=lam_init),
        grid=(nseq, DIFF_HEADS, nq, nq),
        in_specs=[pl.BlockSpec((tq, LANES), lambda b, h, i, j: (b * nq + i, h)),
                  pl.BlockSpec((tq, LANES), lambda b, h, i, j: (b * nq + jnp.minimum(i, j), h)),
                  pl.BlockSpec((tq, LANES), lambda b, h, i, j: (b * nq + jnp.minimum(i, j), DIFF_HEADS + h)),
                  pl.BlockSpec((4, HEAD_DIM), lambda b, h, i, j: (0, 0)),
                  pl.BlockSpec((1, LANES), lambda b, h, i, j: (0, 0))],
        out_specs=pl.BlockSpec((tq, LANES), lambda b, h, i, j: (b * nq + i, h)),
        out_shape=jax.ShapeDtypeStruct((t, 512), BF16),
        scratch_shapes=[pltpu.VMEM((2, tq, 1), F32), pltpu.VMEM((2, tq, 1), F32), pltpu.VMEM((2, tq, LANES), F32)],
        compiler_params=_cparams(("arbitrary", "arbitrary", "arbitrary", "arbitrary")),
        name="diff_attn",
    )(dq, drows, drows, lam_p, out_g)


QPAD = 8


def _pad_keys(x):
    return jnp.concatenate([x, jnp.zeros((LANES - x.shape[0], x.shape[1]), x.dtype)], axis=0)


def _nsa_sample_kernel(pt_ref, *refs, n_pages, past_len, n_new, n_sel, nbp):
    pages = refs[:n_pages]
    (new_ref, qn_ref, qr_ref, wst_ref, wnew_ref, w2_ref, e_ref,
     oc_ref, os_ref, ow_ref, cmp_ref) = refs[n_pages:]
    page = pages[0].shape[0]
    per_page = page // NSA_BLOCK
    n_real = (past_len + n_new + NSA_BLOCK - 1) // NSA_BLOCK

    w2 = w2_ref[...]
    cmp_ref[...] = jnp.zeros_like(cmp_ref)
    for j in range(n_pages):
        pg = pages[j][:, 0:256] * w2
        for r in range(per_page):
            cmp_ref[j * per_page + r:j * per_page + r + 1, :] = jnp.sum(
                pg[r * NSA_BLOCK:(r + 1) * NSA_BLOCK], axis=0, keepdims=True)
    nblk = past_len // NSA_BLOCK
    cmp_ref[nblk:nblk + 1, :] = jnp.sum(new_ref[:, 0:256] * w2[0:QPAD], axis=0, keepdims=True)

    tok = lax.broadcasted_iota(jnp.int32, (QPAD, 1), 0)
    qpos = past_len + tok
    o_cmp, sels = _cmp_select(qn_ref[...], cmp_ref[...], qpos, n_real, n_sel)
    oc_ref[...] = o_cmp

    qlo = _lane_lo((QPAD, LANES))
    qr = qr_ref[...]
    qst = []
    for hk in range(NSA_KV_HEADS):
        parts = []
        for c in range(NSA_GROUP):
            h = hk * NSA_GROUP + c
            parts.append(jnp.where(qlo if h % 2 == 0 else ~qlo, qr[:, (h // 2) * LANES:(h // 2 + 1) * LANES], 0.0))
        qst.append(jnp.concatenate(parts, axis=0).astype(BF16))
    rows = NSA_GROUP * QPAD
    qpos_st = jnp.concatenate([qpos] * NSA_GROUP, axis=0)
    klo = _lane_lo((LANES, LANES))
    key_new = lax.broadcasted_iota(jnp.int32, (1, LANES), 1)
    new_valid = (key_new < n_new) & (past_len + key_new <= qpos_st)

    def attend(chunks, masks, hk):
        ss = []
        for (k, _), mask in zip(chunks, masks):
            kd = _dup_halves(k, klo)[hk].astype(BF16)
            ss.append(jnp.where(mask, _dot_nt(qst[hk], kd) * ATT_SCALE, NEG))
        m = functools.reduce(jnp.maximum, [jnp.max(s, axis=-1, keepdims=True) for s in ss])
        ps = [jnp.where(mask, jnp.exp(s - m), 0.0) for s, mask in zip(ss, masks)]
        l = functools.reduce(lambda a, b: a + b, [jnp.sum(p, axis=-1, keepdims=True) for p in ps])
        inv = 1.0 / jnp.maximum(l, 1e-30)
        o = jnp.zeros((rows, LANES), F32)
        for p, (_, v) in zip(ps, chunks):
            vd = _dup_halves(v, klo)[hk].astype(BF16)
            o += _dot((p * inv).astype(BF16), vd)
        return o

    def emit(o_ref, per_hk):
        for a in range(NSA_HEADS // 2):
            hk, c = (2 * a) // NSA_GROUP, (2 * a) % NSA_GROUP
            o = per_hk[hk]
            o_ref[:, a * LANES:(a + 1) * LANES] = jnp.where(qlo, o[c * QPAD:(c + 1) * QPAD], o[(c + 1) * QPAD:(c + 2) * QPAD])

    new = new_ref[...]
    slc_chunks = [(pages[j][:, 2 * LANES:3 * LANES], pages[j][:, 3 * LANES:4 * LANES]) for j in range(n_pages)]
    slc_chunks.append((_pad_keys(new[:, 2 * LANES:3 * LANES]), _pad_keys(new[:, 3 * LANES:4 * LANES])))
    e = e_ref[...]
    per_hk = []
    for hk in range(NSA_KV_HEADS):
        selx = _dot(sels[hk].astype(BF16), e)
        selx = jnp.concatenate([selx] * NSA_GROUP, axis=0)
        masks = [selx[:, n * LANES:(n + 1) * LANES] > 0.5 for n in range(n_pages)]
        masks.append((selx[:, n_pages * LANES:(n_pages + 1) * LANES] > 0.5) & new_valid)
        per_hk.append(attend(slc_chunks, masks, hk))
    emit(os_ref, per_hk)

    wbuf = wst_ref.shape[0]
    wnew = wnew_ref[...]
    win_chunks, win_masks = [], []
    for n in range(wbuf // LANES):
        blk = wst_ref[n * LANES:(n + 1) * LANES, :]
        win_chunks.append((blk[:, 0:LANES], blk[:, LANES:2 * LANES]))
        kpos = past_len - wbuf + n * LANES + key_new
        win_masks.append((kpos >= 0) & (kpos <= qpos_st) & (qpos_st - kpos <= NSA_WINDOW))
    win_chunks.append((_pad_keys(wnew[:, 0:LANES]), _pad_keys(wnew[:, LANES:2 * LANES])))
    win_masks.append(new_valid & (qpos_st - (past_len + key_new) <= NSA_WINDOW))
    emit(ow_ref, [attend(win_chunks, win_masks, hk) for hk in range(NSA_KV_HEADS)])


def nsa_sample(layer, page_table, cache, new_rows, q_n, q_rot, win_state, win_new, w2x, emat, n_new):
    nb_, n_pages = page_table.shape
    page = cache.shape[2]
    past_len = n_pages * page
    n_real = (past_len + n_new + NSA_BLOCK - 1) // NSA_BLOCK
    nbp = emat.shape[0]
    wbuf = win_state.shape[1]

    def page_spec(j):
        return pl.BlockSpec((None, None, page, 512), lambda b, pt: (layer, pt[b, j], 0, 0))

    def seq(shape):
        return pl.BlockSpec((None,) + shape, lambda b, pt: (b, 0, 0))

    def const(shape):
        return pl.BlockSpec(shape, lambda b, pt: (0, 0))

    grid_spec = pltpu.PrefetchScalarGridSpec(
        num_scalar_prefetch=1,
        grid=(nb_,),
        in_specs=[page_spec(j) for j in range(n_pages)] + [
            seq((QPAD, 512)), seq((QPAD, 512)), seq((QPAD, 512)), seq((wbuf, 256)), seq((QPAD, 256)),
            const((page, 256)), const(emat.shape)],
        out_specs=[seq((QPAD, 512)), seq((QPAD, 512)), seq((QPAD, 512))],
        scratch_shapes=[pltpu.VMEM((nbp, 256), F32)],
    )
    return pl.pallas_call(
        functools.partial(_nsa_sample_kernel, n_pages=n_pages, past_len=past_len, n_new=n_new,
                          n_sel=min(NSA_TOPK, n_real), nbp=nbp),
        grid_spec=grid_spec,
        out_shape=[jax.ShapeDtypeStruct((nb_, QPAD, 512), F32)] * 3,
        compiler_params=_cparams(("arbitrary",)),
        name="nsa_sample",
    )(page_table, *([cache] * n_pages), new_rows, q_n, q_rot, win_state, win_new, w2x, emat)


def _diff_sample_kernel(pt_ref, *refs, n_pages, past_len, n_new, lam_init):
    pages = refs[:n_pages]
    new_ref, q_ref, lam_ref, g_ref, o_ref = refs[n_pages:]
    lo = _lane_lo((QPAD, LANES))
    tok = lax.broadcasted_iota(jnp.int32, (2 * QPAD, 1), 0) % QPAD
    key_new = lax.broadcasted_iota(jnp.int32, (1, LANES), 1)
    new_valid = (key_new < n_new) & (key_new <= tok)
    lam = _diff_lambda(lam_ref, lam_init)
    q = q_ref[...]
    new = new_ref[...]
    for h in range(DIFF_HEADS):
        qp = q[:, h * LANES:(h + 1) * LANES]
        qst = jnp.concatenate([jnp.where(lo, qp, 0.0), jnp.where(lo, 0.0, qp)], axis=0).astype(BF16)
        ks = [pages[j][:, h * LANES:(h + 1) * LANES] for j in range(n_pages)]
        ks.append(_pad_keys(new[:, h * LANES:(h + 1) * LANES]))
        vs = [pages[j][:, (DIFF_HEADS + h) * LANES:(DIFF_HEADS + h + 1) * LANES] for j in range(n_pages)]
        vs.append(_pad_keys(new[:, (DIFF_HEADS + h) * LANES:(DIFF_HEADS + h + 1) * LANES]))
        ss = [_dot_nt(qst, k.astype(BF16)) * ATT_SCALE for k in ks]
        ss[-1] = jnp.where(new_valid, ss[-1], NEG)
        m = functools.reduce(jnp.maximum, [jnp.max(s, axis=-1, keepdims=True) for s in ss])
        ps = [jnp.exp(s - m) for s in ss]
        ps[-1] = jnp.where(new_valid, ps[-1], 0.0)
        l = functools.reduce(lambda a, b: a + b, [jnp.sum(p, axis=-1, keepdims=True) for p in ps])
        inv = 1.0 / jnp.maximum(l, 1e-30)
        o = jnp.zeros((QPAD, LANES), F32)
        for p, v in zip(ps, vs):
            pn = p * inv
            a = (pn[0:QPAD] - lam * pn[QPAD:2 * QPAD]).astype(BF16)
            o += _dot(a, v.astype(BF16))
        ms = jnp.mean(o * o, axis=-1, keepdims=True)
        o_ref[:, h * LANES:(h + 1) * LANES] = (o * lax.rsqrt(ms + NORM_EPS) * g_ref[...] * (1.0 - lam_init)).astype(o_ref.dtype)


def diff_sample(layer, page_table, cache, new_rows, dq, lam_p, out_g, n_new, lam_init):
    nb_, n_pages = page_table.shape
    page = cache.shape[2]

    def page_spec(j):
        return pl.BlockSpec((None, None, page, 1024), lambda b, pt: (layer, pt[b, j], 0, 0))

    grid_spec = pltpu.PrefetchScalarGridSpec(
        num_scalar_prefetch=1,
        grid=(nb_,),
        in_specs=[page_spec(j) for j in range(n_pages)] + [
            pl.BlockSpec((None, QPAD, 1024), lambda b, pt: (b, 0, 0)),
            pl.BlockSpec((None, QPAD, 512), lambda b, pt: (b, 0, 0)),
            pl.BlockSpec((4, HEAD_DIM), lambda b, pt: (0, 0)),
            pl.BlockSpec((1, LANES), lambda b, pt: (0, 0))],
        out_specs=pl.BlockSpec((None, QPAD, 512), lambda b, pt: (b, 0, 0)),
    )
    return pl.pallas_call(
        functools.partial(_diff_sample_kernel, n_pages=n_pages, past_len=n_pages * page, n_new=n_new, lam_init=lam_init),
        grid_spec=grid_spec,
        out_shape=jax.ShapeDtypeStruct((nb_, QPAD, 512), F32),
        compiler_params=_cparams(("arbitrary",)),
        name="diff_sample",
    )(page_table, *([cache] * n_pages), new_rows, dq, lam_p, out_g)


def _mix_weight(w):
    d = w.shape[0]
    splits = (A_WIDTH, A_WIDTH, A_WIDTH, 512, 768, 3 * NSA_HEADS, 512, 512, 512, 2 * D_WIDTH, N_BRANCH * d)
    offs = [0]
    for s in splits:
        offs.append(offs[-1] + s)
    part = {n: w[:, offs[i]:offs[i + 1]] for i, n in enumerate(
        ("ab", "ac", "ax", "nq", "kv", "g", "dq", "dk", "dv", "glu", "merge"))}
    g = part["g"].reshape(d, NSA_HEADS, 3)
    gates = [jnp.repeat(g[:, :, k], HEAD_DIM, axis=1) for k in range(3)]
    pad = jnp.zeros((d, C_MERGE - 768), w.dtype)
    cols = [part["kv"], pad, part["merge"], part["glu"], part["ab"], part["ac"], part["ax"], part["nq"],
            part["dq"], part["dk"], part["dv"]] + gates
    out = jnp.concatenate(cols, axis=1).astype(BF16)
    assert out.shape[1] == MIX_W
    return out


def _rope_tables(pos):
    half = HEAD_DIM // 2
    inv = ROPE_THETA ** (-jnp.arange(half, dtype=F32) / half)
    ang = pos.astype(F32)[:, None] * inv[None, :]
    cos, sin = jnp.cos(ang), jnp.sin(ang)
    return jnp.tile(jnp.concatenate([cos, cos], axis=1), (1, 2)), jnp.tile(jnp.concatenate([-sin, sin], axis=1), (1, 2))


def _tile_gain(g):
    return jnp.tile(g.astype(F32), LANES // HEAD_DIM)


def _expansion(nb_rows, n_keys):
    n = jnp.arange(nb_rows)[:, None]
    k = jnp.arange(n_keys)[None, :]
    return (k // NSA_BLOCK == n).astype(BF16)


def _to_seq_major(x, nb_, n_new, pad_to=None):
    y = jnp.swapaxes(x.reshape(n_new, nb_, x.shape[-1]), 0, 1)
    if pad_to is not None and pad_to > n_new:
        y = jnp.pad(y, ((0, 0), (0, pad_to - n_new), (0, 0)))
    return y


def _to_step_major(y, n_new):
    return jnp.swapaxes(y[:, :n_new], 0, 1).reshape(-1, y.shape[-1])


def kernel(x_prompt, x_sample, cache_nsa, cache_diff, state_nsa_win, state_conv_a, state_conv_d, page_table,
           c_prompt, c_sample, ada_w, ada_b, norm_g, ffn1_up, ffn1_down, ffn2_up, ffn2_down, w_mix_in,
           conv_a_w, conv_a_b, a_out, nsa_qk_g, nsa_cmp_w, nsa_out, diff_qk_g, diff_lambda, diff_out_g, diff_out,
           conv_d_w, conv_d_b, d_norm_g, d_norm_b, d_out, w_out):
    bp, sp, d = x_prompt.shape
    bs, ns, _ = x_sample.shape
    depth = ada_w.shape[0]
    n_pages = page_table.shape[1]
    page = cache_nsa.shape[2]
    past_len = n_pages * page
    wbuf = state_nsa_win.shape[2]

    rows = bp + bs
    rpad = -(-rows // 8) * 8
    c_all = jnp.pad(jnp.concatenate([c_prompt, c_sample], axis=0), ((0, rpad - rows), (0, 0)))
    ada = ada_proj(c_all, ada_w, ada_b)

    xp = x_prompt.reshape(bp * sp, d)
    xs = jnp.swapaxes(x_sample, 0, 1).reshape(ns * bs, d)
    tm_p, tm_s = 512, ns * bs
    tpg_p = sp // tm_p

    cos_p, sin_p = _rope_tables(jnp.arange(sp))
    cos_s, sin_s = _rope_tables(jnp.repeat(past_len + jnp.arange(ns), bs))
    gmat = (jnp.arange(LANES)[:, None] // HEAD_DIM == jnp.arange(LANES)[None, :] // HEAD_DIM).astype(BF16)
    emat_p = _expansion(sp // NSA_BLOCK, sp)
    n_real_s = (past_len + ns + NSA_BLOCK - 1) // NSA_BLOCK
    nbp_s = -(-n_real_s // LANES) * LANES
    emat_s = _expansion(nbp_s, past_len + LANES)
    cache_nsa4 = cache_nsa.reshape(depth, cache_nsa.shape[1], page, 512)
    cache_diff4 = cache_diff.reshape(depth, cache_diff.shape[1], page, 1024)

    zeros_a = jnp.zeros((bp, 8, A_WIDTH), F32)
    zeros_d = jnp.zeros((bp, 32, D_WIDTH), F32)

    outs = {k: [] for k in ("nsa_p", "nsa_s", "diff_p", "diff_s", "win_p", "win_s", "ca_p", "ca_s", "cd_p", "cd_s")}
    for l in range(depth):
        lam_init = 0.8 - 0.6 * math.exp(-0.3 * l)
        mod_p = ada[l, :, 0:bp].reshape(N_ADA, bp, 1, d)
        mod_s = ada[l, :, bp:bp + bs].reshape(N_ADA, 1, bs, d)
        g = norm_g[l]
        w1u, w1d = ffn1_up[l].astype(BF16), ffn1_down[l].astype(BF16)
        w2u, w2d = ffn2_up[l].astype(BF16), ffn2_down[l].astype(BF16)
        wmix = _mix_weight(w_mix_in[l])
        gains = jnp.stack([_tile_gain(nsa_qk_g[l, 0]), _tile_gain(nsa_qk_g[l, 1]), _tile_gain(nsa_qk_g[l, 2]),
                           _tile_gain(nsa_qk_g[l, 3]), _tile_gain(diff_qk_g[l, 0]), _tile_gain(diff_qk_g[l, 1]),
                           jnp.zeros((LANES,), F32), jnp.zeros((LANES,), F32)])
        cw = nsa_cmp_w[l]
        w2 = jnp.concatenate([cw[0], cw[0], cw[1], cw[1]], axis=1)
        w2x = jnp.tile(w2, (page // NSA_BLOCK, 1))
        wts = [a_out[l].astype(BF16), nsa_out[l].astype(BF16), diff_out[l].astype(BF16), d_out[l].astype(BF16),
               w_out[l].astype(BF16)]
        conv_w = (conv_a_w[l], conv_a_b[l][None], conv_d_w[l], conv_d_b[l][None], d_norm_g[l][None], d_norm_b[l][None])
        lam_p, og = diff_lambda[l], diff_out_g[l][None]

        xp = ffn(xp, mod_p, 0, g[0:1], w1u, w1d, tm_p, tpg_p)
        zp = mix_in(xp, mod_p, g[1:2], wmix, 1024, sp // 1024)
        qn, qr, rows_p, win_p, dq, drows_p = qk_prep(zp, cos_p, sin_p, gains, gmat, tm_p, tpg_p)
        ua, ud, ca, cd = conv_branches(zp, zeros_a, zeros_d, *conv_w, nseq=bp, tm=tm_p, stride=1)
        cmp_p = nsa_compress(rows_p, w2)
        o_cmp, sel = nsa_cmp_select(qn, cmp_p, bp)
        o_slc = nsa_slc(qr, rows_p, sel, emat_p, bp)
        o_win = nsa_win(qr, win_p, bp)
        uc = diff_attn(dq, drows_p, lam_p, og, bp, lam_init)
        xp = merge_out(xp, mod_p, zp, ua, uc, ud, o_cmp, o_slc, o_win, *wts, tm=tm_p, tiles_per_group=tpg_p)
        xp = ffn(xp, mod_p, 6, g[2:3], w2u, w2d, tm_p, tpg_p)
        outs["nsa_p"].append(rows_p.reshape(bp, sp, 4, NSA_KV_HEADS, HEAD_DIM))
        outs["diff_p"].append(drows_p.reshape(bp, sp, 2, DIFF_HEADS, 2 * HEAD_DIM))
        wkeep = min(NSA_WINDOW, sp)
        outs["win_p"].append(win_p.reshape(bp, sp, 2, NSA_KV_HEADS, HEAD_DIM)[:, sp - wkeep:])
        outs["ca_p"].append(ca)
        outs["cd_p"].append(cd)

        xs = ffn(xs, mod_s, 0, g[0:1], w1u, w1d, tm_s, 1)
        zs = mix_in(xs, mod_s, g[1:2], wmix, tm_s, 1)
        qn, qr, rows_s, win_s, dq, drows_s = qk_prep(zs, cos_s, sin_s, gains, gmat, tm_s, 1)
        past_a = jnp.swapaxes(state_conv_a[l], 0, 1).reshape(1, (A_CONV - 1) * bs, A_WIDTH)
        past_d = jnp.swapaxes(state_conv_d[l], 0, 1).reshape(1, (D_CONV - 1) * bs, D_WIDTH)
        ua, ud, ca, cd = conv_branches(zs, past_a, past_d, *conv_w, nseq=1, tm=tm_s, stride=bs)
        rows_b = _to_seq_major(rows_s, bs, ns, QPAD)
        win_b = _to_seq_major(win_s, bs, ns, QPAD)
        win_state = state_nsa_win[l].reshape(bs, wbuf, 256)
        o_cmp, o_slc, o_win = nsa_sample(l, page_table, cache_nsa4, rows_b, _to_seq_major(qn, bs, ns, QPAD),
                                         _to_seq_major(qr, bs, ns, QPAD), win_state, win_b, w2x, emat_s, ns)
        drows_b = _to_seq_major(drows_s, bs, ns, QPAD)
        uc = diff_sample(l, page_table, cache_diff4, drows_b, _to_seq_major(dq, bs, ns, QPAD), lam_p, og, ns, lam_init)
        xs = merge_out(xs, mod_s, zs, ua, _to_step_major(uc, ns).astype(BF16), ud, _to_step_major(o_cmp, ns),
                       _to_step_major(o_slc, ns), _to_step_major(o_win, ns), *wts, tm=tm_s, tiles_per_group=1)
        xs = ffn(xs, mod_s, 6, g[2:3], w2u, w2d, tm_s, 1)
        outs["nsa_s"].append(rows_b[:, :ns].reshape(bs, ns, 4, NSA_KV_HEADS, HEAD_DIM))
        outs["diff_s"].append(drows_b[:, :ns].reshape(bs, ns, 2, DIFF_HEADS, 2 * HEAD_DIM))
        new_win = jnp.concatenate([win_state, win_b[:, :ns]], axis=1)[:, -wbuf:]
        outs["win_s"].append(new_win.reshape(bs, wbuf, 2, NSA_KV_HEADS, HEAD_DIM))
        outs["ca_s"].append(jnp.swapaxes(ca.reshape(A_CONV - 1, bs, A_WIDTH), 0, 1))
        outs["cd_s"].append(jnp.swapaxes(cd.reshape(D_CONV - 1, bs, D_WIDTH), 0, 1))

    y_p = xp.reshape(bp, sp, d)
    y_s = jnp.swapaxes(xs.reshape(ns, bs, d), 0, 1)
    st = {k: jnp.stack(v, axis=0) for k, v in outs.items()}
    return (y_p, y_s, st["nsa_p"], st["nsa_s"], st["diff_p"], st["diff_s"], st["win_p"], st["win_s"],
            st["ca_p"], st["ca_s"], st["cd_p"], st["cd_s"])
```

```python
import functools
import math

import jax
import jax.numpy as jnp
from jax import lax
from jax.experimental import pallas as pl
from jax.experimental.pallas import tpu as pltpu

F32 = jnp.float32
BF16 = jnp.bfloat16

HEAD_DIM = 64
ROPE_THETA = 10000.0
NORM_EPS = 1e-6
A_WIDTH = 512
A_CONV = 3
NSA_HEADS = 8
NSA_KV_HEADS = 2
NSA_GROUP = NSA_HEADS // NSA_KV_HEADS
NSA_BLOCK = 64
NSA_TOPK = 16
NSA_WINDOW = 512
DIFF_HEADS = 4
D_WIDTH = 512
D_CONV = 31
N_BRANCH = 4
N_ADA = 9
LANES = 128
NEG = -1e30
ATT_SCALE = HEAD_DIM ** -0.5
Q_SCALE = ATT_SCALE * math.log2(math.e)
VMEM_LIMIT = 56 * 1024 * 1024

C_KV = 0
C_MERGE = 1024
C_GLU = 5120
C_AB = 6144
C_AC = 6656
C_AX = 7168
C_NQ = 7680
C_DQ = 8192
C_DK = 8704
C_DV = 9216
C_GC = 9728
C_GS = 10240
C_GW = 10752
MIX_W = 11264


def _cparams(sem):
    return pltpu.CompilerParams(dimension_semantics=sem, vmem_limit_bytes=VMEM_LIMIT)


def _sigmoid(x):
    return 1.0 / (1.0 + jnp.exp(-x))


def _silu(x):
    return x * _sigmoid(x)


def _rows_to(m, tm):
    r = m.shape[0]
    if r == 1 or r == tm:
        return m
    return jnp.concatenate([m] * (tm // r), axis=0)


def _modulate(x, g, shift, scale):
    ms = jnp.mean(x * x, axis=-1, keepdims=True)
    y = x * lax.rsqrt(ms + NORM_EPS) * g
    return y * (1.0 + scale) + shift


def _dot(a, b):
    return jnp.dot(a, b, preferred_element_type=F32)


def _dot_nt(a, b, precision=None):
    return lax.dot_general(a, b, (((1,), (1,)), ((), ())), preferred_element_type=F32, precision=precision)


def _lane_lo(shape):
    return (lax.broadcasted_iota(jnp.int32, shape, len(shape) - 1) % LANES) < HEAD_DIM


def _dup_halves(x, lo):
    xr = pltpu.roll(x, HEAD_DIM, 1)
    return jnp.where(lo, x, xr), jnp.where(lo, xr, x)


def _ada_kernel(c_ref, w_ref, b_ref, o_ref):
    s = _silu(c_ref[...])
    o_ref[...] = _dot(s.astype(BF16), w_ref[...].astype(BF16)) + b_ref[...]


def ada_proj(c_all, ada_w, ada_b):
    r, d = c_all.shape
    nl = ada_w.shape[0]
    return pl.pallas_call(
        _ada_kernel,
        grid=(nl, N_ADA),
        in_specs=[
            pl.BlockSpec((r, d), lambda l, k: (0, 0)),
            pl.BlockSpec((None, d, d), lambda l, k: (l, 0, k)),
            pl.BlockSpec((None, None, 1, d), lambda l, k: (l, k, 0, 0)),
        ],
        out_specs=pl.BlockSpec((None, None, r, d), lambda l, k: (l, k, 0, 0)),
        out_shape=jax.ShapeDtypeStruct((nl, N_ADA, r, d), F32),
        compiler_params=_cparams(("arbitrary", "arbitrary")),
        name="ada_proj",
    )(c_all, ada_w, ada_b.reshape(nl, N_ADA, 1, d))


def _mod_spec(mod, k, tiles_per_group, ngrid):
    _, _, r, d = mod.shape
    if ngrid == 1:
        return pl.BlockSpec((None, None, r, d), lambda i: (k, i // tiles_per_group, 0, 0))
    return pl.BlockSpec((None, None, r, d), lambda i, j: (k, i // tiles_per_group, 0, 0))


def _ffn_kernel(x_ref, sh_ref, sc_ref, gt_ref, g_ref, wa_ref, wv_ref, wd_ref, o_ref, h_ref, acc_ref, *, nj):
    j = pl.program_id(1)
    tm = x_ref.shape[0]

    @pl.when(j == 0)
    def _():
        h = _modulate(x_ref[...], g_ref[...], _rows_to(sh_ref[...], tm), _rows_to(sc_ref[...], tm))
        h_ref[...] = h.astype(BF16)
        acc_ref[...] = jnp.zeros_like(acc_ref)

    h = h_ref[...]
    a = _dot(h, wa_ref[...])
    v = _dot(h, wv_ref[...])
    act = (_silu(a) * v).astype(BF16)
    acc_ref[...] += _dot(act, wd_ref[...])

    @pl.when(j == nj - 1)
    def _():
        o_ref[...] = x_ref[...] + 0.5 * _rows_to(gt_ref[...], tm) * acc_ref[...]


def ffn(x, mod, k0, g, w_up, w_down, tm, tiles_per_group):
    t, d = x.shape
    f = w_down.shape[0]
    nj = 2
    tf = f // nj
    return pl.pallas_call(
        functools.partial(_ffn_kernel, nj=nj),
        grid=(t // tm, nj),
        in_specs=[
            pl.BlockSpec((tm, d), lambda i, j: (i, 0)),
            _mod_spec(mod, k0, tiles_per_group, 2),
            _mod_spec(mod, k0 + 1, tiles_per_group, 2),
            _mod_spec(mod, k0 + 2, tiles_per_group, 2),
            pl.BlockSpec((1, d), lambda i, j: (0, 0)),
            pl.BlockSpec((d, tf), lambda i, j: (0, j)),
            pl.BlockSpec((d, tf), lambda i, j: (0, nj + j)),
            pl.BlockSpec((tf, d), lambda i, j: (j, 0)),
        ],
        out_specs=pl.BlockSpec((tm, d), lambda i, j: (i, 0)),
        out_shape=jax.ShapeDtypeStruct((t, d), F32),
        scratch_shapes=[pltpu.VMEM((tm, d), BF16), pltpu.VMEM((tm, d), F32)],
        compiler_params=_cparams(("arbitrary", "arbitrary")),
        name="ffn",
    )(x, mod, mod, mod, g, w_up, w_up, w_down)


def _mix_kernel(x_ref, sh_ref, sc_ref, g_ref, w_ref, o_ref, h_ref):
    tm = x_ref.shape[0]

    @pl.when(pl.program_id(1) == 0)
    def _():
        h = _modulate(x_ref[...], g_ref[...], _rows_to(sh_ref[...], tm), _rows_to(sc_ref[...], tm))
        h_ref[...] = h.astype(BF16)

    o_ref[...] = _dot(h_ref[...], w_ref[...])


def mix_in(x, mod, g, w_mix, tm, tiles_per_group, tn=1024):
    t, d = x.shape
    nw = w_mix.shape[1]
    return pl.pallas_call(
        _mix_kernel,
        grid=(t // tm, nw // tn),
        in_specs=[
            pl.BlockSpec((tm, d), lambda i, j: (i, 0)),
            _mod_spec(mod, 3, tiles_per_group, 2),
            _mod_spec(mod, 4, tiles_per_group, 2),
            pl.BlockSpec((1, d), lambda i, j: (0, 0)),
            pl.BlockSpec((d, tn), lambda i, j: (0, j)),
        ],
        out_specs=pl.BlockSpec((tm, tn), lambda i, j: (i, j)),
        out_shape=jax.ShapeDtypeStruct((t, nw), F32),
        scratch_shapes=[pltpu.VMEM((tm, d), BF16)],
        compiler_params=_cparams(("arbitrary", "arbitrary")),
        name="mix_in",
    )(x, mod, mod, g, w_mix)


def _group_norm(x, gain, gmat):
    x2 = x * x
    hi = x2.astype(BF16)
    lo = (x2 - hi.astype(F32)).astype(BF16)
    ss = _dot(hi, gmat) + _dot(lo, gmat)
    return x * lax.rsqrt(ss * (1.0 / HEAD_DIM) + NORM_EPS) * gain


def _rope(x, cos, sin_signed, lo):
    sw = jnp.where(lo, pltpu.roll(x, LANES - HEAD_DIM // 2, 1), pltpu.roll(x, HEAD_DIM // 2, 1))
    return x * cos + sw * sin_signed


def _prep_kernel(kv_ref, nq_ref, dq_ref, dk_ref, dv_ref, cos_ref, sin_ref, gain_ref, gmat_ref,
                 qn_ref, qr_ref, rows_ref, win_ref, dqo_ref, drows_ref, ksd_ref, vsd_ref, dkb_ref, dvb_ref):
    cos = cos_ref[...]
    sin = sin_ref[...]
    gmat = gmat_ref[...]
    lo32 = (lax.broadcasted_iota(jnp.int32, cos.shape, 1) % HEAD_DIM) < HEAD_DIM // 2
    lo64 = _lane_lo(cos.shape)

    def gain(r):
        return gain_ref[r:r + 1, :]

    def piece(ref, p):
        return ref[:, p * LANES:(p + 1) * LANES]

    for p in range(4):
        qn = _group_norm(piece(nq_ref, p), gain(0), gmat)
        qn_ref[:, p * LANES:(p + 1) * LANES] = qn
        qr_ref[:, p * LANES:(p + 1) * LANES] = (_rope(qn, cos, sin, lo32) * Q_SCALE).astype(BF16)
    rows_ref[:, 0:LANES] = _group_norm(piece(kv_ref, 0), gain(1), gmat)
    rows_ref[:, LANES:2 * LANES] = piece(kv_ref, 1)
    k_slc = _rope(_group_norm(piece(kv_ref, 2), gain(2), gmat), cos, sin, lo32)
    v_slc = piece(kv_ref, 3)
    rows_ref[:, 2 * LANES:3 * LANES] = k_slc
    rows_ref[:, 3 * LANES:4 * LANES] = v_slc
    for hk, (kd, vd) in enumerate(zip(_dup_halves(k_slc, lo64), _dup_halves(v_slc, lo64))):
        ksd_ref[:, hk * LANES:(hk + 1) * LANES] = kd.astype(BF16)
        vsd_ref[:, hk * LANES:(hk + 1) * LANES] = vd.astype(BF16)
    win_ref[:, 0:LANES] = _rope(_group_norm(piece(kv_ref, 4), gain(3), gmat), cos, sin, lo32)
    win_ref[:, LANES:2 * LANES] = piece(kv_ref, 5)
    for p in range(4):
        dq = _rope(_group_norm(piece(dq_ref, p), gain(4), gmat), cos, sin, lo32)
        dqo_ref[:, p * LANES:(p + 1) * LANES] = (dq * Q_SCALE).astype(BF16)
        dk = _rope(_group_norm(piece(dk_ref, p), gain(5), gmat), cos, sin, lo32)
        drows_ref[:, p * LANES:(p + 1) * LANES] = dk
        dkb_ref[:, p * LANES:(p + 1) * LANES] = dk.astype(BF16)
    dv = dv_ref[...]
    drows_ref[:, 4 * LANES:8 * LANES] = dv
    dvb_ref[...] = dv.astype(BF16)


def qk_prep(z, cos, sin, gains, gmat, tm, tiles_per_seq):
    t = z.shape[0]

    def zs(width, col):
        return pl.BlockSpec((tm, width), lambda i: (i, col // width))

    def os_(width):
        return pl.BlockSpec((tm, width), lambda i: (i, 0))

    tab = pl.BlockSpec((tm, LANES), lambda i: (i % tiles_per_seq, 0))
    outs = ((512, F32), (512, BF16), (512, F32), (256, F32), (512, BF16), (1024, F32),
            (256, BF16), (256, BF16), (512, BF16), (512, BF16))
    return pl.pallas_call(
        _prep_kernel,
        grid=(t // tm,),
        in_specs=[zs(768, C_KV), zs(512, C_NQ), zs(512, C_DQ), zs(512, C_DK), zs(512, C_DV), tab, tab,
                  pl.BlockSpec((8, LANES), lambda i: (0, 0)), pl.BlockSpec((LANES, LANES), lambda i: (0, 0))],
        out_specs=[os_(w) for w, _ in outs],
        out_shape=[jax.ShapeDtypeStruct((t, w), dt) for w, dt in outs],
        compiler_params=_cparams(("arbitrary",)),
        name="qk_prep",
    )(z, z, z, z, z, cos, sin, gains, gmat)


def _conv_kernel(ab_ref, ac_ref, ax_ref, glu_ref, pa_ref, pd_ref, wa_ref, ba_ref, wd_ref, bd_ref, lg_ref, lb_ref,
                 ua_ref, ud_ref, sa_ref, sd_ref, bufa, bufd, *, tm, ha, hd, stride, nt):
    i = pl.program_id(1)

    @pl.when(i == 0)
    def _():
        bufa[0:ha, :] = pa_ref[...]
        bufd[0:hd, :] = pd_ref[...]

    if nt > 1:
        @pl.when(i > 0)
        def _():
            bufa[0:ha, :] = bufa[tm:tm + ha, :]
            bufd[0:hd, :] = bufd[tm:tm + hd, :]

    bufa[ha:ha + tm, :] = ac_ref[...] * ax_ref[...]
    ya = ba_ref[...]
    for k in range(A_CONV):
        ya = ya + wa_ref[k:k + 1, :] * bufa[pl.ds(ha - (A_CONV - 1 - k) * stride, tm), :]
    ua_ref[...] = (ab_ref[...] * ya).astype(ua_ref.dtype)

    glu = glu_ref[...]
    bufd[hd:hd + tm, :] = glu[:, :D_WIDTH] * _sigmoid(glu[:, D_WIDTH:])
    yd = bd_ref[...]
    for k in range(D_CONV):
        yd = yd + wd_ref[k:k + 1, :] * bufd[pl.ds(hd - (D_CONV - 1 - k) * stride, tm), :]
    mu = jnp.mean(yd, axis=-1, keepdims=True)
    var = jnp.mean(jnp.square(yd - mu), axis=-1, keepdims=True)
    y = (yd - mu) * lax.rsqrt(var + NORM_EPS) * lg_ref[...] + lb_ref[...]
    ud_ref[...] = _silu(y).astype(ud_ref.dtype)

    @pl.when(i == nt - 1)
    def _():
        sa_ref[...] = bufa[ha + tm - (A_CONV - 1) * stride:ha + tm, :]
        sd_ref[...] = bufd[hd + tm - (D_CONV - 1) * stride:hd + tm, :]


def conv_branches(z, past_a, past_d, wa, ba, wd, bd, lg, lb, nseq, tm, stride):
    t = z.shape[0]
    nt = t // nseq // tm
    ha, hd = past_a.shape[1], past_d.shape[1]
    if nt > 1:
        assert tm >= ha and tm >= hd

    def zs(width, col):
        return pl.BlockSpec((tm, width), lambda s, i: (s * nt + i, col // width))

    def cs(shape):
        return pl.BlockSpec(shape, lambda s, i: (0, 0))

    sa_rows, sd_rows = (A_CONV - 1) * stride, (D_CONV - 1) * stride
    return pl.pallas_call(
        functools.partial(_conv_kernel, tm=tm, ha=ha, hd=hd, stride=stride, nt=nt),
        grid=(nseq, nt),
        in_specs=[zs(512, C_AB), zs(512, C_AC), zs(512, C_AX), zs(1024, C_GLU),
                  pl.BlockSpec((None, ha, A_WIDTH), lambda s, i: (s, 0, 0)),
                  pl.BlockSpec((None, hd, D_WIDTH), lambda s, i: (s, 0, 0)),
                  cs((A_CONV, A_WIDTH)), cs((1, A_WIDTH)), cs((D_CONV, D_WIDTH)), cs((1, D_WIDTH)),
                  cs((1, D_WIDTH)), cs((1, D_WIDTH))],
        out_specs=[pl.BlockSpec((tm, A_WIDTH), lambda s, i: (s * nt + i, 0)),
                   pl.BlockSpec((tm, D_WIDTH), lambda s, i: (s * nt + i, 0)),
                   pl.BlockSpec((None, sa_rows, A_WIDTH), lambda s, i: (s, 0, 0)),
                   pl.BlockSpec((None, sd_rows, D_WIDTH), lambda s, i: (s, 0, 0))],
        out_shape=[jax.ShapeDtypeStruct((t, A_WIDTH), BF16), jax.ShapeDtypeStruct((t, D_WIDTH), BF16),
                   jax.ShapeDtypeStruct((nseq, sa_rows, A_WIDTH), F32),
                   jax.ShapeDtypeStruct((nseq, sd_rows, D_WIDTH), F32)],
        scratch_shapes=[pltpu.VMEM((ha + tm, A_WIDTH), F32), pltpu.VMEM((hd + tm, D_WIDTH), F32)],
        compiler_params=_cparams(("arbitrary", "arbitrary")),
        name="conv_branches",
    )(z, z, z, z, past_a, past_d, wa, ba, wd, bd, lg, lb)


def _merge_kernel(x_ref, gt_ref, ua_ref, uc_ref, ud_ref, oc_ref, os_ref, ow_ref, gc_ref, gs_ref, gw_ref,
                  m0_ref, m1_ref, m2_ref, m3_ref, wa_ref, wb_ref, wc_ref, wd_ref, wo_ref, o_ref):
    tm = x_ref.shape[0]
    ub = (_sigmoid(gc_ref[...]) * oc_ref[...] + _sigmoid(gs_ref[...]) * os_ref[...]
          + _sigmoid(gw_ref[...]) * ow_ref[...]).astype(BF16)
    merged = _sigmoid(m0_ref[...]) * _dot(ua_ref[...], wa_ref[...])
    merged += _sigmoid(m1_ref[...]) * _dot(ub, wb_ref[...])
    merged += _sigmoid(m2_ref[...]) * _dot(uc_ref[...], wc_ref[...])
    merged += _sigmoid(m3_ref[...]) * _dot(ud_ref[...], wd_ref[...])
    y = _dot(merged.astype(BF16), wo_ref[...])
    o_ref[...] = x_ref[...] + _rows_to(gt_ref[...], tm) * y


def merge_out(x, mod, z, ua, uc, ud, o_cmp, o_slc, o_win, w_a, w_b, w_c, w_d, w_o, tm, tiles_per_group):
    t, d = x.shape

    def ts(width):
        return pl.BlockSpec((tm, width), lambda i: (i, 0))

    def zs(width, col):
        return pl.BlockSpec((tm, width), lambda i: (i, col // width))

    def ws(shape):
        return pl.BlockSpec(shape, lambda i: (0, 0))

    return pl.pallas_call(
        _merge_kernel,
        grid=(t // tm,),
        in_specs=[ts(d), _mod_spec(mod, 5, tiles_per_group, 1), ts(512), ts(512), ts(512), ts(512), ts(512), ts(512),
                  zs(512, C_GC), zs(512, C_GS), zs(512, C_GW),
                  zs(1024, C_MERGE), zs(1024, C_MERGE + 1024), zs(1024, C_MERGE + 2048), zs(1024, C_MERGE + 3072),
                  ws((512, d)), ws((512, d)), ws((512, d)), ws((512, d)), ws((d, d))],
        out_specs=ts(d),
        out_shape=jax.ShapeDtypeStruct((t, d), F32),
        compiler_params=_cparams(("arbitrary",)),
        name="merge_out",
    )(x, mod, ua, uc, ud, o_cmp, o_slc, o_win, z, z, z, z, z, z, z, w_a, w_b, w_c, w_d, w_o)


def _compress_kernel(r_ref, w_ref, o_ref):
    x = r_ref[...]
    nb = x.shape[0] // NSA_BLOCK
    o_ref[...] = jnp.sum(x.reshape(nb, NSA_BLOCK, x.shape[1]) * w_ref[...][None], axis=1)


def nsa_compress(rows, w2, tm=512):
    t = rows.shape[0]
    return pl.pallas_call(
        _compress_kernel,
        grid=(t // tm,),
        in_specs=[pl.BlockSpec((tm, 256), lambda i: (i, 0)), pl.BlockSpec((NSA_BLOCK, 256), lambda i: (0, 0))],
        out_specs=pl.BlockSpec((tm // NSA_BLOCK, 256), lambda i: (i, 0)),
        out_shape=jax.ShapeDtypeStruct((t // NSA_BLOCK, 256), F32),
        compiler_params=_cparams(("arbitrary",)),
        name="nsa_compress",
    )(rows, w2)


def _masked_softmax(s, mask):
    s = jnp.where(mask, s, NEG)
    p = jnp.exp(s - jnp.max(s, axis=-1, keepdims=True))
    p = jnp.where(mask, p, 0.0)
    return p / jnp.maximum(jnp.sum(p, axis=-1, keepdims=True), 1e-30)


def _top_blocks(imp, n_sel):
    idx = lax.broadcasted_iota(jnp.int32, imp.shape, 1)
    big = imp.shape[1]
    sel = jnp.zeros(imp.shape, F32)
    for _ in range(n_sel):
        m = jnp.max(imp, axis=-1, keepdims=True)
        first = jnp.min(jnp.where(imp == m, idx, big), axis=-1, keepdims=True)
        hit = idx == first
        sel = jnp.where(hit, 1.0, sel)
        imp = jnp.where(hit, -3.0, imp)
    return sel


def _cmp_select(q, kdup, vdup, qpos, nb, n_real, n_sel, blocks_on_lanes):
    tq = q.shape[0]
    qlo = _lane_lo((tq, LANES))
    blk = lax.broadcasted_iota(jnp.int32, (1, nb), 1)
    complete = ((blk + 1) * NSA_BLOCK - 1 <= qpos) & (blk < n_real)
    cur = qpos // NSA_BLOCK
    outs, sels = [], []
    for hk in range(NSA_KV_HEADS):
        imp = jnp.zeros((tq, nb), F32)
        vd = vdup[hk].astype(BF16)
        for a in (2 * hk, 2 * hk + 1):
            qp = q[:, a * LANES:(a + 1) * LANES]
            pair = []
            for par in range(2):
                qm = jnp.where(qlo if par == 0 else ~qlo, qp, 0.0)
                if blocks_on_lanes:
                    s = jnp.dot(qm, kdup[hk], precision=lax.Precision.HIGHEST, preferred_element_type=F32)
                else:
                    s = _dot_nt(qm, kdup[hk], precision=lax.Precision.HIGHEST)
                pc = _masked_softmax(s * ATT_SCALE, complete)
                imp = imp + pc
                pair.append(_dot_nt(pc.astype(BF16), vd) if blocks_on_lanes else _dot(pc.astype(BF16), vd))
            outs.append(jnp.where(qlo, pair[0], pair[1]))
        forced = (blk == cur) | (blk == 0)
        imp = jnp.where(forced, NSA_GROUP + 1.0, imp)
        imp = jnp.where(blk <= cur, imp, -1.0)
        imp = jnp.where(blk < n_real, imp, -2.0)
        sels.append(_top_blocks(imp, n_sel))
    return jnp.concatenate(outs, axis=1), sels


def _cmpsel_kernel(q_ref, c_ref, o_ref, sel_ref, *, tq, n_sel):
    qpos = pl.program_id(1) * tq + lax.broadcasted_iota(jnp.int32, (tq, 1), 0)
    nb = c_ref.shape[0]
    lo = _lane_lo((nb, LANES))
    kdup = _dup_halves(c_ref[:, 0:LANES], lo)
    vdup = _dup_halves(c_ref[:, LANES:2 * LANES], lo)
    o, sels = _cmp_select(q_ref[...], kdup, vdup, qpos, nb, nb, n_sel, False)
    o_ref[...] = o
    sel_ref[...] = jnp.concatenate(sels, axis=1).astype(sel_ref.dtype)


def nsa_cmp_select(q_n, cmp, nseq, tq=256):
    t = q_n.shape[0]
    nq = t // nseq // tq
    nb = cmp.shape[0] // nseq
    return pl.pallas_call(
        functools.partial(_cmpsel_kernel, tq=tq, n_sel=min(NSA_TOPK, nb)),
        grid=(nseq, nq),
        in_specs=[pl.BlockSpec((tq, 512), lambda b, i: (b * nq + i, 0)),
                  pl.BlockSpec((nb, 256), lambda b, i: (b, 0))],
        out_specs=[pl.BlockSpec((tq, 512), lambda b, i: (b * nq + i, 0)),
                   pl.BlockSpec((tq, 2 * nb), lambda b, i: (b * nq + i, 0))],
        out_shape=[jax.ShapeDtypeStruct((t, 512), F32), jax.ShapeDtypeStruct((t, 2 * nb), BF16)],
        compiler_params=_cparams(("arbitrary", "arbitrary")),
        name="nsa_cmp_select",
    )(q_n, cmp)


def _online_step(s, m, acc, v_ext):
    tk = s.shape[1]
    m_new = jnp.maximum(m, jnp.max(s, axis=-1, keepdims=True))
    p = jnp.exp2(s - jnp.concatenate([m_new] * (tk // LANES), axis=1))
    alpha = jnp.exp2(m - m_new)
    if acc.shape[1] != LANES:
        alpha = jnp.concatenate([alpha] * (acc.shape[1] // LANES), axis=1)
    return m_new, alpha * acc + _dot(p.astype(BF16), v_ext)


def _slc_kernel(q_ref, k_ref, v_ref, sel_ref, e_ref, o_ref, *, tq, tk, nb):
    qi = pl.program_id(1)
    n_full = (qi * tq) // tk
    rows = NSA_GROUP * tq
    qlo = _lane_lo((tq, LANES))
    klo = _lane_lo((tk, LANES))
    zero = jnp.zeros((tq, LANES), BF16)
    qpos = qi * tq + lax.broadcasted_iota(jnp.int32, (tq, 1), 0)
    col = lax.broadcasted_iota(jnp.int32, (1, tk), 1)
    for hk in range(NSA_KV_HEADS):
        parts = []
        for c in range(NSA_GROUP):
            h = hk * NSA_GROUP + c
            qp = q_ref[:, (h // 2) * LANES:(h // 2 + 1) * LANES]
            parts.append(jnp.where(qlo, qp, zero) if h % 2 == 0 else jnp.where(qlo, zero, qp))
        qst = jnp.concatenate(parts, axis=0)
        sel = sel_ref[:, hk * nb:(hk + 1) * nb]

        def step(j, carry, diag, qst=qst, sel=sel, hk=hk):
            m, acc = carry
            off = pl.multiple_of(j * tk, tk)
            k = k_ref[pl.ds(off, tk), hk * LANES:(hk + 1) * LANES]
            v = v_ref[pl.ds(off, tk), hk * LANES:(hk + 1) * LANES]
            vx = jnp.where(klo, v, jnp.ones_like(v))
            bias = (_dot(sel, e_ref[j]) - 1.0) * (-NEG)
            if diag:
                bias = jnp.where(j * tk + col <= qpos, bias, NEG)
            s = _dot_nt(qst, k) + jnp.concatenate([bias] * NSA_GROUP, axis=0)
            return _online_step(s, m, acc, vx)

        init = (jnp.full((rows, LANES), NEG, F32), jnp.zeros((rows, LANES), F32))
        carry = lax.fori_loop(0, n_full, lambda j, cr, step=step: step(j, cr, False), init)
        _, acc = step(n_full, carry, True)
        o = acc / jnp.maximum(pltpu.roll(acc, HEAD_DIM, 1), 1e-30)
        for pair in range(NSA_GROUP // 2):
            a = hk * (NSA_GROUP // 2) + pair
            o_even = o[(2 * pair) * tq:(2 * pair + 1) * tq]
            o_odd = o[(2 * pair + 1) * tq:(2 * pair + 2) * tq]
            o_ref[:, a * LANES:(a + 1) * LANES] = jnp.where(qlo, o_even, pltpu.roll(o_odd, HEAD_DIM, 1))


def nsa_slc(q_rot, ksd, vsd, sel, emat, nseq, tq=256):
    t = q_rot.shape[0]
    s = t // nseq
    nq = s // tq
    nk, nb, tk = emat.shape
    return pl.pallas_call(
        functools.partial(_slc_kernel, tq=tq, tk=tk, nb=nb),
        grid=(nseq, nq),
        in_specs=[pl.BlockSpec((tq, 512), lambda b, i: (b * nq + i, 0)),
                  pl.BlockSpec((s, 256), lambda b, i: (b, 0)),
                  pl.BlockSpec((s, 256), lambda b, i: (b, 0)),
                  pl.BlockSpec((tq, 2 * nb), lambda b, i: (b * nq + i, 0)),
                  pl.BlockSpec((nk, nb, tk), lambda b, i: (0, 0, 0))],
        out_specs=pl.BlockSpec((tq, 512), lambda b, i: (b * nq + i, 0)),
        out_shape=jax.ShapeDtypeStruct((t, 512), F32),
        compiler_params=_cparams(("arbitrary", "arbitrary")),
        name="nsa_slc",
    )(q_rot, ksd, vsd, sel, emat)


def _win_kernel(q_ref, wp_ref, wc_ref, o_ref, *, tq):
    i = pl.program_id(1)
    lo = _lane_lo((tq, LANES))
    qpos = i * tq + lax.broadcasted_iota(jnp.int32, (tq, 1), 0)
    chunks = []
    for off, ref in ((-tq, wp_ref), (0, wc_ref)):
        kpos = i * tq + off + lax.broadcasted_iota(jnp.int32, (1, tq), 1)
        mask = (kpos >= 0) & (kpos <= qpos) & (qpos - kpos <= NSA_WINDOW)
        chunks.append((mask, _dup_halves(ref[:, 0:LANES], lo), _dup_halves(ref[:, LANES:2 * LANES], lo)))
    zero = jnp.zeros((tq, LANES), BF16)
    for a in range(NSA_HEADS // 2):
        pair = []
        for par in range(2):
            hk = (2 * a + par) // NSA_GROUP
            qp = q_ref[:, a * LANES:(a + 1) * LANES]
            qm = jnp.where(lo, qp, zero) if par == 0 else jnp.where(lo, zero, qp)
            ss = [jnp.where(mask, _dot_nt(qm, kd[hk].astype(BF16)), NEG) for mask, kd, _ in chunks]
            m = jnp.maximum(jnp.max(ss[0], axis=-1, keepdims=True), jnp.max(ss[1], axis=-1, keepdims=True))
            ps = [jnp.where(ch[0], jnp.exp2(s - m), 0.0) for s, ch in zip(ss, chunks)]
            l = jnp.sum(ps[0], axis=-1, keepdims=True) + jnp.sum(ps[1], axis=-1, keepdims=True)
            inv = 1.0 / jnp.maximum(l, 1e-30)
            o = _dot((ps[0] * inv).astype(BF16), chunks[0][2][hk].astype(BF16))
            o += _dot((ps[1] * inv).astype(BF16), chunks[1][2][hk].astype(BF16))
            pair.append(o)
        o_ref[:, a * LANES:(a + 1) * LANES] = jnp.where(lo, pair[0], pair[1])


def nsa_win(q_rot, win_rows, nseq, tq=NSA_WINDOW):
    t = q_rot.shape[0]
    nq = t // nseq // tq
    return pl.pallas_call(
        functools.partial(_win_kernel, tq=tq),
        grid=(nseq, nq),
        in_specs=[pl.BlockSpec((tq, 512), lambda b, i: (b * nq + i, 0)),
                  pl.BlockSpec((tq, 256), lambda b, i: (b * nq + jnp.maximum(i - 1, 0), 0)),
                  pl.BlockSpec((tq, 256), lambda b, i: (b * nq + i, 0))],
        out_specs=pl.BlockSpec((tq, 512), lambda b, i: (b * nq + i, 0)),
        out_shape=jax.ShapeDtypeStruct((t, 512), F32),
        compiler_params=_cparams(("arbitrary", "arbitrary")),
        name="nsa_win",
    )(q_rot, win_rows, win_rows)


def _diff_lambda(lam_ref, lam_init):
    lp = lam_ref[...]
    a = jnp.sum(lp[0:1] * lp[1:2], axis=-1, keepdims=True)
    b = jnp.sum(lp[2:3] * lp[3:4], axis=-1, keepdims=True)
    return jnp.exp(a) - jnp.exp(b) + lam_init


def _diff_finish(o1, o2, lam, g, lam_init):
    o = o1 - lam * o2
    ms = jnp.mean(o * o, axis=-1, keepdims=True)
    return o * lax.rsqrt(ms + NORM_EPS) * g * (1.0 - lam_init)


def _diff_kernel(q_ref, k_ref, v_ref, lam_ref, g_ref, o_ref, *, tq, tk, lam_init):
    qi = pl.program_id(2)
    n_full = (qi * tq) // tk
    lo = _lane_lo((tq, LANES))
    q = q_ref[...]
    zero = jnp.zeros_like(q)
    qst = jnp.concatenate([jnp.where(lo, q, zero), jnp.where(lo, zero, q)], axis=0)
    ones = jnp.ones((tk, LANES), BF16)
    qpos = qi * tq + lax.broadcasted_iota(jnp.int32, (tq, 1), 0)
    col = lax.broadcasted_iota(jnp.int32, (1, tk), 1)

    def step(j, carry, diag):
        m, acc = carry
        off = pl.multiple_of(j * tk, tk)
        k = k_ref[pl.ds(off, tk), :]
        v = v_ref[pl.ds(off, tk), :]
        s = _dot_nt(qst, k)
        if diag:
            bias = jnp.where(j * tk + col <= qpos, 0.0, NEG)
            s = s + jnp.concatenate([bias, bias], axis=0)
        return _online_step(s, m, acc, jnp.concatenate([v, ones], axis=1))

    init = (jnp.full((2 * tq, LANES), NEG, F32), jnp.zeros((2 * tq, 2 * LANES), F32))
    carry = lax.fori_loop(0, n_full, lambda j, cr: step(j, cr, False), init)
    _, acc = step(n_full, carry, True)
    o = acc[:, 0:LANES] / jnp.maximum(acc[:, LANES:2 * LANES], 1e-30)
    lam = _diff_lambda(lam_ref, lam_init)
    o_ref[...] = _diff_finish(o[0:tq], o[tq:2 * tq], lam, g_ref[...], lam_init).astype(o_ref.dtype)


def diff_attn(dq, dk, dv, lam_p, out_g, nseq, lam_init, tq=256, tk=1024):
    t = dq.shape[0]
    s = t // nseq
    nq = s // tq
    return pl.pallas_call(
        functools.partial(_diff_kernel, tq=tq, tk=tk, lam_init=lam_init),
        grid=(nseq, DIFF_HEADS, nq),
        in_specs=[pl.BlockSpec((tq, LANES), lambda b, h, i: (b * nq + i, h)),
                  pl.BlockSpec((s, LANES), lambda b, h, i: (b, h)),
                  pl.BlockSpec((s, LANES), lambda b, h, i: (b, h)),
                  pl.BlockSpec((4, HEAD_DIM), lambda b, h, i: (0, 0)),
                  pl.BlockSpec((1, LANES), lambda b, h, i: (0, 0))],
        out_specs=pl.BlockSpec((tq, LANES), lambda b, h, i: (b * nq + i, h)),
        out_shape=jax.ShapeDtypeStruct((t, 512), BF16),
        compiler_params=_cparams(("arbitrary", "arbitrary", "arbitrary")),
        name="diff_attn",
    )(dq, dk, dv, lam_p, out_g)


QPAD = 8


def _pad_keys(x):
    return jnp.concatenate([x, jnp.zeros((LANES - x.shape[0], x.shape[1]), x.dtype)], axis=0)


def _nsa_sample_kernel(pt_ref, *refs, n_pages, past_len, n_new, n_sel):
    chunks = refs[:n_pages + 1]
    qn_ref, qr_ref, wst_ref, wnew_ref, wt_ref, selm_ref, oc_ref, os_ref, ow_ref = refs[n_pages + 1:]
    n_real = (past_len + n_new + NSA_BLOCK - 1) // NSA_BLOCK
    nbp = selm_ref.shape[1]
    page = chunks[0].shape[3]

    def twice(x):
        return jnp.concatenate([x, x], axis=0)

    wt = wt_ref[...]
    y = jnp.concatenate([ch[0:2].reshape(4 * HEAD_DIM, page) * wt for ch in chunks], axis=1)
    hi = y.astype(BF16)
    lo = (y - hi.astype(F32)).astype(BF16)
    selm = selm_ref[...]
    cmp_t = _dot(hi, selm) + _dot(lo, selm)

    tok = lax.broadcasted_iota(jnp.int32, (QPAD, 1), 0)
    qpos = past_len + tok
    kdup = [twice(cmp_t[hk * HEAD_DIM:(hk + 1) * HEAD_DIM]) for hk in range(NSA_KV_HEADS)]
    vdup = [twice(cmp_t[(2 + hk) * HEAD_DIM:(3 + hk) * HEAD_DIM]) for hk in range(NSA_KV_HEADS)]
    o_cmp, sels = _cmp_select(qn_ref[...], kdup, vdup, qpos, nbp, n_real, n_sel, True)
    oc_ref[...] = o_cmp

    qlo = _lane_lo((QPAD, LANES))
    zero = jnp.zeros((QPAD, LANES), BF16)
    qst = []
    for hk in range(NSA_KV_HEADS):
        parts = []
        for c in range(NSA_GROUP):
            h = hk * NSA_GROUP + c
            qp = qr_ref[:, (h // 2) * LANES:(h // 2 + 1) * LANES]
            parts.append(jnp.where(qlo, qp, zero) if h % 2 == 0 else jnp.where(qlo, zero, qp))
        qst.append(jnp.concatenate(parts, axis=0))
    rows = NSA_GROUP * QPAD
    qpos_st = jnp.concatenate([qpos] * NSA_GROUP, axis=0)
    key_new = lax.broadcasted_iota(jnp.int32, (1, LANES), 1)
    new_valid = (key_new < n_new) & (past_len + key_new <= qpos_st)

    def attend(kvs, biases, hk):
        ss = [_dot(qst[hk], twice(k).astype(BF16)) + b for (k, _), b in zip(kvs, biases)]
        m = functools.reduce(jnp.maximum, [jnp.max(s, axis=-1, keepdims=True) for s in ss])
        ps = [jnp.exp2(s - m) for s in ss]
        l = functools.reduce(lambda a, b: a + b, [jnp.sum(p, axis=-1, keepdims=True) for p in ps])
        inv = 1.0 / jnp.maximum(l, 1e-30)
        o = jnp.zeros((rows, LANES), F32)
        for p, (_, v) in zip(ps, kvs):
            o += _dot_nt((p * inv).astype(BF16), twice(v).astype(BF16))
        return o

    def emit(o_ref, per_hk):
        for a in range(NSA_HEADS // 2):
            hk, c = (2 * a) // NSA_GROUP, (2 * a) % NSA_GROUP
            o = per_hk[hk]
            o_ref[:, a * LANES:(a + 1) * LANES] = jnp.where(qlo, o[c * QPAD:(c + 1) * QPAD], o[(c + 1) * QPAD:(c + 2) * QPAD])

    per_hk = []
    for hk in range(NSA_KV_HEADS):
        selx = _dot_nt(sels[hk].astype(BF16), selm)
        selx = jnp.concatenate([selx] * NSA_GROUP, axis=0)
        kvs, biases = [], []
        for j, ch in enumerate(chunks):
            kvs.append((ch[2, hk], ch[3, hk]))
            bias = (selx[:, j * page:(j + 1) * page] - 1.0) * (-NEG)
            biases.append(jnp.where(new_valid, bias, NEG) if j == n_pages else bias)
        per_hk.append(attend(kvs, biases, hk))
    emit(os_ref, per_hk)

    wbuf = wst_ref.shape[3]
    kpos = past_len - wbuf + lax.broadcasted_iota(jnp.int32, (1, wbuf), 1)
    bias_st = jnp.where((kpos >= 0) & (kpos <= qpos_st) & (qpos_st - kpos <= NSA_WINDOW), 0.0, NEG)
    bias_new = jnp.where(new_valid & (qpos_st - (past_len + key_new) <= NSA_WINDOW), 0.0, NEG)
    per_hk = []
    for hk in range(NSA_KV_HEADS):
        kvs = [(wst_ref[0, hk], wst_ref[1, hk]), (wnew_ref[0, hk], wnew_ref[1, hk])]
        per_hk.append(attend(kvs, [bias_st, bias_new], hk))
    emit(ow_ref, per_hk)


def nsa_sample(layer, page_table, cache_t, new_t, q_n, q_rot, win_t, win_new_t, wt, selm, n_new):
    nb_, n_pages = page_table.shape
    page = cache_t.shape[5]
    past_len = n_pages * page
    n_real = (past_len + n_new + NSA_BLOCK - 1) // NSA_BLOCK
    wbuf = win_t.shape[5]
    kv = (NSA_KV_HEADS, HEAD_DIM)

    def page_spec(j):
        return pl.BlockSpec((None, None, 4) + kv + (page,), lambda b, pt: (layer, pt[b, j], 0, 0, 0, 0))

    def seq(shape):
        return pl.BlockSpec((None,) + shape, lambda b, pt: (b,) + (0,) * len(shape))

    def const(shape):
        return pl.BlockSpec(shape, lambda b, pt: (0,) * len(shape))

    grid_spec = pltpu.PrefetchScalarGridSpec(
        num_scalar_prefetch=1,
        grid=(nb_,),
        in_specs=[page_spec(j) for j in range(n_pages)] + [
            seq((4,) + kv + (page,)), seq((QPAD, 512)), seq((QPAD, 512)),
            pl.BlockSpec((None, None, 2) + kv + (wbuf,), lambda b, pt: (layer, b, 0, 0, 0, 0)),
            seq((2,) + kv + (LANES,)), const(wt.shape), const(selm.shape)],
        out_specs=[seq((QPAD, 512)), seq((QPAD, 512)), seq((QPAD, 512))],
    )
    return pl.pallas_call(
        functools.partial(_nsa_sample_kernel, n_pages=n_pages, past_len=past_len, n_new=n_new,
                          n_sel=min(NSA_TOPK, n_real)),
        grid_spec=grid_spec,
        out_shape=[jax.ShapeDtypeStruct((nb_, QPAD, 512), F32)] * 3,
        compiler_params=_cparams(("arbitrary",)),
        name="nsa_sample",
    )(page_table, *([cache_t] * n_pages), new_t, q_n, q_rot, win_t, win_new_t, wt, selm)


def _diff_sample_kernel(pt_ref, *refs, n_pages, past_len, n_new, lam_init):
    pages = refs[:n_pages]
    new_ref, q_ref, lam_ref, g_ref, o_ref = refs[n_pages:]
    lo = _lane_lo((QPAD, LANES))
    tok = lax.broadcasted_iota(jnp.int32, (2 * QPAD, 1), 0) % QPAD
    key_new = lax.broadcasted_iota(jnp.int32, (1, LANES), 1)
    new_valid = (key_new < n_new) & (key_new <= tok)
    lam = _diff_lambda(lam_ref, lam_init)
    q = q_ref[...]
    new = new_ref[...]
    zero = jnp.zeros((QPAD, LANES), BF16)
    page = pages[0].shape[0] // (2 * DIFF_HEADS)
    for h in range(DIFF_HEADS):
        qp = q[:, h * LANES:(h + 1) * LANES]
        qst = jnp.concatenate([jnp.where(lo, qp, zero), jnp.where(lo, zero, qp)], axis=0)
        ks = [pages[j][pl.ds(h, page, stride=2 * DIFF_HEADS), :] for j in range(n_pages)]
        ks.append(_pad_keys(new[:, h * LANES:(h + 1) * LANES]))
        vs = [pages[j][pl.ds(DIFF_HEADS + h, page, stride=2 * DIFF_HEADS), :] for j in range(n_pages)]
        vs.append(_pad_keys(new[:, (DIFF_HEADS + h) * LANES:(DIFF_HEADS + h + 1) * LANES]))
        ss = [_dot_nt(qst, k.astype(BF16)) for k in ks]
        ss[-1] = jnp.where(new_valid, ss[-1], NEG)
        m = functools.reduce(jnp.maximum, [jnp.max(s, axis=-1, keepdims=True) for s in ss])
        ps = [jnp.exp2(s - m) for s in ss]
        ps[-1] = jnp.where(new_valid, ps[-1], 0.0)
        l = functools.reduce(lambda a, b: a + b, [jnp.sum(p, axis=-1, keepdims=True) for p in ps])
        inv = 1.0 / jnp.maximum(l, 1e-30)
        o = jnp.zeros((QPAD, LANES), F32)
        for p, v in zip(ps, vs):
            pn = p * inv
            a = (pn[0:QPAD] - lam * pn[QPAD:2 * QPAD]).astype(BF16)
            o += _dot(a, v.astype(BF16))
        ms = jnp.mean(o * o, axis=-1, keepdims=True)
        o_ref[:, h * LANES:(h + 1) * LANES] = (o * lax.rsqrt(ms + NORM_EPS) * g_ref[...] * (1.0 - lam_init)).astype(o_ref.dtype)


def diff_sample(layer, page_table, cache, new_rows, dq, lam_p, out_g, n_new, lam_init):
    nb_, n_pages = page_table.shape
    page = cache.shape[2] // (2 * DIFF_HEADS)

    def page_spec(j):
        return pl.BlockSpec((None, None, page * 2 * DIFF_HEADS, LANES), lambda b, pt: (layer, pt[b, j], 0, 0))

    grid_spec = pltpu.PrefetchScalarGridSpec(
        num_scalar_prefetch=1,
        grid=(nb_,),
        in_specs=[page_spec(j) for j in range(n_pages)] + [
            pl.BlockSpec((None, QPAD, 1024), lambda b, pt: (b, 0, 0)),
            pl.BlockSpec((None, QPAD, 512), lambda b, pt: (b, 0, 0)),
            pl.BlockSpec((4, HEAD_DIM), lambda b, pt: (0, 0)),
            pl.BlockSpec((1, LANES), lambda b, pt: (0, 0))],
        out_specs=pl.BlockSpec((None, QPAD, 512), lambda b, pt: (b, 0, 0)),
    )
    return pl.pallas_call(
        functools.partial(_diff_sample_kernel, n_pages=n_pages, past_len=n_pages * page, n_new=n_new, lam_init=lam_init),
        grid_spec=grid_spec,
        out_shape=jax.ShapeDtypeStruct((nb_, QPAD, 512), F32),
        compiler_params=_cparams(("arbitrary",)),
        name="diff_sample",
    )(page_table, *([cache] * n_pages), new_rows, dq, lam_p, out_g)


def _mix_weight(w):
    d = w.shape[0]
    splits = (A_WIDTH, A_WIDTH, A_WIDTH, 512, 768, 3 * NSA_HEADS, 512, 512, 512, 2 * D_WIDTH, N_BRANCH * d)
    offs = [0]
    for s in splits:
        offs.append(offs[-1] + s)
    part = {n: w[:, offs[i]:offs[i + 1]] for i, n in enumerate(
        ("ab", "ac", "ax", "nq", "kv", "g", "dq", "dk", "dv", "glu", "merge"))}
    g = part["g"].reshape(d, NSA_HEADS, 3)
    gates = [jnp.repeat(g[:, :, k], HEAD_DIM, axis=1) for k in range(3)]
    pad = jnp.zeros((d, C_MERGE - 768), w.dtype)
    cols = [part["kv"], pad, part["merge"], part["glu"], part["ab"], part["ac"], part["ax"], part["nq"],
            part["dq"], part["dk"], part["dv"]] + gates
    out = jnp.concatenate(cols, axis=1).astype(BF16)
    assert out.shape[1] == MIX_W
    return out


def _rope_tables(pos):
    half = HEAD_DIM // 2
    inv = ROPE_THETA ** (-jnp.arange(half, dtype=F32) / half)
    ang = pos.astype(F32)[:, None] * inv[None, :]
    cos, sin = jnp.cos(ang), jnp.sin(ang)
    return jnp.tile(jnp.concatenate([cos, cos], axis=1), (1, 2)), jnp.tile(jnp.concatenate([-sin, sin], axis=1), (1, 2))


def _tile_gain(g):
    return jnp.tile(g.astype(F32), LANES // HEAD_DIM)


def _expansion(n_tiles, nb_rows, tk):
    j = jnp.arange(n_tiles)[:, None, None]
    n = jnp.arange(nb_rows)[None, :, None]
    k = jnp.arange(tk)[None, None, :]
    return ((j * tk + k) // NSA_BLOCK == n).astype(BF16)


def _tokens_last(x, nb_, n_new, lead):
    y = jnp.moveaxis(x.reshape((n_new, nb_) + lead), 0, -1)
    return jnp.pad(y, [(0, 0)] * (y.ndim - 1) + [(0, LANES - n_new)])


def _to_seq_major(x, nb_, n_new, pad_to=None):
    y = jnp.swapaxes(x.reshape(n_new, nb_, x.shape[-1]), 0, 1)
    if pad_to is not None and pad_to > n_new:
        y = jnp.pad(y, ((0, 0), (0, pad_to - n_new), (0, 0)))
    return y


def _to_step_major(y, n_new):
    return jnp.swapaxes(y[:, :n_new], 0, 1).reshape(-1, y.shape[-1])


def kernel(x_prompt, x_sample, cache_nsa, cache_diff, state_nsa_win, state_conv_a, state_conv_d, page_table,
           c_prompt, c_sample, ada_w, ada_b, norm_g, ffn1_up, ffn1_down, ffn2_up, ffn2_down, w_mix_in,
           conv_a_w, conv_a_b, a_out, nsa_qk_g, nsa_cmp_w, nsa_out, diff_qk_g, diff_lambda, diff_out_g, diff_out,
           conv_d_w, conv_d_b, d_norm_g, d_norm_b, d_out, w_out):
    bp, sp, d = x_prompt.shape
    bs, ns, _ = x_sample.shape
    depth = ada_w.shape[0]
    n_pages = page_table.shape[1]
    page = cache_nsa.shape[2]
    past_len = n_pages * page
    wbuf = state_nsa_win.shape[2]

    rows = bp + bs
    rpad = -(-rows // 8) * 8
    c_all = jnp.pad(jnp.concatenate([c_prompt, c_sample], axis=0), ((0, rpad - rows), (0, 0)))
    ada = ada_proj(c_all, ada_w, ada_b)

    xp = x_prompt.reshape(bp * sp, d)
    xs = jnp.swapaxes(x_sample, 0, 1).reshape(ns * bs, d)
    tm_p, tm_s = 512, ns * bs
    tpg_p = sp // tm_p

    cos_p, sin_p = _rope_tables(jnp.arange(sp))
    cos_s, sin_s = _rope_tables(jnp.repeat(past_len + jnp.arange(ns), bs))
    gmat = (jnp.arange(LANES)[:, None] // HEAD_DIM == jnp.arange(LANES)[None, :] // HEAD_DIM).astype(BF16)
    tk_slc = 512
    emat_p = _expansion(sp // tk_slc, sp // NSA_BLOCK, tk_slc)
    n_real_s = (past_len + ns + NSA_BLOCK - 1) // NSA_BLOCK
    nbp_s = -(-n_real_s // LANES) * LANES
    selm_s = _expansion(1, nbp_s, past_len + page)[0].T
    cache_nsa_t = jnp.transpose(cache_nsa, (0, 1, 3, 4, 5, 2))
    win_state_t = jnp.transpose(state_nsa_win, (0, 1, 3, 4, 5, 2))
    cache_diff2 = cache_diff.reshape(depth, cache_diff.shape[1], page * 2 * DIFF_HEADS, 2 * HEAD_DIM)

    zeros_a = jnp.zeros((bp, 8, A_WIDTH), F32)
    zeros_d = jnp.zeros((bp, 32, D_WIDTH), F32)

    outs = {k: [] for k in ("nsa_p", "nsa_s", "diff_p", "diff_s", "win_p", "win_s", "ca_p", "ca_s", "cd_p", "cd_s")}
    for l in range(depth):
        lam_init = 0.8 - 0.6 * math.exp(-0.3 * l)
        mod_p = ada[l, :, 0:bp].reshape(N_ADA, bp, 1, d)
        mod_s = ada[l, :, bp:bp + bs].reshape(N_ADA, 1, bs, d)
        g = norm_g[l]
        w1u, w1d = ffn1_up[l].astype(BF16), ffn1_down[l].astype(BF16)
        w2u, w2d = ffn2_up[l].astype(BF16), ffn2_down[l].astype(BF16)
        wmix = _mix_weight(w_mix_in[l])
        gains = jnp.stack([_tile_gain(nsa_qk_g[l, 0]), _tile_gain(nsa_qk_g[l, 1]), _tile_gain(nsa_qk_g[l, 2]),
                           _tile_gain(nsa_qk_g[l, 3]), _tile_gain(diff_qk_g[l, 0]), _tile_gain(diff_qk_g[l, 1]),
                           jnp.zeros((LANES,), F32), jnp.zeros((LANES,), F32)])
        cw = nsa_cmp_w[l]
        w2 = jnp.concatenate([cw[0], cw[0], cw[1], cw[1]], axis=1)
        wt = jnp.tile(jnp.swapaxes(cw, 1, 2), (1, 1, page // NSA_BLOCK))
        wt = jnp.broadcast_to(wt[:, None], (2, NSA_KV_HEADS, HEAD_DIM, page)).reshape(4 * HEAD_DIM, page)
        wts = [a_out[l].astype(BF16), nsa_out[l].astype(BF16), diff_out[l].astype(BF16), d_out[l].astype(BF16),
               w_out[l].astype(BF16)]
        conv_w = (conv_a_w[l], conv_a_b[l][None], conv_d_w[l], conv_d_b[l][None], d_norm_g[l][None], d_norm_b[l][None])
        lam_p, og = diff_lambda[l], diff_out_g[l][None]

        xp = ffn(xp, mod_p, 0, g[0:1], w1u, w1d, tm_p, tpg_p)
        zp = mix_in(xp, mod_p, g[1:2], wmix, 1024, sp // 1024)
        qn, qr, rows_p, win_p, dq, drows_p, ksd, vsd, dkb, dvb = qk_prep(zp, cos_p, sin_p, gains, gmat, tm_p, tpg_p)
        ua, ud, ca, cd = conv_branches(zp, zeros_a, zeros_d, *conv_w, nseq=bp, tm=tm_p, stride=1)
        cmp_p = nsa_compress(rows_p, w2)
        o_cmp, sel = nsa_cmp_select(qn, cmp_p, bp)
        o_slc = nsa_slc(qr, ksd, vsd, sel, emat_p, bp)
        o_win = nsa_win(qr, win_p, bp)
        uc = diff_attn(dq, dkb, dvb, lam_p, og, bp, lam_init)
        xp = merge_out(xp, mod_p, zp, ua, uc, ud, o_cmp, o_slc, o_win, *wts, tm=tm_p, tiles_per_group=tpg_p)
        xp = ffn(xp, mod_p, 6, g[2:3], w2u, w2d, tm_p, tpg_p)
        outs["nsa_p"].append(rows_p.reshape(bp, sp, 4, NSA_KV_HEADS, HEAD_DIM))
        outs["diff_p"].append(drows_p.reshape(bp, sp, 2, DIFF_HEADS, 2 * HEAD_DIM))
        wkeep = min(NSA_WINDOW, sp)
        outs["win_p"].append(win_p.reshape(bp, sp, 2, NSA_KV_HEADS, HEAD_DIM)[:, sp - wkeep:])
        outs["ca_p"].append(ca)
        outs["cd_p"].append(cd)

        xs = ffn(xs, mod_s, 0, g[0:1], w1u, w1d, tm_s, 1)
        zs = mix_in(xs, mod_s, g[1:2], wmix, tm_s, 1)
        qn, qr, rows_s, win_s, dq, drows_s = qk_prep(zs, cos_s, sin_s, gains, gmat, tm_s, 1)[:6]
        past_a = jnp.swapaxes(state_conv_a[l], 0, 1).reshape(1, (A_CONV - 1) * bs, A_WIDTH)
        past_d = jnp.swapaxes(state_conv_d[l], 0, 1).reshape(1, (D_CONV - 1) * bs, D_WIDTH)
        ua, ud, ca, cd = conv_branches(zs, past_a, past_d, *conv_w, nseq=1, tm=tm_s, stride=bs)
        rows_b = _to_seq_major(rows_s, bs, ns)
        new_t = _tokens_last(rows_s, bs, ns, (4, NSA_KV_HEADS, HEAD_DIM))
        win_new_t = _tokens_last(win_s, bs, ns, (2, NSA_KV_HEADS, HEAD_DIM))
        o_cmp, o_slc, o_win = nsa_sample(l, page_table, cache_nsa_t, new_t, _to_seq_major(qn, bs, ns, QPAD),
                                         _to_seq_major(qr, bs, ns, QPAD), win_state_t, win_new_t, wt, selm_s, ns)
        drows_b = _to_seq_major(drows_s, bs, ns, QPAD)
        uc = diff_sample(l, page_table, cache_diff2, drows_b, _to_seq_major(dq, bs, ns, QPAD), lam_p, og, ns, lam_init)
        xs = merge_out(xs, mod_s, zs, ua, _to_step_major(uc, ns).astype(BF16), ud, _to_step_major(o_cmp, ns),
                       _to_step_major(o_slc, ns), _to_step_major(o_win, ns), *wts, tm=tm_s, tiles_per_group=1)
        xs = ffn(xs, mod_s, 6, g[2:3], w2u, w2d, tm_s, 1)
        outs["nsa_s"].append(rows_b.reshape(bs, ns, 4, NSA_KV_HEADS, HEAD_DIM))
        outs["diff_s"].append(drows_b[:, :ns].reshape(bs, ns, 2, DIFF_HEADS, 2 * HEAD_DIM))
        new_win_t = jnp.concatenate([win_state_t[l][..., ns:], win_new_t[..., :ns]], axis=-1)
        outs["win_s"].append(jnp.moveaxis(new_win_t, -1, 1))
        outs["ca_s"].append(jnp.swapaxes(ca.reshape(A_CONV - 1, bs, A_WIDTH), 0, 1))
        outs["cd_s"].append(jnp.swapaxes(cd.reshape(D_CONV - 1, bs, D_WIDTH), 0, 1))

    y_p = xp.reshape(bp, sp, d)
    y_s = jnp.swapaxes(xs.reshape(ns, bs, d), 0, 1)
    st = {k: jnp.stack(v, axis=0) for k, v in outs.items()}
    return (y_p, y_s, st["nsa_p"], st["nsa_s"], st["diff_p"], st["diff_s"], st["win_p"], st["win_s"],
            st["ca_p"], st["ca_s"], st["cd_p"], st["cd_s"])
```

```python
import functools
import math

import jax
import jax.numpy as jnp
from jax import lax
from jax.experimental import pallas as pl
from jax.experimental.pallas import tpu as pltpu

F32 = jnp.float32
BF16 = jnp.bfloat16

HEAD_DIM = 64
ROPE_THETA = 10000.0
NORM_EPS = 1e-6
A_WIDTH = 512
A_CONV = 3
NSA_HEADS = 8
NSA_KV_HEADS = 2
NSA_GROUP = NSA_HEADS // NSA_KV_HEADS
NSA_BLOCK = 64
NSA_TOPK = 16
NSA_WINDOW = 512
DIFF_HEADS = 4
D_WIDTH = 512
D_CONV = 31
N_BRANCH = 4
N_ADA = 9
LANES = 128
NEG = -1e30
ATT_SCALE = HEAD_DIM ** -0.5
Q_SCALE = ATT_SCALE * math.log2(math.e)
VMEM_LIMIT = 56 * 1024 * 1024

C_KV = 0
C_MERGE = 1024
C_GLU = 5120
C_AB = 6144
C_AC = 6656
C_AX = 7168
C_NQ = 7680
C_DQ = 8192
C_DK = 8704
C_DV = 9216
C_GC = 9728
C_GS = 10240
C_GW = 10752
MIX_W = 11264


def _cparams(sem):
    return pltpu.CompilerParams(dimension_semantics=sem, vmem_limit_bytes=VMEM_LIMIT)


def _sigmoid(x):
    return 1.0 / (1.0 + jnp.exp(-x))


def _silu(x):
    return x * _sigmoid(x)


def _rows_to(m, tm):
    r = m.shape[0]
    if r == 1 or r == tm:
        return m
    return jnp.concatenate([m] * (tm // r), axis=0)


def _modulate(x, g, shift, scale):
    ms = jnp.mean(x * x, axis=-1, keepdims=True)
    y = x * lax.rsqrt(ms + NORM_EPS) * g
    return y * (1.0 + scale) + shift


def _dot(a, b):
    return jnp.dot(a, b, preferred_element_type=F32)


def _dot_nt(a, b, precision=None):
    return lax.dot_general(a, b, (((1,), (1,)), ((), ())), preferred_element_type=F32, precision=precision)


def _lane_lo(shape):
    return (lax.broadcasted_iota(jnp.int32, shape, len(shape) - 1) % LANES) < HEAD_DIM


def _dup_halves(x, lo):
    xr = pltpu.roll(x, HEAD_DIM, 1)
    return jnp.where(lo, x, xr), jnp.where(lo, xr, x)


def _ada_kernel(c_ref, w_ref, b_ref, o_ref):
    s = _silu(c_ref[...])
    o_ref[...] = _dot(s.astype(BF16), w_ref[...].astype(BF16)) + b_ref[...]


def ada_proj(c_all, ada_w, ada_b):
    r, d = c_all.shape
    nl = ada_w.shape[0]
    return pl.pallas_call(
        _ada_kernel,
        grid=(nl, N_ADA),
        in_specs=[
            pl.BlockSpec((r, d), lambda l, k: (0, 0)),
            pl.BlockSpec((None, d, d), lambda l, k: (l, 0, k)),
            pl.BlockSpec((None, None, 1, d), lambda l, k: (l, k, 0, 0)),
        ],
        out_specs=pl.BlockSpec((None, None, r, d), lambda l, k: (l, k, 0, 0)),
        out_shape=jax.ShapeDtypeStruct((nl, N_ADA, r, d), F32),
        compiler_params=_cparams(("arbitrary", "arbitrary")),
        name="ada_proj",
    )(c_all, ada_w, ada_b.reshape(nl, N_ADA, 1, d))


def _mod_spec(mod, k, tiles_per_group, ngrid):
    _, _, r, d = mod.shape
    if ngrid == 1:
        return pl.BlockSpec((None, None, r, d), lambda i: (k, i // tiles_per_group, 0, 0))
    return pl.BlockSpec((None, None, r, d), lambda i, j: (k, i // tiles_per_group, 0, 0))


def _ffn_kernel(x_ref, sh_ref, sc_ref, gt_ref, g_ref, wa_ref, wv_ref, wd_ref, o_ref, h_ref, acc_ref, *, nj):
    j = pl.program_id(1)
    tm = x_ref.shape[0]

    @pl.when(j == 0)
    def _():
        h = _modulate(x_ref[...], g_ref[...], _rows_to(sh_ref[...], tm), _rows_to(sc_ref[...], tm))
        h_ref[...] = h.astype(BF16)
        acc_ref[...] = jnp.zeros_like(acc_ref)

    h = h_ref[...]
    a = _dot(h, wa_ref[...])
    v = _dot(h, wv_ref[...])
    act = (_silu(a) * v).astype(BF16)
    acc_ref[...] += _dot(act, wd_ref[...])

    @pl.when(j == nj - 1)
    def _():
        o_ref[...] = x_ref[...] + 0.5 * _rows_to(gt_ref[...], tm) * acc_ref[...]


def ffn(x, mod, k0, g, w_up, w_down, tm, tiles_per_group):
    t, d = x.shape
    f = w_down.shape[0]
    nj = 2
    tf = f // nj
    return pl.pallas_call(
        functools.partial(_ffn_kernel, nj=nj),
        grid=(t // tm, nj),
        in_specs=[
            pl.BlockSpec((tm, d), lambda i, j: (i, 0)),
            _mod_spec(mod, k0, tiles_per_group, 2),
            _mod_spec(mod, k0 + 1, tiles_per_group, 2),
            _mod_spec(mod, k0 + 2, tiles_per_group, 2),
            pl.BlockSpec((1, d), lambda i, j: (0, 0)),
            pl.BlockSpec((d, tf), lambda i, j: (0, j)),
            pl.BlockSpec((d, tf), lambda i, j: (0, nj + j)),
            pl.BlockSpec((tf, d), lambda i, j: (j, 0)),
        ],
        out_specs=pl.BlockSpec((tm, d), lambda i, j: (i, 0)),
        out_shape=jax.ShapeDtypeStruct((t, d), F32),
        scratch_shapes=[pltpu.VMEM((tm, d), BF16), pltpu.VMEM((tm, d), F32)],
        compiler_params=_cparams(("arbitrary", "arbitrary")),
        name="ffn",
    )(x, mod, mod, mod, g, w_up, w_up, w_down)


def _mix_kernel(x_ref, sh_ref, sc_ref, g_ref, w_ref, o_ref, h_ref):
    tm = x_ref.shape[0]

    @pl.when(pl.program_id(1) == 0)
    def _():
        h = _modulate(x_ref[...], g_ref[...], _rows_to(sh_ref[...], tm), _rows_to(sc_ref[...], tm))
        h_ref[...] = h.astype(BF16)

    o_ref[...] = _dot(h_ref[...], w_ref[...])


def mix_in(x, mod, g, w_mix, tm, tiles_per_group, tn=1024):
    t, d = x.shape
    nw = w_mix.shape[1]
    return pl.pallas_call(
        _mix_kernel,
        grid=(t // tm, nw // tn),
        in_specs=[
            pl.BlockSpec((tm, d), lambda i, j: (i, 0)),
            _mod_spec(mod, 3, tiles_per_group, 2),
            _mod_spec(mod, 4, tiles_per_group, 2),
            pl.BlockSpec((1, d), lambda i, j: (0, 0)),
            pl.BlockSpec((d, tn), lambda i, j: (0, j)),
        ],
        out_specs=pl.BlockSpec((tm, tn), lambda i, j: (i, j)),
        out_shape=jax.ShapeDtypeStruct((t, nw), F32),
        scratch_shapes=[pltpu.VMEM((tm, d), BF16)],
        compiler_params=_cparams(("arbitrary", "arbitrary")),
        name="mix_in",
    )(x, mod, mod, g, w_mix)


def _group_norm(x, gain, gmat):
    x2 = x * x
    hi = x2.astype(BF16)
    lo = (x2 - hi.astype(F32)).astype(BF16)
    ss = _dot(hi, gmat) + _dot(lo, gmat)
    return x * lax.rsqrt(ss * (1.0 / HEAD_DIM) + NORM_EPS) * gain


def _rope(x, cos, sin_signed, lo):
    sw = jnp.where(lo, pltpu.roll(x, LANES - HEAD_DIM // 2, 1), pltpu.roll(x, HEAD_DIM // 2, 1))
    return x * cos + sw * sin_signed


def _prep_kernel(kv_ref, nq_ref, dq_ref, dk_ref, dv_ref, cos_ref, sin_ref, gain_ref, gmat_ref,
                 qn_ref, qr_ref, rows_ref, win_ref, dqo_ref, drows_ref, ksd_ref, vsd_ref, dkb_ref, dvb_ref):
    cos = cos_ref[...]
    sin = sin_ref[...]
    gmat = gmat_ref[...]
    lo32 = (lax.broadcasted_iota(jnp.int32, cos.shape, 1) % HEAD_DIM) < HEAD_DIM // 2
    lo64 = _lane_lo(cos.shape)

    def gain(r):
        return gain_ref[r:r + 1, :]

    def piece(ref, p):
        return ref[:, p * LANES:(p + 1) * LANES]

    for p in range(4):
        qn = _group_norm(piece(nq_ref, p), gain(0), gmat)
        qn_ref[:, p * LANES:(p + 1) * LANES] = qn
        qr_ref[:, p * LANES:(p + 1) * LANES] = (_rope(qn, cos, sin, lo32) * Q_SCALE).astype(BF16)
    rows_ref[:, 0:LANES] = _group_norm(piece(kv_ref, 0), gain(1), gmat)
    rows_ref[:, LANES:2 * LANES] = piece(kv_ref, 1)
    k_slc = _rope(_group_norm(piece(kv_ref, 2), gain(2), gmat), cos, sin, lo32)
    v_slc = piece(kv_ref, 3)
    rows_ref[:, 2 * LANES:3 * LANES] = k_slc
    rows_ref[:, 3 * LANES:4 * LANES] = v_slc
    for hk, (kd, vd) in enumerate(zip(_dup_halves(k_slc, lo64), _dup_halves(v_slc, lo64))):
        ksd_ref[:, hk * LANES:(hk + 1) * LANES] = kd.astype(BF16)
        vsd_ref[:, hk * LANES:(hk + 1) * LANES] = vd.astype(BF16)
    win_ref[:, 0:LANES] = _rope(_group_norm(piece(kv_ref, 4), gain(3), gmat), cos, sin, lo32)
    win_ref[:, LANES:2 * LANES] = piece(kv_ref, 5)
    for p in range(4):
        dq = _rope(_group_norm(piece(dq_ref, p), gain(4), gmat), cos, sin, lo32)
        dqo_ref[:, p * LANES:(p + 1) * LANES] = (dq * Q_SCALE).astype(BF16)
        dk = _rope(_group_norm(piece(dk_ref, p), gain(5), gmat), cos, sin, lo32)
        drows_ref[:, p, :] = dk
        dkb_ref[:, p * LANES:(p + 1) * LANES] = dk.astype(BF16)
        drows_ref[:, DIFF_HEADS + p, :] = piece(dv_ref, p)
    dvb_ref[...] = dv_ref[...].astype(BF16)


def qk_prep(z, cos, sin, gains, gmat, tm, tiles_per_seq, layer, n_layers, drows_prev):
    t = z.shape[0]

    def zs(width, col):
        return pl.BlockSpec((tm, width), lambda i: (i, col // width))

    def os_(width):
        return pl.BlockSpec((tm, width), lambda i: (i, 0))

    tab = pl.BlockSpec((tm, LANES), lambda i: (i % tiles_per_seq, 0))
    outs = ((512, F32), (512, BF16), (512, F32), (256, F32), (512, BF16), None,
            (256, BF16), (256, BF16), (512, BF16), (512, BF16))
    drows_shape = jax.ShapeDtypeStruct((n_layers, t, 2 * DIFF_HEADS, LANES), F32)
    drows_spec = pl.BlockSpec((None, tm, 2 * DIFF_HEADS, LANES), lambda i: (layer, i, 0, 0))
    in_specs = [zs(768, C_KV), zs(512, C_NQ), zs(512, C_DQ), zs(512, C_DK), zs(512, C_DV), tab, tab,
                pl.BlockSpec((8, LANES), lambda i: (0, 0)), pl.BlockSpec((LANES, LANES), lambda i: (0, 0))]
    args = [z, z, z, z, z, cos, sin, gains, gmat]
    kern, aliases = _prep_kernel, {}
    if drows_prev is not None:
        in_specs.append(pl.BlockSpec(memory_space=pl.ANY))
        args.append(drows_prev)
        aliases = {len(args) - 1: 5}
        kern = _prep_kernel_aliased
    return pl.pallas_call(
        kern,
        grid=(t // tm,),
        in_specs=in_specs,
        out_specs=[drows_spec if o is None else os_(o[0]) for o in outs],
        out_shape=[drows_shape if o is None else jax.ShapeDtypeStruct((t, o[0]), o[1]) for o in outs],
        input_output_aliases=aliases,
        compiler_params=_cparams(("arbitrary",)),
        name="qk_prep",
    )(*args)


def _prep_kernel_aliased(*refs):
    _prep_kernel(*refs[:9], *refs[10:])


def _conv_kernel(ab_ref, ac_ref, ax_ref, glu_ref, pa_ref, pd_ref, wa_ref, ba_ref, wd_ref, bd_ref, lg_ref, lb_ref,
                 ua_ref, ud_ref, sa_ref, sd_ref, bufa, bufd, *, tm, ha, hd, stride, nt):
    i = pl.program_id(1)

    @pl.when(i == 0)
    def _():
        bufa[0:ha, :] = pa_ref[...]
        bufd[0:hd, :] = pd_ref[...]

    if nt > 1:
        @pl.when(i > 0)
        def _():
            bufa[0:ha, :] = bufa[tm:tm + ha, :]
            bufd[0:hd, :] = bufd[tm:tm + hd, :]

    bufa[ha:ha + tm, :] = ac_ref[...] * ax_ref[...]
    ya = ba_ref[...]
    for k in range(A_CONV):
        ya = ya + wa_ref[k:k + 1, :] * bufa[pl.ds(ha - (A_CONV - 1 - k) * stride, tm), :]
    ua_ref[...] = (ab_ref[...] * ya).astype(ua_ref.dtype)

    glu = glu_ref[...]
    bufd[hd:hd + tm, :] = glu[:, :D_WIDTH] * _sigmoid(glu[:, D_WIDTH:])
    yd = bd_ref[...]
    for k in range(D_CONV):
        yd = yd + wd_ref[k:k + 1, :] * bufd[pl.ds(hd - (D_CONV - 1 - k) * stride, tm), :]
    mu = jnp.mean(yd, axis=-1, keepdims=True)
    var = jnp.mean(jnp.square(yd - mu), axis=-1, keepdims=True)
    y = (yd - mu) * lax.rsqrt(var + NORM_EPS) * lg_ref[...] + lb_ref[...]
    ud_ref[...] = _silu(y).astype(ud_ref.dtype)

    @pl.when(i == nt - 1)
    def _():
        sa_ref[...] = bufa[ha + tm - (A_CONV - 1) * stride:ha + tm, :]
        sd_ref[...] = bufd[hd + tm - (D_CONV - 1) * stride:hd + tm, :]


def conv_branches(z, past_a, past_d, wa, ba, wd, bd, lg, lb, nseq, tm, stride):
    t = z.shape[0]
    nt = t // nseq // tm
    ha, hd = past_a.shape[1], past_d.shape[1]
    if nt > 1:
        assert tm >= ha and tm >= hd

    def zs(width, col):
        return pl.BlockSpec((tm, width), lambda s, i: (s * nt + i, col // width))

    def cs(shape):
        return pl.BlockSpec(shape, lambda s, i: (0, 0))

    sa_rows, sd_rows = (A_CONV - 1) * stride, (D_CONV - 1) * stride
    return pl.pallas_call(
        functools.partial(_conv_kernel, tm=tm, ha=ha, hd=hd, stride=stride, nt=nt),
        grid=(nseq, nt),
        in_specs=[zs(512, C_AB), zs(512, C_AC), zs(512, C_AX), zs(1024, C_GLU),
                  pl.BlockSpec((None, ha, A_WIDTH), lambda s, i: (s, 0, 0)),
                  pl.BlockSpec((None, hd, D_WIDTH), lambda s, i: (s, 0, 0)),
                  cs((A_CONV, A_WIDTH)), cs((1, A_WIDTH)), cs((D_CONV, D_WIDTH)), cs((1, D_WIDTH)),
                  cs((1, D_WIDTH)), cs((1, D_WIDTH))],
        out_specs=[pl.BlockSpec((tm, A_WIDTH), lambda s, i: (s * nt + i, 0)),
                   pl.BlockSpec((tm, D_WIDTH), lambda s, i: (s * nt + i, 0)),
                   pl.BlockSpec((None, sa_rows, A_WIDTH), lambda s, i: (s, 0, 0)),
                   pl.BlockSpec((None, sd_rows, D_WIDTH), lambda s, i: (s, 0, 0))],
        out_shape=[jax.ShapeDtypeStruct((t, A_WIDTH), BF16), jax.ShapeDtypeStruct((t, D_WIDTH), BF16),
                   jax.ShapeDtypeStruct((nseq, sa_rows, A_WIDTH), F32),
                   jax.ShapeDtypeStruct((nseq, sd_rows, D_WIDTH), F32)],
        scratch_shapes=[pltpu.VMEM((ha + tm, A_WIDTH), F32), pltpu.VMEM((hd + tm, D_WIDTH), F32)],
        compiler_params=_cparams(("arbitrary", "arbitrary")),
        name="conv_branches",
    )(z, z, z, z, past_a, past_d, wa, ba, wd, bd, lg, lb)


def _merge_kernel(x_ref, gt_ref, ua_ref, uc_ref, ud_ref, oc_ref, os_ref, ow_ref, gc_ref, gs_ref, gw_ref,
                  m0_ref, m1_ref, m2_ref, m3_ref, wa_ref, wb_ref, wc_ref, wd_ref, wo_ref, o_ref):
    tm = x_ref.shape[0]
    ub = (_sigmoid(gc_ref[...]) * oc_ref[...] + _sigmoid(gs_ref[...]) * os_ref[...]
          + _sigmoid(gw_ref[...]) * ow_ref[...]).astype(BF16)
    merged = _sigmoid(m0_ref[...]) * _dot(ua_ref[...], wa_ref[...])
    merged += _sigmoid(m1_ref[...]) * _dot(ub, wb_ref[...])
    merged += _sigmoid(m2_ref[...]) * _dot(uc_ref[...], wc_ref[...])
    merged += _sigmoid(m3_ref[...]) * _dot(ud_ref[...], wd_ref[...])
    y = _dot(merged.astype(BF16), wo_ref[...])
    o_ref[...] = x_ref[...] + _rows_to(gt_ref[...], tm) * y


def merge_out(x, mod, z, ua, uc, ud, o_cmp, o_slc, o_win, w_a, w_b, w_c, w_d, w_o, tm, tiles_per_group):
    t, d = x.shape

    def ts(width):
        return pl.BlockSpec((tm, width), lambda i: (i, 0))

    def zs(width, col):
        return pl.BlockSpec((tm, width), lambda i: (i, col // width))

    def ws(shape):
        return pl.BlockSpec(shape, lambda i: (0, 0))

    return pl.pallas_call(
        _merge_kernel,
        grid=(t // tm,),
        in_specs=[ts(d), _mod_spec(mod, 5, tiles_per_group, 1), ts(512), ts(512), ts(512), ts(512), ts(512), ts(512),
                  zs(512, C_GC), zs(512, C_GS), zs(512, C_GW),
                  zs(1024, C_MERGE), zs(1024, C_MERGE + 1024), zs(1024, C_MERGE + 2048), zs(1024, C_MERGE + 3072),
                  ws((512, d)), ws((512, d)), ws((512, d)), ws((512, d)), ws((d, d))],
        out_specs=ts(d),
        out_shape=jax.ShapeDtypeStruct((t, d), F32),
        compiler_params=_cparams(("arbitrary",)),
        name="merge_out",
    )(x, mod, ua, uc, ud, o_cmp, o_slc, o_win, z, z, z, z, z, z, z, w_a, w_b, w_c, w_d, w_o)


def _compress_kernel(r_ref, w_ref, o_ref):
    x = r_ref[...]
    nb = x.shape[0] // NSA_BLOCK
    o_ref[...] = jnp.sum(x.reshape(nb, NSA_BLOCK, x.shape[1]) * w_ref[...][None], axis=1)


def nsa_compress(rows, w2, tm=512):
    t = rows.shape[0]
    return pl.pallas_call(
        _compress_kernel,
        grid=(t // tm,),
        in_specs=[pl.BlockSpec((tm, 256), lambda i: (i, 0)), pl.BlockSpec((NSA_BLOCK, 256), lambda i: (0, 0))],
        out_specs=pl.BlockSpec((tm // NSA_BLOCK, 256), lambda i: (i, 0)),
        out_shape=jax.ShapeDtypeStruct((t // NSA_BLOCK, 256), F32),
        compiler_params=_cparams(("arbitrary",)),
        name="nsa_compress",
    )(rows, w2)


def _masked_softmax(s, mask):
    s = jnp.where(mask, s, NEG)
    p = jnp.exp(s - jnp.max(s, axis=-1, keepdims=True))
    p = jnp.where(mask, p, 0.0)
    return p / jnp.maximum(jnp.sum(p, axis=-1, keepdims=True), 1e-30)


def _top_blocks(imp, n_sel):
    idx = lax.broadcasted_iota(jnp.int32, imp.shape, 1)
    big = imp.shape[1]
    sel = jnp.zeros(imp.shape, F32)
    for _ in range(n_sel):
        m = jnp.max(imp, axis=-1, keepdims=True)
        first = jnp.min(jnp.where(imp == m, idx, big), axis=-1, keepdims=True)
        hit = idx == first
        sel = jnp.where(hit, 1.0, sel)
        imp = jnp.where(hit, -3.0, imp)
    return sel


def _top_blocks_by_rank(imp, n_sel):
    nb = imp.shape[1]
    idx = lax.broadcasted_iota(jnp.int32, imp.shape, 1)
    rank = jnp.zeros(imp.shape, F32)
    for d in range(1, nb):
        other = pltpu.roll(imp, d, 1)
        beats = (other > imp) | ((other == imp) & (idx >= d))
        rank = rank + jnp.where(beats, 1.0, 0.0)
    return jnp.where(rank < n_sel, 1.0, 0.0)


def _cmp_select(q, kdup, vdup, qpos, nb, n_sel):
    tq = q.shape[0]
    qlo = _lane_lo((tq, LANES))
    blk = lax.broadcasted_iota(jnp.int32, (1, nb), 1)
    complete = (blk + 1) * NSA_BLOCK - 1 <= qpos
    cur = qpos // NSA_BLOCK
    outs, sels = [], []
    for hk in range(NSA_KV_HEADS):
        imp = jnp.zeros((tq, nb), F32)
        vd = vdup[hk].astype(BF16)
        for a in (2 * hk, 2 * hk + 1):
            qp = q[:, a * LANES:(a + 1) * LANES]
            pair = []
            for par in range(2):
                qm = jnp.where(qlo if par == 0 else ~qlo, qp, 0.0)
                s = _dot_nt(qm, kdup[hk], precision=lax.Precision.HIGHEST)
                pc = _masked_softmax(s * ATT_SCALE, complete)
                imp = imp + pc
                pair.append(_dot(pc.astype(BF16), vd))
            outs.append(jnp.where(qlo, pair[0], pair[1]))
        forced = (blk == cur) | (blk == 0)
        imp = jnp.where(forced, NSA_GROUP + 1.0, imp)
        imp = jnp.where(blk <= cur, imp, -1.0)
        sels.append(_top_blocks(imp, n_sel))
    return jnp.concatenate(outs, axis=1), sels


def _cmpsel_kernel(q_ref, c_ref, o_ref, sel_ref, *, tq, n_sel):
    qpos = pl.program_id(1) * tq + lax.broadcasted_iota(jnp.int32, (tq, 1), 0)
    nb = c_ref.shape[0]
    lo = _lane_lo((nb, LANES))
    kdup = _dup_halves(c_ref[:, 0:LANES], lo)
    vdup = _dup_halves(c_ref[:, LANES:2 * LANES], lo)
    o, sels = _cmp_select(q_ref[...], kdup, vdup, qpos, nb, n_sel)
    o_ref[...] = o
    sel_ref[...] = jnp.concatenate(sels, axis=1).astype(sel_ref.dtype)


def nsa_cmp_select(q_n, cmp, nseq, tq=256):
    t = q_n.shape[0]
    nq = t // nseq // tq
    nb = cmp.shape[0] // nseq
    return pl.pallas_call(
        functools.partial(_cmpsel_kernel, tq=tq, n_sel=min(NSA_TOPK, nb)),
        grid=(nseq, nq),
        in_specs=[pl.BlockSpec((tq, 512), lambda b, i: (b * nq + i, 0)),
                  pl.BlockSpec((nb, 256), lambda b, i: (b, 0))],
        out_specs=[pl.BlockSpec((tq, 512), lambda b, i: (b * nq + i, 0)),
                   pl.BlockSpec((tq, 2 * nb), lambda b, i: (b * nq + i, 0))],
        out_shape=[jax.ShapeDtypeStruct((t, 512), F32), jax.ShapeDtypeStruct((t, 2 * nb), BF16)],
        compiler_params=_cparams(("arbitrary", "arbitrary")),
        name="nsa_cmp_select",
    )(q_n, cmp)


def _online_step(s, m, acc, v_ext):
    tk = s.shape[1]
    m_new = jnp.maximum(m, jnp.max(s, axis=-1, keepdims=True))
    p = jnp.exp2(s - jnp.concatenate([m_new] * (tk // LANES), axis=1))
    alpha = jnp.exp2(m - m_new)
    if acc.shape[1] != LANES:
        alpha = jnp.concatenate([alpha] * (acc.shape[1] // LANES), axis=1)
    return m_new, alpha * acc + _dot(p.astype(BF16), v_ext)


def _slc_kernel(q_ref, k_ref, v_ref, sel_ref, e_ref, o_ref, *, tq, tk, nb):
    qi = pl.program_id(1)
    n_full = (qi * tq) // tk
    rows = NSA_GROUP * tq
    qlo = _lane_lo((tq, LANES))
    klo = _lane_lo((tk, LANES))
    zero = jnp.zeros((tq, LANES), BF16)
    qpos = qi * tq + lax.broadcasted_iota(jnp.int32, (tq, 1), 0)
    col = lax.broadcasted_iota(jnp.int32, (1, tk), 1)
    for hk in range(NSA_KV_HEADS):
        parts = []
        for c in range(NSA_GROUP):
            h = hk * NSA_GROUP + c
            qp = q_ref[:, (h // 2) * LANES:(h // 2 + 1) * LANES]
            parts.append(jnp.where(qlo, qp, zero) if h % 2 == 0 else jnp.where(qlo, zero, qp))
        qst = jnp.concatenate(parts, axis=0)
        sel = sel_ref[:, hk * nb:(hk + 1) * nb]

        def step(j, carry, diag, qst=qst, sel=sel, hk=hk):
            m, acc = carry
            off = pl.multiple_of(j * tk, tk)
            k = k_ref[pl.ds(off, tk), hk * LANES:(hk + 1) * LANES]
            v = v_ref[pl.ds(off, tk), hk * LANES:(hk + 1) * LANES]
            vx = jnp.where(klo, v, jnp.ones_like(v))
            bias = (_dot(sel, e_ref[j]) - 1.0) * (-NEG)
            if diag:
                bias = jnp.where(j * tk + col <= qpos, bias, NEG)
            s = _dot_nt(qst, k) + jnp.concatenate([bias] * NSA_GROUP, axis=0)
            return _online_step(s, m, acc, vx)

        init = (jnp.full((rows, LANES), NEG, F32), jnp.zeros((rows, LANES), F32))
        carry = lax.fori_loop(0, n_full, lambda j, cr, step=step: step(j, cr, False), init)
        _, acc = step(n_full, carry, True)
        o = acc / jnp.maximum(pltpu.roll(acc, HEAD_DIM, 1), 1e-30)
        for pair in range(NSA_GROUP // 2):
            a = hk * (NSA_GROUP // 2) + pair
            o_even = o[(2 * pair) * tq:(2 * pair + 1) * tq]
            o_odd = o[(2 * pair + 1) * tq:(2 * pair + 2) * tq]
            o_ref[:, a * LANES:(a + 1) * LANES] = jnp.where(qlo, o_even, pltpu.roll(o_odd, HEAD_DIM, 1))


def nsa_slc(q_rot, ksd, vsd, sel, emat, nseq, tq=256):
    t = q_rot.shape[0]
    s = t // nseq
    nq = s // tq
    nk, nb, tk = emat.shape
    return pl.pallas_call(
        functools.partial(_slc_kernel, tq=tq, tk=tk, nb=nb),
        grid=(nseq, nq),
        in_specs=[pl.BlockSpec((tq, 512), lambda b, i: (b * nq + i, 0)),
                  pl.BlockSpec((s, 256), lambda b, i: (b, 0)),
                  pl.BlockSpec((s, 256), lambda b, i: (b, 0)),
                  pl.BlockSpec((tq, 2 * nb), lambda b, i: (b * nq + i, 0)),
                  pl.BlockSpec((nk, nb, tk), lambda b, i: (0, 0, 0))],
        out_specs=pl.BlockSpec((tq, 512), lambda b, i: (b * nq + i, 0)),
        out_shape=jax.ShapeDtypeStruct((t, 512), F32),
        compiler_params=_cparams(("arbitrary", "arbitrary")),
        name="nsa_slc",
    )(q_rot, ksd, vsd, sel, emat)


def _win_kernel(q_ref, wp_ref, wc_ref, o_ref, *, tq):
    i = pl.program_id(1)
    lo = _lane_lo((tq, LANES))
    qpos = i * tq + lax.broadcasted_iota(jnp.int32, (tq, 1), 0)
    chunks = []
    for off, ref in ((-tq, wp_ref), (0, wc_ref)):
        kpos = i * tq + off + lax.broadcasted_iota(jnp.int32, (1, tq), 1)
        mask = (kpos >= 0) & (kpos <= qpos) & (qpos - kpos <= NSA_WINDOW)
        chunks.append((mask, _dup_halves(ref[:, 0:LANES], lo), _dup_halves(ref[:, LANES:2 * LANES], lo)))
    zero = jnp.zeros((tq, LANES), BF16)
    for a in range(NSA_HEADS // 2):
        pair = []
        for par in range(2):
            hk = (2 * a + par) // NSA_GROUP
            qp = q_ref[:, a * LANES:(a + 1) * LANES]
            qm = jnp.where(lo, qp, zero) if par == 0 else jnp.where(lo, zero, qp)
            ss = [jnp.where(mask, _dot_nt(qm, kd[hk].astype(BF16)), NEG) for mask, kd, _ in chunks]
            m = jnp.maximum(jnp.max(ss[0], axis=-1, keepdims=True), jnp.max(ss[1], axis=-1, keepdims=True))
            ps = [jnp.where(ch[0], jnp.exp2(s - m), 0.0) for s, ch in zip(ss, chunks)]
            l = jnp.sum(ps[0], axis=-1, keepdims=True) + jnp.sum(ps[1], axis=-1, keepdims=True)
            inv = 1.0 / jnp.maximum(l, 1e-30)
            o = _dot((ps[0] * inv).astype(BF16), chunks[0][2][hk].astype(BF16))
            o += _dot((ps[1] * inv).astype(BF16), chunks[1][2][hk].astype(BF16))
            pair.append(o)
        o_ref[:, a * LANES:(a + 1) * LANES] = jnp.where(lo, pair[0], pair[1])


def nsa_win(q_rot, win_rows, nseq, tq=NSA_WINDOW):
    t = q_rot.shape[0]
    nq = t // nseq // tq
    return pl.pallas_call(
        functools.partial(_win_kernel, tq=tq),
        grid=(nseq, nq),
        in_specs=[pl.BlockSpec((tq, 512), lambda b, i: (b * nq + i, 0)),
                  pl.BlockSpec((tq, 256), lambda b, i: (b * nq + jnp.maximum(i - 1, 0), 0)),
                  pl.BlockSpec((tq, 256), lambda b, i: (b * nq + i, 0))],
        out_specs=pl.BlockSpec((tq, 512), lambda b, i: (b * nq + i, 0)),
        out_shape=jax.ShapeDtypeStruct((t, 512), F32),
        compiler_params=_cparams(("arbitrary", "arbitrary")),
        name="nsa_win",
    )(q_rot, win_rows, win_rows)


def _diff_lambda(lam_ref, lam_init):
    lp = lam_ref[...]
    a = jnp.sum(lp[0:1] * lp[1:2], axis=-1, keepdims=True)
    b = jnp.sum(lp[2:3] * lp[3:4], axis=-1, keepdims=True)
    return jnp.exp(a) - jnp.exp(b) + lam_init


def _diff_finish(o1, o2, lam, g, lam_init):
    o = o1 - lam * o2
    ms = jnp.mean(o * o, axis=-1, keepdims=True)
    return o * lax.rsqrt(ms + NORM_EPS) * g * (1.0 - lam_init)


def _diff_kernel(q_ref, k_ref, v_ref, lam_ref, g_ref, o_ref, *, tq, tk, lam_init):
    qi = pl.program_id(2)
    n_full = (qi * tq) // tk
    lo = _lane_lo((tq, LANES))
    q = q_ref[...]
    zero = jnp.zeros_like(q)
    qst = jnp.concatenate([jnp.where(lo, q, zero), jnp.where(lo, zero, q)], axis=0)
    ones = jnp.ones((tk, LANES), BF16)
    qpos = qi * tq + lax.broadcasted_iota(jnp.int32, (tq, 1), 0)
    col = lax.broadcasted_iota(jnp.int32, (1, tk), 1)

    def scores(j):
        return _dot_nt(qst, k_ref[pl.ds(pl.multiple_of(j * tk, tk), tk), :])

    def update(j, s, m, acc):
        v = v_ref[pl.ds(pl.multiple_of(j * tk, tk), tk), :]
        return _online_step(s, m, acc, jnp.concatenate([v, ones], axis=1))

    def body(j, carry):
        m, acc, s = carry
        s_next = scores(j + 1)
        m, acc = update(j, s, m, acc)
        return m, acc, s_next

    init = (jnp.full((2 * tq, LANES), NEG, F32), jnp.zeros((2 * tq, 2 * LANES), F32), scores(0))
    m, acc, s = lax.fori_loop(0, n_full, body, init)
    bias = jnp.where(n_full * tk + col <= qpos, 0.0, NEG)
    _, acc = update(n_full, s + jnp.concatenate([bias, bias], axis=0), m, acc)
    o = acc[:, 0:LANES] / jnp.maximum(acc[:, LANES:2 * LANES], 1e-30)
    lam = _diff_lambda(lam_ref, lam_init)
    o_ref[...] = _diff_finish(o[0:tq], o[tq:2 * tq], lam, g_ref[...], lam_init).astype(o_ref.dtype)


def diff_attn(dq, dk, dv, lam_p, out_g, nseq, lam_init, tq=256, tk=1024):
    t = dq.shape[0]
    s = t // nseq
    nq = s // tq
    return pl.pallas_call(
        functools.partial(_diff_kernel, tq=tq, tk=tk, lam_init=lam_init),
        grid=(nseq, DIFF_HEADS, nq),
        in_specs=[pl.BlockSpec((tq, LANES), lambda b, h, i: (b * nq + i, h)),
                  pl.BlockSpec((s, LANES), lambda b, h, i: (b, h)),
                  pl.BlockSpec((s, LANES), lambda b, h, i: (b, h)),
                  pl.BlockSpec((4, HEAD_DIM), lambda b, h, i: (0, 0)),
                  pl.BlockSpec((1, LANES), lambda b, h, i: (0, 0))],
        out_specs=pl.BlockSpec((tq, LANES), lambda b, h, i: (b * nq + i, h)),
        out_shape=jax.ShapeDtypeStruct((t, 512), BF16),
        compiler_params=_cparams(("arbitrary", "arbitrary", "arbitrary")),
        name="diff_attn",
    )(dq, dk, dv, lam_p, out_g)


QPAD = 8


def _pad_keys(x):
    return jnp.concatenate([x, jnp.zeros((LANES - x.shape[0], x.shape[1]), x.dtype)], axis=0)


def _nsa_sample_kernel(pt_ref, *refs, n_pages, past_len, n_new, n_sel):
    chunks = refs[:n_pages + 1]
    qn_ref, qr_ref, wst_ref, wnew_ref, wt_ref, selm_ref = refs[n_pages + 1:n_pages + 7]
    oc_ref, os_ref, ow_ref, wout_ref = refs[-4:]
    n_real = (past_len + n_new + NSA_BLOCK - 1) // NSA_BLOCK
    nbp = selm_ref.shape[1]
    page = chunks[0].shape[3]

    def twice(x):
        return jnp.concatenate([x, x], axis=0)

    wt = wt_ref[...]
    y = jnp.concatenate([ch[0:2].reshape(4 * HEAD_DIM, page) * wt for ch in chunks], axis=1)
    hi = y.astype(BF16)
    selm = selm_ref[...]
    cmp_t = _dot(hi, selm)
    yk = y[0:2 * HEAD_DIM]
    cmp_k = cmp_t[0:2 * HEAD_DIM] + _dot((yk - hi[0:2 * HEAD_DIM].astype(F32)).astype(BF16), selm)

    tok = lax.broadcasted_iota(jnp.int32, (QPAD, 1), 0)
    qpos = past_len + tok
    rows = NSA_GROUP * QPAD
    qpos_st = jnp.concatenate([qpos] * NSA_GROUP, axis=0)
    qlo = _lane_lo((QPAD, LANES))

    def stack_group(q_ref, hk, zero):
        parts = []
        for c in range(NSA_GROUP):
            h = hk * NSA_GROUP + c
            qp = q_ref[:, (h // 2) * LANES:(h // 2 + 1) * LANES]
            parts.append(jnp.where(qlo, qp, zero) if h % 2 == 0 else jnp.where(qlo, zero, qp))
        return jnp.concatenate(parts, axis=0)

    def emit(o_ref, per_hk):
        for a in range(NSA_HEADS // 2):
            hk, c = (2 * a) // NSA_GROUP, (2 * a) % NSA_GROUP
            o = per_hk[hk]
            o_ref[:, a * LANES:(a + 1) * LANES] = jnp.where(qlo, o[c * QPAD:(c + 1) * QPAD], o[(c + 1) * QPAD:(c + 2) * QPAD])

    blk = lax.broadcasted_iota(jnp.int32, (1, nbp), 1)
    complete = ((blk + 1) * NSA_BLOCK - 1 <= qpos_st) & (blk < n_real)
    cur = qpos // NSA_BLOCK
    per_hk, imps = [], []
    for hk in range(NSA_KV_HEADS):
        kd = twice(cmp_k[hk * HEAD_DIM:(hk + 1) * HEAD_DIM])
        vd = twice(cmp_t[(2 + hk) * HEAD_DIM:(3 + hk) * HEAD_DIM]).astype(BF16)
        s = jnp.dot(stack_group(qn_ref, hk, jnp.zeros((QPAD, LANES), F32)), kd,
                    precision=lax.Precision.HIGHEST, preferred_element_type=F32)
        pc = _masked_softmax(s * ATT_SCALE, complete)
        per_hk.append(_dot_nt(pc.astype(BF16), vd))
        imp = functools.reduce(lambda a, b: a + b, [pc[c * QPAD:(c + 1) * QPAD] for c in range(NSA_GROUP)])
        imp = jnp.where((blk == cur) | (blk == 0), NSA_GROUP + 1.0, imp)
        imp = jnp.where(blk <= cur, imp, -1.0)
        imps.append(jnp.where(blk < n_real, imp, -2.0))
    emit(oc_ref, per_hk)
    sel_all = _top_blocks_by_rank(jnp.concatenate(imps, axis=0), n_sel)
    sels = [sel_all[hk * QPAD:(hk + 1) * QPAD] for hk in range(NSA_KV_HEADS)]

    qst = [stack_group(qr_ref, hk, jnp.zeros((QPAD, LANES), BF16)) for hk in range(NSA_KV_HEADS)]
    key_new = lax.broadcasted_iota(jnp.int32, (1, LANES), 1)
    new_valid = (key_new < n_new) & (past_len + key_new <= qpos_st)

    def attend(kvs, biases, hk):
        ss = [_dot(qst[hk], twice(k).astype(BF16)) + b for (k, _), b in zip(kvs, biases)]
        m = functools.reduce(jnp.maximum, [jnp.max(s, axis=-1, keepdims=True) for s in ss])
        ps = [jnp.exp2(s - m) for s in ss]
        l = functools.reduce(lambda a, b: a + b, [jnp.sum(p, axis=-1, keepdims=True) for p in ps])
        inv = 1.0 / jnp.maximum(l, 1e-30)
        o = jnp.zeros((rows, LANES), F32)
        for p, (_, v) in zip(ps, kvs):
            o += _dot_nt((p * inv).astype(BF16), twice(v).astype(BF16))
        return o

    per_hk = []
    for hk in range(NSA_KV_HEADS):
        selx = _dot_nt(sels[hk].astype(BF16), selm)
        selx = jnp.concatenate([selx] * NSA_GROUP, axis=0)
        kvs, biases = [], []
        for j, ch in enumerate(chunks):
            kvs.append((ch[2, hk], ch[3, hk]))
            bias = (selx[:, j * page:(j + 1) * page] - 1.0) * (-NEG)
            biases.append(jnp.where(new_valid, bias, NEG) if j == n_pages else bias)
        per_hk.append(attend(kvs, biases, hk))
    emit(os_ref, per_hk)

    wbuf = wst_ref.shape[3]
    kpos = past_len - wbuf + lax.broadcasted_iota(jnp.int32, (1, wbuf), 1)
    bias_st = jnp.where((kpos >= 0) & (kpos <= qpos_st) & (qpos_st - kpos <= NSA_WINDOW), 0.0, NEG)
    bias_new = jnp.where(new_valid & (qpos_st - (past_len + key_new) <= NSA_WINDOW), 0.0, NEG)
    per_hk = []
    for hk in range(NSA_KV_HEADS):
        kvs = [(wst_ref[0, hk], wst_ref[1, hk]), (wnew_ref[0, hk], wnew_ref[1, hk])]
        per_hk.append(attend(kvs, [bias_st, bias_new], hk))
    emit(ow_ref, per_hk)

    lane = lax.broadcasted_iota(jnp.int32, (HEAD_DIM, wbuf), 1)
    for r in range(2):
        for hk in range(NSA_KV_HEADS):
            shifted = pltpu.roll(wst_ref[r, hk], wbuf - n_new, 1)
            tail = jnp.concatenate([jnp.zeros((HEAD_DIM, wbuf - LANES), F32),
                                    pltpu.roll(wnew_ref[r, hk], LANES - n_new, 1)], axis=1)
            wout_ref[r, hk] = jnp.where(lane >= wbuf - n_new, tail, shifted)


def nsa_sample(layer, page_table, cache_t, new_t, q_n, q_rot, win_t, win_new_t, wt, selm, n_new, wout_prev):
    nb_, n_pages = page_table.shape
    page = cache_t.shape[5]
    past_len = n_pages * page
    n_real = (past_len + n_new + NSA_BLOCK - 1) // NSA_BLOCK
    wbuf = win_t.shape[5]
    kv = (NSA_KV_HEADS, HEAD_DIM)

    def page_spec(j):
        return pl.BlockSpec((None, None, 4) + kv + (page,), lambda b, pt: (layer, pt[b, j], 0, 0, 0, 0))

    def seq(shape):
        return pl.BlockSpec((None,) + shape, lambda b, pt: (b,) + (0,) * len(shape))

    def const(shape):
        return pl.BlockSpec(shape, lambda b, pt: (0,) * len(shape))

    win_spec = pl.BlockSpec((None, None, 2) + kv + (wbuf,), lambda b, pt: (layer, b, 0, 0, 0, 0))
    in_specs = [page_spec(j) for j in range(n_pages)] + [
        seq((4,) + kv + (page,)), seq((QPAD, 512)), seq((QPAD, 512)), win_spec,
        seq((2,) + kv + (LANES,)), const(wt.shape), const(selm.shape)]
    args = [page_table] + [cache_t] * n_pages + [new_t, q_n, q_rot, win_t, win_new_t, wt, selm]
    aliases = {}
    if wout_prev is not None:
        in_specs.append(pl.BlockSpec(memory_space=pl.ANY))
        args.append(wout_prev)
        aliases = {len(args) - 1: 3}
    grid_spec = pltpu.PrefetchScalarGridSpec(
        num_scalar_prefetch=1,
        grid=(nb_,),
        in_specs=in_specs,
        out_specs=[seq((QPAD, 512)), seq((QPAD, 512)), seq((QPAD, 512)), win_spec],
    )
    return pl.pallas_call(
        functools.partial(_nsa_sample_kernel, n_pages=n_pages, past_len=past_len, n_new=n_new,
                          n_sel=min(NSA_TOPK, n_real)),
        grid_spec=grid_spec,
        out_shape=[jax.ShapeDtypeStruct((nb_, QPAD, 512), F32)] * 3 + [jax.ShapeDtypeStruct(win_t.shape, F32)],
        input_output_aliases=aliases,
        compiler_params=_cparams(("arbitrary",)),
        name="nsa_sample",
    )(*args)


def _diff_sample_kernel(pt_ref, *refs, n_pages, past_len, n_new, lam_init):
    pages = refs[:n_pages]
    new_ref, q_ref, lam_ref, g_ref, o_ref = refs[n_pages:]
    lo = _lane_lo((QPAD, LANES))
    tok = lax.broadcasted_iota(jnp.int32, (2 * QPAD, 1), 0) % QPAD
    key_new = lax.broadcasted_iota(jnp.int32, (1, LANES), 1)
    new_valid = (key_new < n_new) & (key_new <= tok)
    lam = _diff_lambda(lam_ref, lam_init)
    q = q_ref[...]
    new = new_ref[...]
    zero = jnp.zeros((QPAD, LANES), BF16)
    page = pages[0].shape[0] // (2 * DIFF_HEADS)
    for h in range(DIFF_HEADS):
        qp = q[:, h * LANES:(h + 1) * LANES]
        qst = jnp.concatenate([jnp.where(lo, qp, zero), jnp.where(lo, zero, qp)], axis=0)
        ks = [pages[j][pl.ds(h, page, stride=2 * DIFF_HEADS), :] for j in range(n_pages)]
        ks.append(_pad_keys(new[:, h * LANES:(h + 1) * LANES]))
        vs = [pages[j][pl.ds(DIFF_HEADS + h, page, stride=2 * DIFF_HEADS), :] for j in range(n_pages)]
        vs.append(_pad_keys(new[:, (DIFF_HEADS + h) * LANES:(DIFF_HEADS + h + 1) * LANES]))
        ss = [_dot_nt(qst, k.astype(BF16)) for k in ks]
        ss[-1] = jnp.where(new_valid, ss[-1], NEG)
        m = functools.reduce(jnp.maximum, [jnp.max(s, axis=-1, keepdims=True) for s in ss])
        ps = [jnp.exp2(s - m) for s in ss]
        ps[-1] = jnp.where(new_valid, ps[-1], 0.0)
        l = functools.reduce(lambda a, b: a + b, [jnp.sum(p, axis=-1, keepdims=True) for p in ps])
        inv = 1.0 / jnp.maximum(l, 1e-30)
        o = jnp.zeros((QPAD, LANES), F32)
        for p, v in zip(ps, vs):
            pn = p * inv
            a = (pn[0:QPAD] - lam * pn[QPAD:2 * QPAD]).astype(BF16)
            o += _dot(a, v.astype(BF16))
        ms = jnp.mean(o * o, axis=-1, keepdims=True)
        o_ref[:, h * LANES:(h + 1) * LANES] = (o * lax.rsqrt(ms + NORM_EPS) * g_ref[...] * (1.0 - lam_init)).astype(o_ref.dtype)


def diff_sample(layer, page_table, cache, new_rows, dq, lam_p, out_g, n_new, lam_init):
    nb_, n_pages = page_table.shape
    page = cache.shape[2] // (2 * DIFF_HEADS)

    def page_spec(j):
        return pl.BlockSpec((None, None, page * 2 * DIFF_HEADS, LANES), lambda b, pt: (layer, pt[b, j], 0, 0))

    grid_spec = pltpu.PrefetchScalarGridSpec(
        num_scalar_prefetch=1,
        grid=(nb_,),
        in_specs=[page_spec(j) for j in range(n_pages)] + [
            pl.BlockSpec((None, QPAD, 1024), lambda b, pt: (b, 0, 0)),
            pl.BlockSpec((None, QPAD, 512), lambda b, pt: (b, 0, 0)),
            pl.BlockSpec((4, HEAD_DIM), lambda b, pt: (0, 0)),
            pl.BlockSpec((1, LANES), lambda b, pt: (0, 0))],
        out_specs=pl.BlockSpec((None, QPAD, 512), lambda b, pt: (b, 0, 0)),
    )
    return pl.pallas_call(
        functools.partial(_diff_sample_kernel, n_pages=n_pages, past_len=n_pages * page, n_new=n_new, lam_init=lam_init),
        grid_spec=grid_spec,
        out_shape=jax.ShapeDtypeStruct((nb_, QPAD, 512), F32),
        compiler_params=_cparams(("arbitrary",)),
        name="diff_sample",
    )(page_table, *([cache] * n_pages), new_rows, dq, lam_p, out_g)


def _mix_weight(w):
    d = w.shape[0]
    splits = (A_WIDTH, A_WIDTH, A_WIDTH, 512, 768, 3 * NSA_HEADS, 512, 512, 512, 2 * D_WIDTH, N_BRANCH * d)
    offs = [0]
    for s in splits:
        offs.append(offs[-1] + s)
    part = {n: w[:, offs[i]:offs[i + 1]] for i, n in enumerate(
        ("ab", "ac", "ax", "nq", "kv", "g", "dq", "dk", "dv", "glu", "merge"))}
    g = part["g"].reshape(d, NSA_HEADS, 3)
    gates = [jnp.repeat(g[:, :, k], HEAD_DIM, axis=1) for k in range(3)]
    pad = jnp.zeros((d, C_MERGE - 768), w.dtype)
    cols = [part["kv"], pad, part["merge"], part["glu"], part["ab"], part["ac"], part["ax"], part["nq"],
            part["dq"], part["dk"], part["dv"]] + gates
    out = jnp.concatenate(cols, axis=1).astype(BF16)
    assert out.shape[1] == MIX_W
    return out


def _rope_tables(pos):
    half = HEAD_DIM // 2
    inv = ROPE_THETA ** (-jnp.arange(half, dtype=F32) / half)
    ang = pos.astype(F32)[:, None] * inv[None, :]
    cos, sin = jnp.cos(ang), jnp.sin(ang)
    return jnp.tile(jnp.concatenate([cos, cos], axis=1), (1, 2)), jnp.tile(jnp.concatenate([-sin, sin], axis=1), (1, 2))


def _tile_gain(g):
    return jnp.tile(g.astype(F32), LANES // HEAD_DIM)


def _expansion(n_tiles, nb_rows, tk):
    j = jnp.arange(n_tiles)[:, None, None]
    n = jnp.arange(nb_rows)[None, :, None]
    k = jnp.arange(tk)[None, None, :]
    return ((j * tk + k) // NSA_BLOCK == n).astype(BF16)


def _tokens_last(x, nb_, n_new, lead):
    y = jnp.moveaxis(x.reshape((n_new, nb_) + lead), 0, -1)
    return jnp.pad(y, [(0, 0)] * (y.ndim - 1) + [(0, LANES - n_new)])


def _to_seq_major(x, nb_, n_new, pad_to=None):
    y = jnp.swapaxes(x.reshape(n_new, nb_, x.shape[-1]), 0, 1)
    if pad_to is not None and pad_to > n_new:
        y = jnp.pad(y, ((0, 0), (0, pad_to - n_new), (0, 0)))
    return y


def _to_step_major(y, n_new):
    return jnp.swapaxes(y[:, :n_new], 0, 1).reshape(-1, y.shape[-1])


def kernel(x_prompt, x_sample, cache_nsa, cache_diff, state_nsa_win, state_conv_a, state_conv_d, page_table,
           c_prompt, c_sample, ada_w, ada_b, norm_g, ffn1_up, ffn1_down, ffn2_up, ffn2_down, w_mix_in,
           conv_a_w, conv_a_b, a_out, nsa_qk_g, nsa_cmp_w, nsa_out, diff_qk_g, diff_lambda, diff_out_g, diff_out,
           conv_d_w, conv_d_b, d_norm_g, d_norm_b, d_out, w_out):
    bp, sp, d = x_prompt.shape
    bs, ns, _ = x_sample.shape
    depth = ada_w.shape[0]
    n_pages = page_table.shape[1]
    page = cache_nsa.shape[2]
    past_len = n_pages * page
    wbuf = state_nsa_win.shape[2]

    rows = bp + bs
    rpad = -(-rows // 8) * 8
    c_all = jnp.pad(jnp.concatenate([c_prompt, c_sample], axis=0), ((0, rpad - rows), (0, 0)))
    ada = ada_proj(c_all, ada_w, ada_b)

    xp = x_prompt.reshape(bp * sp, d)
    xs = jnp.swapaxes(x_sample, 0, 1).reshape(ns * bs, d)
    tm_p, tm_s = 512, ns * bs
    tpg_p = sp // tm_p

    cos_p, sin_p = _rope_tables(jnp.arange(sp))
    cos_s, sin_s = _rope_tables(jnp.repeat(past_len + jnp.arange(ns), bs))
    gmat = (jnp.arange(LANES)[:, None] // HEAD_DIM == jnp.arange(LANES)[None, :] // HEAD_DIM).astype(BF16)
    tk_slc = 512
    emat_p = _expansion(sp // tk_slc, sp // NSA_BLOCK, tk_slc)
    n_real_s = (past_len + ns + NSA_BLOCK - 1) // NSA_BLOCK
    nbp_s = -(-n_real_s // LANES) * LANES
    selm_s = _expansion(1, nbp_s, past_len + page)[0].T
    cache_nsa_t = jnp.transpose(cache_nsa, (0, 1, 3, 4, 5, 2))
    win_state_t = jnp.transpose(state_nsa_win, (0, 1, 3, 4, 5, 2))
    cache_diff2 = cache_diff.reshape(depth, cache_diff.shape[1], page * 2 * DIFF_HEADS, 2 * HEAD_DIM)

    zeros_a = jnp.zeros((bp, 8, A_WIDTH), F32)
    zeros_d = jnp.zeros((bp, 32, D_WIDTH), F32)

    outs = {k: [] for k in ("nsa_p", "nsa_s", "win_p", "ca_p", "ca_s", "cd_p", "cd_s")}
    drows_p = drows_s = win_out_t = None
    for l in range(depth):
        lam_init = 0.8 - 0.6 * math.exp(-0.3 * l)
        mod_p = ada[l, :, 0:bp].reshape(N_ADA, bp, 1, d)
        mod_s = ada[l, :, bp:bp + bs].reshape(N_ADA, 1, bs, d)
        g = norm_g[l]
        w1u, w1d = ffn1_up[l].astype(BF16), ffn1_down[l].astype(BF16)
        w2u, w2d = ffn2_up[l].astype(BF16), ffn2_down[l].astype(BF16)
        wmix = _mix_weight(w_mix_in[l])
        gains = jnp.stack([_tile_gain(nsa_qk_g[l, 0]), _tile_gain(nsa_qk_g[l, 1]), _tile_gain(nsa_qk_g[l, 2]),
                           _tile_gain(nsa_qk_g[l, 3]), _tile_gain(diff_qk_g[l, 0]), _tile_gain(diff_qk_g[l, 1]),
                           jnp.zeros((LANES,), F32), jnp.zeros((LANES,), F32)])
        cw = nsa_cmp_w[l]
        w2 = jnp.concatenate([cw[0], cw[0], cw[1], cw[1]], axis=1)
        wt = jnp.tile(jnp.swapaxes(cw, 1, 2), (1, 1, page // NSA_BLOCK))
        wt = jnp.broadcast_to(wt[:, None], (2, NSA_KV_HEADS, HEAD_DIM, page)).reshape(4 * HEAD_DIM, page)
        wts = [a_out[l].astype(BF16), nsa_out[l].astype(BF16), diff_out[l].astype(BF16), d_out[l].astype(BF16),
               w_out[l].astype(BF16)]
        conv_w = (conv_a_w[l], conv_a_b[l][None], conv_d_w[l], conv_d_b[l][None], d_norm_g[l][None], d_norm_b[l][None])
        lam_p, og = diff_lambda[l], diff_out_g[l][None]

        xp = ffn(xp, mod_p, 0, g[0:1], w1u, w1d, tm_p, tpg_p)
        zp = mix_in(xp, mod_p, g[1:2], wmix, 1024, sp // 1024)
        qn, qr, rows_p, win_p, dq, drows_p, ksd, vsd, dkb, dvb = qk_prep(
            zp, cos_p, sin_p, gains, gmat, tm_p, tpg_p, l, depth, drows_p)
        ua, ud, ca, cd = conv_branches(zp, zeros_a, zeros_d, *conv_w, nseq=bp, tm=tm_p, stride=1)
        cmp_p = nsa_compress(rows_p, w2)
        o_cmp, sel = nsa_cmp_select(qn, cmp_p, bp)
        o_slc = nsa_slc(qr, ksd, vsd, sel, emat_p, bp)
        o_win = nsa_win(qr, win_p, bp)
        uc = diff_attn(dq, dkb, dvb, lam_p, og, bp, lam_init)
        xp = merge_out(xp, mod_p, zp, ua, uc, ud, o_cmp, o_slc, o_win, *wts, tm=tm_p, tiles_per_group=tpg_p)
        xp = ffn(xp, mod_p, 6, g[2:3], w2u, w2d, tm_p, tpg_p)
        outs["nsa_p"].append(rows_p.reshape(bp, sp, 4, NSA_KV_HEADS, HEAD_DIM))
        wkeep = min(NSA_WINDOW, sp)
        outs["win_p"].append(win_p.reshape(bp, sp, 2, NSA_KV_HEADS, HEAD_DIM)[:, sp - wkeep:])
        outs["ca_p"].append(ca)
        outs["cd_p"].append(cd)

        xs = ffn(xs, mod_s, 0, g[0:1], w1u, w1d, tm_s, 1)
        zs = mix_in(xs, mod_s, g[1:2], wmix, tm_s, 1)
        qn, qr, rows_s, win_s, dq, drows_s = qk_prep(zs, cos_s, sin_s, gains, gmat, tm_s, 1, l, depth, drows_s)[:6]
        past_a = jnp.swapaxes(state_conv_a[l], 0, 1).reshape(1, (A_CONV - 1) * bs, A_WIDTH)
        past_d = jnp.swapaxes(state_conv_d[l], 0, 1).reshape(1, (D_CONV - 1) * bs, D_WIDTH)
        ua, ud, ca, cd = conv_branches(zs, past_a, past_d, *conv_w, nseq=1, tm=tm_s, stride=bs)
        rows_b = _to_seq_major(rows_s, bs, ns)
        new_t = _tokens_last(rows_s, bs, ns, (4, NSA_KV_HEADS, HEAD_DIM))
        win_new_t = _tokens_last(win_s, bs, ns, (2, NSA_KV_HEADS, HEAD_DIM))
        o_cmp, o_slc, o_win, win_out_t = nsa_sample(
            l, page_table, cache_nsa_t, new_t, _to_seq_major(qn, bs, ns, QPAD), _to_seq_major(qr, bs, ns, QPAD),
            win_state_t, win_new_t, wt, selm_s, ns, win_out_t)
        drows_b = _to_seq_major(drows_s[l].reshape(ns * bs, 2 * DIFF_HEADS * LANES), bs, ns, QPAD)
        uc = diff_sample(l, page_table, cache_diff2, drows_b, _to_seq_major(dq, bs, ns, QPAD), lam_p, og, ns, lam_init)
        xs = merge_out(xs, mod_s, zs, ua, _to_step_major(uc, ns).astype(BF16), ud, _to_step_major(o_cmp, ns),
                       _to_step_major(o_slc, ns), _to_step_major(o_win, ns), *wts, tm=tm_s, tiles_per_group=1)
        xs = ffn(xs, mod_s, 6, g[2:3], w2u, w2d, tm_s, 1)
        outs["nsa_s"].append(rows_b.reshape(bs, ns, 4, NSA_KV_HEADS, HEAD_DIM))
        outs["ca_s"].append(jnp.swapaxes(ca.reshape(A_CONV - 1, bs, A_WIDTH), 0, 1))
        outs["cd_s"].append(jnp.swapaxes(cd.reshape(D_CONV - 1, bs, D_WIDTH), 0, 1))

    y_p = xp.reshape(bp, sp, d)
    y_s = jnp.swapaxes(xs.reshape(ns, bs, d), 0, 1)
    st = {k: jnp.stack(v, axis=0) for k, v in outs.items()}
    diff_p = drows_p.reshape(depth, bp, sp, 2, DIFF_HEADS, 2 * HEAD_DIM)
    diff_s = jnp.swapaxes(drows_s.reshape(depth, ns, bs, 2, DIFF_HEADS, 2 * HEAD_DIM), 1, 2)
    win_s = jnp.moveaxis(win_out_t, -1, 2)
    return (y_p, y_s, st["nsa_p"], st["nsa_s"], diff_p, diff_s, st["win_p"], win_s,
            st["ca_p"], st["ca_s"], st["cd_p"], st["cd_s"])
```

```python
import functools
import math

import jax
import jax.numpy as jnp
from jax import lax
from jax.experimental import pallas as pl
from jax.experimental.pallas import tpu as pltpu

F32 = jnp.float32
BF16 = jnp.bfloat16

HEAD_DIM = 64
ROPE_THETA = 10000.0
NORM_EPS = 1e-6
A_WIDTH = 512
A_CONV = 3
NSA_HEADS = 8
NSA_KV_HEADS = 2
NSA_GROUP = NSA_HEADS // NSA_KV_HEADS
NSA_BLOCK = 64
NSA_TOPK = 16
NSA_WINDOW = 512
DIFF_HEADS = 4
D_WIDTH = 512
D_CONV = 31
N_BRANCH = 4
N_ADA = 9
LANES = 128
NEG = -1e30
ATT_SCALE = HEAD_DIM ** -0.5
Q_SCALE = ATT_SCALE * math.log2(math.e)
VMEM_LIMIT = 56 * 1024 * 1024

C_KV = 0
C_MERGE = 1024
C_GLU = 5120
C_AB = 6144
C_AC = 6656
C_AX = 7168
C_NQ = 7680
C_DQ = 8192
C_DK = 8704
C_DV = 9216
C_GC = 9728
C_GS = 10240
C_GW = 10752
MIX_W = 11264


def _cparams(sem):
    return pltpu.CompilerParams(dimension_semantics=sem, vmem_limit_bytes=VMEM_LIMIT)


def _sigmoid(x):
    return 1.0 / (1.0 + jnp.exp(-x))


def _silu(x):
    return x * _sigmoid(x)


def _rows_to(m, tm):
    r = m.shape[0]
    if r == 1 or r == tm:
        return m
    return jnp.concatenate([m] * (tm // r), axis=0)


def _modulate(x, g, shift, scale):
    ms = jnp.mean(x * x, axis=-1, keepdims=True)
    y = x * lax.rsqrt(ms + NORM_EPS) * g
    return y * (1.0 + scale) + shift


def _dot(a, b):
    return jnp.dot(a, b, preferred_element_type=F32)


def _dot_nt(a, b, precision=None):
    return lax.dot_general(a, b, (((1,), (1,)), ((), ())), preferred_element_type=F32, precision=precision)


def _lane_lo(shape):
    return (lax.broadcasted_iota(jnp.int32, shape, len(shape) - 1) % LANES) < HEAD_DIM


def _dup_halves(x, lo):
    xr = pltpu.roll(x, HEAD_DIM, 1)
    return jnp.where(lo, x, xr), jnp.where(lo, xr, x)


def _ada_kernel(c_ref, w_ref, b_ref, o_ref):
    s = _silu(c_ref[...])
    o_ref[...] = _dot(s.astype(BF16), w_ref[...].astype(BF16)) + b_ref[...]


def ada_proj(c_all, ada_w, ada_b):
    r, d = c_all.shape
    nl = ada_w.shape[0]
    return pl.pallas_call(
        _ada_kernel,
        grid=(nl, N_ADA),
        in_specs=[
            pl.BlockSpec((r, d), lambda l, k: (0, 0)),
            pl.BlockSpec((None, d, d), lambda l, k: (l, 0, k)),
            pl.BlockSpec((None, None, 1, d), lambda l, k: (l, k, 0, 0)),
        ],
        out_specs=pl.BlockSpec((None, None, r, d), lambda l, k: (l, k, 0, 0)),
        out_shape=jax.ShapeDtypeStruct((nl, N_ADA, r, d), F32),
        compiler_params=_cparams(("arbitrary", "arbitrary")),
        name="ada_proj",
    )(c_all, ada_w, ada_b.reshape(nl, N_ADA, 1, d))


def _mod_spec(mod, k, tiles_per_group, ngrid):
    _, _, r, d = mod.shape
    if ngrid == 1:
        return pl.BlockSpec((None, None, r, d), lambda i: (k, i // tiles_per_group, 0, 0))
    return pl.BlockSpec((None, None, r, d), lambda i, j: (k, i // tiles_per_group, 0, 0))


def _ffn_kernel(x_ref, sh_ref, sc_ref, gt_ref, g_ref, wa_ref, wv_ref, wd_ref, o_ref, h_ref, acc_ref, *, nj):
    j = pl.program_id(1)
    tm = x_ref.shape[0]

    @pl.when(j == 0)
    def _():
        h = _modulate(x_ref[...], g_ref[...], _rows_to(sh_ref[...], tm), _rows_to(sc_ref[...], tm))
        h_ref[...] = h.astype(BF16)
        acc_ref[...] = jnp.zeros_like(acc_ref)

    h = h_ref[...]
    a = _dot(h, wa_ref[...])
    v = _dot(h, wv_ref[...])
    act = (_silu(a) * v).astype(BF16)
    acc_ref[...] += _dot(act, wd_ref[...])

    @pl.when(j == nj - 1)
    def _():
        o_ref[...] = x_ref[...] + 0.5 * _rows_to(gt_ref[...], tm) * acc_ref[...]


def ffn(x, mod, k0, g, w_up, w_down, tm, tiles_per_group):
    t, d = x.shape
    f = w_down.shape[0]
    nj = 2
    tf = f // nj
    return pl.pallas_call(
        functools.partial(_ffn_kernel, nj=nj),
        grid=(t // tm, nj),
        in_specs=[
            pl.BlockSpec((tm, d), lambda i, j: (i, 0)),
            _mod_spec(mod, k0, tiles_per_group, 2),
            _mod_spec(mod, k0 + 1, tiles_per_group, 2),
            _mod_spec(mod, k0 + 2, tiles_per_group, 2),
            pl.BlockSpec((1, d), lambda i, j: (0, 0)),
            pl.BlockSpec((d, tf), lambda i, j: (0, j)),
            pl.BlockSpec((d, tf), lambda i, j: (0, nj + j)),
            pl.BlockSpec((tf, d), lambda i, j: (j, 0)),
        ],
        out_specs=pl.BlockSpec((tm, d), lambda i, j: (i, 0)),
        out_shape=jax.ShapeDtypeStruct((t, d), F32),
        scratch_shapes=[pltpu.VMEM((tm, d), BF16), pltpu.VMEM((tm, d), F32)],
        compiler_params=_cparams(("arbitrary", "arbitrary")),
        name="ffn",
    )(x, mod, mod, mod, g, w_up, w_up, w_down)


def _mix_kernel(x_ref, sh_ref, sc_ref, g_ref, w_ref, o_ref, h_ref):
    tm = x_ref.shape[0]

    @pl.when(pl.program_id(1) == 0)
    def _():
        h = _modulate(x_ref[...], g_ref[...], _rows_to(sh_ref[...], tm), _rows_to(sc_ref[...], tm))
        h_ref[...] = h.astype(BF16)

    o_ref[...] = _dot(h_ref[...], w_ref[...])


def mix_in(x, mod, g, w_mix, tm, tiles_per_group, tn=1024):
    t, d = x.shape
    nw = w_mix.shape[1]
    return pl.pallas_call(
        _mix_kernel,
        grid=(t // tm, nw // tn),
        in_specs=[
            pl.BlockSpec((tm, d), lambda i, j: (i, 0)),
            _mod_spec(mod, 3, tiles_per_group, 2),
            _mod_spec(mod, 4, tiles_per_group, 2),
            pl.BlockSpec((1, d), lambda i, j: (0, 0)),
            pl.BlockSpec((d, tn), lambda i, j: (0, j)),
        ],
        out_specs=pl.BlockSpec((tm, tn), lambda i, j: (i, j)),
        out_shape=jax.ShapeDtypeStruct((t, nw), F32),
        scratch_shapes=[pltpu.VMEM((tm, d), BF16)],
        compiler_params=_cparams(("arbitrary", "arbitrary")),
        name="mix_in",
    )(x, mod, mod, g, w_mix)


def _group_norm(x, gain, gmat):
    x2 = x * x
    hi = x2.astype(BF16)
    lo = (x2 - hi.astype(F32)).astype(BF16)
    ss = _dot(hi, gmat) + _dot(lo, gmat)
    return x * lax.rsqrt(ss * (1.0 / HEAD_DIM) + NORM_EPS) * gain


def _rope(x, cos, sin_signed, lo):
    sw = jnp.where(lo, pltpu.roll(x, LANES - HEAD_DIM // 2, 1), pltpu.roll(x, HEAD_DIM // 2, 1))
    return x * cos + sw * sin_signed


def _prep_kernel(kv_ref, nq_ref, dq_ref, dk_ref, dv_ref, cos_ref, sin_ref, gain_ref, gmat_ref,
                 qn_ref, qr_ref, rows_ref, win_ref, dqo_ref, drows_ref, ksd_ref, vsd_ref, dkb_ref, dvb_ref):
    cos = cos_ref[...]
    sin = sin_ref[...]
    gmat = gmat_ref[...]
    lo32 = (lax.broadcasted_iota(jnp.int32, cos.shape, 1) % HEAD_DIM) < HEAD_DIM // 2
    lo64 = _lane_lo(cos.shape)

    def gain(r):
        return gain_ref[r:r + 1, :]

    def piece(ref, p):
        return ref[:, p * LANES:(p + 1) * LANES]

    for p in range(4):
        qn = _group_norm(piece(nq_ref, p), gain(0), gmat)
        qn_ref[:, p * LANES:(p + 1) * LANES] = qn
        qr_ref[:, p * LANES:(p + 1) * LANES] = (_rope(qn, cos, sin, lo32) * Q_SCALE).astype(BF16)
    rows_ref[:, 0:LANES] = _group_norm(piece(kv_ref, 0), gain(1), gmat)
    rows_ref[:, LANES:2 * LANES] = piece(kv_ref, 1)
    k_slc = _rope(_group_norm(piece(kv_ref, 2), gain(2), gmat), cos, sin, lo32)
    v_slc = piece(kv_ref, 3)
    rows_ref[:, 2 * LANES:3 * LANES] = k_slc
    rows_ref[:, 3 * LANES:4 * LANES] = v_slc
    for hk, (kd, vd) in enumerate(zip(_dup_halves(k_slc, lo64), _dup_halves(v_slc, lo64))):
        ksd_ref[:, hk * LANES:(hk + 1) * LANES] = kd.astype(BF16)
        vsd_ref[:, hk * LANES:(hk + 1) * LANES] = vd.astype(BF16)
    win_ref[:, 0:LANES] = _rope(_group_norm(piece(kv_ref, 4), gain(3), gmat), cos, sin, lo32)
    win_ref[:, LANES:2 * LANES] = piece(kv_ref, 5)
    for p in range(4):
        dq = _rope(_group_norm(piece(dq_ref, p), gain(4), gmat), cos, sin, lo32)
        dqo_ref[:, p * LANES:(p + 1) * LANES] = (dq * Q_SCALE).astype(BF16)
        dk = _rope(_group_norm(piece(dk_ref, p), gain(5), gmat), cos, sin, lo32)
        drows_ref[:, p, :] = dk
        dkb_ref[:, p * LANES:(p + 1) * LANES] = dk.astype(BF16)
        drows_ref[:, DIFF_HEADS + p, :] = piece(dv_ref, p)
    dvb_ref[...] = dv_ref[...].astype(BF16)


def qk_prep(z, cos, sin, gains, gmat, tm, tiles_per_seq, layer, n_layers, drows_prev):
    t = z.shape[0]

    def zs(width, col):
        return pl.BlockSpec((tm, width), lambda i: (i, col // width))

    def os_(width):
        return pl.BlockSpec((tm, width), lambda i: (i, 0))

    tab = pl.BlockSpec((tm, LANES), lambda i: (i % tiles_per_seq, 0))
    outs = ((512, F32), (512, BF16), (512, F32), (256, F32), (512, BF16), None,
            (256, BF16), (256, BF16), (512, BF16), (512, BF16))
    drows_shape = jax.ShapeDtypeStruct((n_layers, t, 2 * DIFF_HEADS, LANES), F32)
    drows_spec = pl.BlockSpec((None, tm, 2 * DIFF_HEADS, LANES), lambda i: (layer, i, 0, 0))
    in_specs = [zs(768, C_KV), zs(512, C_NQ), zs(512, C_DQ), zs(512, C_DK), zs(512, C_DV), tab, tab,
                pl.BlockSpec((8, LANES), lambda i: (0, 0)), pl.BlockSpec((LANES, LANES), lambda i: (0, 0))]
    args = [z, z, z, z, z, cos, sin, gains, gmat]
    kern, aliases = _prep_kernel, {}
    if drows_prev is not None:
        in_specs.append(pl.BlockSpec(memory_space=pl.ANY))
        args.append(drows_prev)
        aliases = {len(args) - 1: 5}
        kern = _prep_kernel_aliased
    return pl.pallas_call(
        kern,
        grid=(t // tm,),
        in_specs=in_specs,
        out_specs=[drows_spec if o is None else os_(o[0]) for o in outs],
        out_shape=[drows_shape if o is None else jax.ShapeDtypeStruct((t, o[0]), o[1]) for o in outs],
        input_output_aliases=aliases,
        compiler_params=_cparams(("arbitrary",)),
        name="qk_prep",
    )(*args)


def _prep_kernel_aliased(*refs):
    _prep_kernel(*refs[:9], *refs[10:])


def _conv_kernel(ab_ref, ac_ref, ax_ref, glu_ref, pa_ref, pd_ref, wa_ref, ba_ref, wd_ref, bd_ref, lg_ref, lb_ref,
                 ua_ref, ud_ref, sa_ref, sd_ref, bufa, bufd, *, tm, ha, hd, stride, nt):
    i = pl.program_id(1)

    @pl.when(i == 0)
    def _():
        bufa[0:ha, :] = pa_ref[...]
        bufd[0:hd, :] = pd_ref[...]

    if nt > 1:
        @pl.when(i > 0)
        def _():
            bufa[0:ha, :] = bufa[tm:tm + ha, :]
            bufd[0:hd, :] = bufd[tm:tm + hd, :]

    bufa[ha:ha + tm, :] = ac_ref[...] * ax_ref[...]
    ya = ba_ref[...]
    for k in range(A_CONV):
        ya = ya + wa_ref[k:k + 1, :] * bufa[pl.ds(ha - (A_CONV - 1 - k) * stride, tm), :]
    ua_ref[...] = (ab_ref[...] * ya).astype(ua_ref.dtype)

    glu = glu_ref[...]
    bufd[hd:hd + tm, :] = glu[:, :D_WIDTH] * _sigmoid(glu[:, D_WIDTH:])
    yd = bd_ref[...]
    for k in range(D_CONV):
        yd = yd + wd_ref[k:k + 1, :] * bufd[pl.ds(hd - (D_CONV - 1 - k) * stride, tm), :]
    mu = jnp.mean(yd, axis=-1, keepdims=True)
    var = jnp.mean(jnp.square(yd - mu), axis=-1, keepdims=True)
    y = (yd - mu) * lax.rsqrt(var + NORM_EPS) * lg_ref[...] + lb_ref[...]
    ud_ref[...] = _silu(y).astype(ud_ref.dtype)

    @pl.when(i == nt - 1)
    def _():
        sa_ref[...] = bufa[ha + tm - (A_CONV - 1) * stride:ha + tm, :]
        sd_ref[...] = bufd[hd + tm - (D_CONV - 1) * stride:hd + tm, :]


def conv_branches(z, past_a, past_d, wa, ba, wd, bd, lg, lb, nseq, tm, stride):
    t = z.shape[0]
    nt = t // nseq // tm
    ha, hd = past_a.shape[1], past_d.shape[1]
    if nt > 1:
        assert tm >= ha and tm >= hd

    def zs(width, col):
        return pl.BlockSpec((tm, width), lambda s, i: (s * nt + i, col // width))

    def cs(shape):
        return pl.BlockSpec(shape, lambda s, i: (0, 0))

    sa_rows, sd_rows = (A_CONV - 1) * stride, (D_CONV - 1) * stride
    return pl.pallas_call(
        functools.partial(_conv_kernel, tm=tm, ha=ha, hd=hd, stride=stride, nt=nt),
        grid=(nseq, nt),
        in_specs=[zs(512, C_AB), zs(512, C_AC), zs(512, C_AX), zs(1024, C_GLU),
                  pl.BlockSpec((None, ha, A_WIDTH), lambda s, i: (s, 0, 0)),
                  pl.BlockSpec((None, hd, D_WIDTH), lambda s, i: (s, 0, 0)),
                  cs((A_CONV, A_WIDTH)), cs((1, A_WIDTH)), cs((D_CONV, D_WIDTH)), cs((1, D_WIDTH)),
                  cs((1, D_WIDTH)), cs((1, D_WIDTH))],
        out_specs=[pl.BlockSpec((tm, A_WIDTH), lambda s, i: (s * nt + i, 0)),
                   pl.BlockSpec((tm, D_WIDTH), lambda s, i: (s * nt + i, 0)),
                   pl.BlockSpec((None, sa_rows, A_WIDTH), lambda s, i: (s, 0, 0)),
                   pl.BlockSpec((None, sd_rows, D_WIDTH), lambda s, i: (s, 0, 0))],
        out_shape=[jax.ShapeDtypeStruct((t, A_WIDTH), BF16), jax.ShapeDtypeStruct((t, D_WIDTH), BF16),
                   jax.ShapeDtypeStruct((nseq, sa_rows, A_WIDTH), F32),
                   jax.ShapeDtypeStruct((nseq, sd_rows, D_WIDTH), F32)],
        scratch_shapes=[pltpu.VMEM((ha + tm, A_WIDTH), F32), pltpu.VMEM((hd + tm, D_WIDTH), F32)],
        compiler_params=_cparams(("arbitrary", "arbitrary")),
        name="conv_branches",
    )(z, z, z, z, past_a, past_d, wa, ba, wd, bd, lg, lb)


def _merge_kernel(x_ref, gt_ref, ua_ref, uc_ref, ud_ref, oc_ref, os_ref, ow_ref, gc_ref, gs_ref, gw_ref,
                  m0_ref, m1_ref, m2_ref, m3_ref, wa_ref, wb_ref, wc_ref, wd_ref, wo_ref, o_ref):
    tm = x_ref.shape[0]
    ub = (_sigmoid(gc_ref[...]) * oc_ref[...] + _sigmoid(gs_ref[...]) * os_ref[...]
          + _sigmoid(gw_ref[...]) * ow_ref[...]).astype(BF16)
    merged = _sigmoid(m0_ref[...]) * _dot(ua_ref[...], wa_ref[...])
    merged += _sigmoid(m1_ref[...]) * _dot(ub, wb_ref[...])
    merged += _sigmoid(m2_ref[...]) * _dot(uc_ref[...], wc_ref[...])
    merged += _sigmoid(m3_ref[...]) * _dot(ud_ref[...], wd_ref[...])
    y = _dot(merged.astype(BF16), wo_ref[...])
    o_ref[...] = x_ref[...] + _rows_to(gt_ref[...], tm) * y


def merge_out(x, mod, z, ua, uc, ud, o_cmp, o_slc, o_win, w_a, w_b, w_c, w_d, w_o, tm, tiles_per_group):
    t, d = x.shape

    def ts(width):
        return pl.BlockSpec((tm, width), lambda i: (i, 0))

    def zs(width, col):
        return pl.BlockSpec((tm, width), lambda i: (i, col // width))

    def ws(shape):
        return pl.BlockSpec(shape, lambda i: (0, 0))

    return pl.pallas_call(
        _merge_kernel,
        grid=(t // tm,),
        in_specs=[ts(d), _mod_spec(mod, 5, tiles_per_group, 1), ts(512), ts(512), ts(512), ts(512), ts(512), ts(512),
                  zs(512, C_GC), zs(512, C_GS), zs(512, C_GW),
                  zs(1024, C_MERGE), zs(1024, C_MERGE + 1024), zs(1024, C_MERGE + 2048), zs(1024, C_MERGE + 3072),
                  ws((512, d)), ws((512, d)), ws((512, d)), ws((512, d)), ws((d, d))],
        out_specs=ts(d),
        out_shape=jax.ShapeDtypeStruct((t, d), F32),
        compiler_params=_cparams(("arbitrary",)),
        name="merge_out",
    )(x, mod, ua, uc, ud, o_cmp, o_slc, o_win, z, z, z, z, z, z, z, w_a, w_b, w_c, w_d, w_o)


def _compress_kernel(r_ref, w_ref, o_ref):
    x = r_ref[...]
    nb = x.shape[0] // NSA_BLOCK
    o_ref[...] = jnp.sum(x.reshape(nb, NSA_BLOCK, x.shape[1]) * w_ref[...][None], axis=1)


def nsa_compress(rows, w2, tm=512):
    t = rows.shape[0]
    return pl.pallas_call(
        _compress_kernel,
        grid=(t // tm,),
        in_specs=[pl.BlockSpec((tm, 256), lambda i: (i, 0)), pl.BlockSpec((NSA_BLOCK, 256), lambda i: (0, 0))],
        out_specs=pl.BlockSpec((tm // NSA_BLOCK, 256), lambda i: (i, 0)),
        out_shape=jax.ShapeDtypeStruct((t // NSA_BLOCK, 256), F32),
        compiler_params=_cparams(("arbitrary",)),
        name="nsa_compress",
    )(rows, w2)


def _masked_softmax(s, mask):
    s = jnp.where(mask, s, NEG)
    p = jnp.exp(s - jnp.max(s, axis=-1, keepdims=True))
    p = jnp.where(mask, p, 0.0)
    return p / jnp.maximum(jnp.sum(p, axis=-1, keepdims=True), 1e-30)


def _top_blocks(imp, n_sel):
    idx = lax.broadcasted_iota(jnp.int32, imp.shape, 1)
    big = imp.shape[1]
    sel = jnp.zeros(imp.shape, F32)
    for _ in range(n_sel):
        m = jnp.max(imp, axis=-1, keepdims=True)
        first = jnp.min(jnp.where(imp == m, idx, big), axis=-1, keepdims=True)
        hit = idx == first
        sel = jnp.where(hit, 1.0, sel)
        imp = jnp.where(hit, -3.0, imp)
    return sel


def _top_blocks_by_rank(imp, n_sel):
    nb = imp.shape[1]
    idx = lax.broadcasted_iota(jnp.int32, imp.shape, 1)
    rank = jnp.zeros(imp.shape, F32)
    for d in range(1, nb):
        other = pltpu.roll(imp, d, 1)
        beats = (other > imp) | ((other == imp) & (idx >= d))
        rank = rank + jnp.where(beats, 1.0, 0.0)
    return jnp.where(rank < n_sel, 1.0, 0.0)


def _cmp_select(q, kdup, vdup, qpos, nb, n_sel):
    tq = q.shape[0]
    qlo = _lane_lo((tq, LANES))
    blk = lax.broadcasted_iota(jnp.int32, (1, nb), 1)
    complete = (blk + 1) * NSA_BLOCK - 1 <= qpos
    cur = qpos // NSA_BLOCK
    outs, sels = [], []
    for hk in range(NSA_KV_HEADS):
        imp = jnp.zeros((tq, nb), F32)
        vd = vdup[hk].astype(BF16)
        for a in (2 * hk, 2 * hk + 1):
            qp = q[:, a * LANES:(a + 1) * LANES]
            pair = []
            for par in range(2):
                qm = jnp.where(qlo if par == 0 else ~qlo, qp, 0.0)
                s = _dot_nt(qm, kdup[hk], precision=lax.Precision.HIGHEST)
                pc = _masked_softmax(s * ATT_SCALE, complete)
                imp = imp + pc
                pair.append(_dot(pc.astype(BF16), vd))
            outs.append(jnp.where(qlo, pair[0], pair[1]))
        forced = (blk == cur) | (blk == 0)
        imp = jnp.where(forced, NSA_GROUP + 1.0, imp)
        imp = jnp.where(blk <= cur, imp, -1.0)
        sels.append(_top_blocks(imp, n_sel))
    return jnp.concatenate(outs, axis=1), sels


def _cmpsel_kernel(q_ref, c_ref, o_ref, sel_ref, *, tq, n_sel):
    qpos = pl.program_id(1) * tq + lax.broadcasted_iota(jnp.int32, (tq, 1), 0)
    nb = c_ref.shape[0]
    lo = _lane_lo((nb, LANES))
    kdup = _dup_halves(c_ref[:, 0:LANES], lo)
    vdup = _dup_halves(c_ref[:, LANES:2 * LANES], lo)
    o, sels = _cmp_select(q_ref[...], kdup, vdup, qpos, nb, n_sel)
    o_ref[...] = o
    sel_ref[...] = jnp.concatenate(sels, axis=1).astype(sel_ref.dtype)


def nsa_cmp_select(q_n, cmp, nseq, tq=256):
    t = q_n.shape[0]
    nq = t // nseq // tq
    nb = cmp.shape[0] // nseq
    return pl.pallas_call(
        functools.partial(_cmpsel_kernel, tq=tq, n_sel=min(NSA_TOPK, nb)),
        grid=(nseq, nq),
        in_specs=[pl.BlockSpec((tq, 512), lambda b, i: (b * nq + i, 0)),
                  pl.BlockSpec((nb, 256), lambda b, i: (b, 0))],
        out_specs=[pl.BlockSpec((tq, 512), lambda b, i: (b * nq + i, 0)),
                   pl.BlockSpec((tq, 2 * nb), lambda b, i: (b * nq + i, 0))],
        out_shape=[jax.ShapeDtypeStruct((t, 512), F32), jax.ShapeDtypeStruct((t, 2 * nb), BF16)],
        compiler_params=_cparams(("arbitrary", "arbitrary")),
        name="nsa_cmp_select",
    )(q_n, cmp)


def _online_step(s, m, acc, v_ext, exp_dtype=F32):
    tk = s.shape[1]
    m_new = jnp.maximum(m, jnp.max(s, axis=-1, keepdims=True))
    x = s - jnp.concatenate([m_new] * (tk // LANES), axis=1)
    p = jnp.exp2(x.astype(exp_dtype)).astype(BF16)
    alpha = jnp.exp2(m - m_new)
    if acc.shape[1] != LANES:
        alpha = jnp.concatenate([alpha] * (acc.shape[1] // LANES), axis=1)
    return m_new, alpha * acc + _dot(p, v_ext)


GROWTH_LIMIT = 64.0


def _single_pass_step(s, ref, acc, growth, v_ext):
    tk = s.shape[1]
    p = jnp.exp2(s - jnp.concatenate([ref] * (tk // LANES), axis=1)).astype(BF16)
    row_max = jnp.max(s, axis=-1, keepdims=True)
    growth = jnp.maximum(growth, row_max - ref)
    ref_new = jnp.maximum(ref, row_max)
    alpha = jnp.exp2(ref - ref_new)
    if acc.shape[1] != LANES:
        alpha = jnp.concatenate([alpha] * (acc.shape[1] // LANES), axis=1)
    return ref_new, (acc + _dot(p, v_ext)) * alpha, growth


def _lane_bcast0(x):
    return jnp.broadcast_to(x[:, 0:1], x.shape)


def _slc_kernel(q_ref, k_ref, v_ref, sel_ref, e_ref, o_ref, *, tq, tk, nb):
    qi = pl.program_id(1)
    n_full = (qi * tq) // tk
    rows = NSA_GROUP * tq
    qlo = _lane_lo((tq, LANES))
    klo = _lane_lo((tk, LANES))
    zero = jnp.zeros((tq, LANES), BF16)
    qpos = qi * tq + lax.broadcasted_iota(jnp.int32, (tq, 1), 0)
    col = lax.broadcasted_iota(jnp.int32, (1, tk), 1)
    for hk in range(NSA_KV_HEADS):
        parts = []
        for c in range(NSA_GROUP):
            h = hk * NSA_GROUP + c
            qp = q_ref[:, (h // 2) * LANES:(h // 2 + 1) * LANES]
            parts.append(jnp.where(qlo, qp, zero) if h % 2 == 0 else jnp.where(qlo, zero, qp))
        qst = jnp.concatenate(parts, axis=0)
        sel = sel_ref[:, hk * nb:(hk + 1) * nb]

        def tile(j, diag, qst=qst, sel=sel, hk=hk):
            off = pl.multiple_of(j * tk, tk)
            k = k_ref[pl.ds(off, tk), hk * LANES:(hk + 1) * LANES]
            v = v_ref[pl.ds(off, tk), hk * LANES:(hk + 1) * LANES]
            bias = (_dot(sel, e_ref[j]) - 1.0) * (-NEG)
            if diag:
                bias = jnp.where(j * tk + col <= qpos, bias, NEG)
            s = _dot_nt(qst, k) + jnp.concatenate([bias] * NSA_GROUP, axis=0)
            return s, jnp.where(klo, v, jnp.ones_like(v))

        def finish(acc, hk=hk):
            o = acc / jnp.maximum(pltpu.roll(acc, HEAD_DIM, 1), 1e-30)
            for pair in range(NSA_GROUP // 2):
                a = hk * (NSA_GROUP // 2) + pair
                o_even = o[(2 * pair) * tq:(2 * pair + 1) * tq]
                o_odd = o[(2 * pair + 1) * tq:(2 * pair + 2) * tq]
                o_ref[:, a * LANES:(a + 1) * LANES] = jnp.where(qlo, o_even, pltpu.roll(o_odd, HEAD_DIM, 1))

        def body(j, carry, tile=tile):
            s, vx = tile(j, False)
            return _online_step(s, *carry, vx, exp_dtype=BF16)

        init = (jnp.full((rows, LANES), NEG, F32), jnp.zeros((rows, LANES), F32))
        m, acc = lax.fori_loop(0, n_full, body, init)
        s, vx = tile(n_full, True)
        finish(_online_step(s, m, acc, vx, exp_dtype=BF16)[1])


def nsa_slc(q_rot, ksd, vsd, sel, emat, nseq, tq=256):
    t = q_rot.shape[0]
    s = t // nseq
    nq = s // tq
    nk, nb, tk = emat.shape
    return pl.pallas_call(
        functools.partial(_slc_kernel, tq=tq, tk=tk, nb=nb),
        grid=(nseq, nq),
        in_specs=[pl.BlockSpec((tq, 512), lambda b, i: (b * nq + i, 0)),
                  pl.BlockSpec((s, 256), lambda b, i: (b, 0)),
                  pl.BlockSpec((s, 256), lambda b, i: (b, 0)),
                  pl.BlockSpec((tq, 2 * nb), lambda b, i: (b * nq + i, 0)),
                  pl.BlockSpec((nk, nb, tk), lambda b, i: (0, 0, 0))],
        out_specs=pl.BlockSpec((tq, 512), lambda b, i: (b * nq + i, 0)),
        out_shape=jax.ShapeDtypeStruct((t, 512), F32),
        compiler_params=_cparams(("arbitrary", "arbitrary")),
        name="nsa_slc",
    )(q_rot, ksd, vsd, sel, emat)


def _win_kernel(q_ref, wp_ref, wc_ref, o_ref, *, tq):
    i = pl.program_id(1)
    lo = _lane_lo((tq, LANES))
    qpos = i * tq + lax.broadcasted_iota(jnp.int32, (tq, 1), 0)
    chunks = []
    for off, ref in ((-tq, wp_ref), (0, wc_ref)):
        kpos = i * tq + off + lax.broadcasted_iota(jnp.int32, (1, tq), 1)
        mask = (kpos >= 0) & (kpos <= qpos) & (qpos - kpos <= NSA_WINDOW)
        chunks.append((mask, _dup_halves(ref[:, 0:LANES], lo), _dup_halves(ref[:, LANES:2 * LANES], lo)))
    zero = jnp.zeros((tq, LANES), BF16)
    for a in range(NSA_HEADS // 2):
        pair = []
        for par in range(2):
            hk = (2 * a + par) // NSA_GROUP
            qp = q_ref[:, a * LANES:(a + 1) * LANES]
            qm = jnp.where(lo, qp, zero) if par == 0 else jnp.where(lo, zero, qp)
            ss = [jnp.where(mask, _dot_nt(qm, kd[hk].astype(BF16)), NEG) for mask, kd, _ in chunks]
            m = jnp.maximum(jnp.max(ss[0], axis=-1, keepdims=True), jnp.max(ss[1], axis=-1, keepdims=True))
            ps = [jnp.where(ch[0], jnp.exp2(s - m), 0.0) for s, ch in zip(ss, chunks)]
            l = jnp.sum(ps[0], axis=-1, keepdims=True) + jnp.sum(ps[1], axis=-1, keepdims=True)
            inv = 1.0 / jnp.maximum(l, 1e-30)
            o = _dot((ps[0] * inv).astype(BF16), chunks[0][2][hk].astype(BF16))
            o += _dot((ps[1] * inv).astype(BF16), chunks[1][2][hk].astype(BF16))
            pair.append(o)
        o_ref[:, a * LANES:(a + 1) * LANES] = jnp.where(lo, pair[0], pair[1])


def nsa_win(q_rot, win_rows, nseq, tq=NSA_WINDOW):
    t = q_rot.shape[0]
    nq = t // nseq // tq
    return pl.pallas_call(
        functools.partial(_win_kernel, tq=tq),
        grid=(nseq, nq),
        in_specs=[pl.BlockSpec((tq, 512), lambda b, i: (b * nq + i, 0)),
                  pl.BlockSpec((tq, 256), lambda b, i: (b * nq + jnp.maximum(i - 1, 0), 0)),
                  pl.BlockSpec((tq, 256), lambda b, i: (b * nq + i, 0))],
        out_specs=pl.BlockSpec((tq, 512), lambda b, i: (b * nq + i, 0)),
        out_shape=jax.ShapeDtypeStruct((t, 512), F32),
        compiler_params=_cparams(("arbitrary", "arbitrary")),
        name="nsa_win",
    )(q_rot, win_rows, win_rows)


def _diff_lambda(lam_ref, lam_init):
    lp = lam_ref[...]
    a = jnp.sum(lp[0:1] * lp[1:2], axis=-1, keepdims=True)
    b = jnp.sum(lp[2:3] * lp[3:4], axis=-1, keepdims=True)
    return jnp.exp(a) - jnp.exp(b) + lam_init


def _diff_finish(o1, o2, lam, g, lam_init):
    o = o1 - lam * o2
    ms = jnp.mean(o * o, axis=-1, keepdims=True)
    return o * lax.rsqrt(ms + NORM_EPS) * g * (1.0 - lam_init)


def _diff_kernel(q_ref, k_ref, v_ref, lam_ref, g_ref, o_ref, *, tq, tk, lam_init):
    qi = pl.program_id(2)
    n_full = (qi * tq) // tk
    lo = _lane_lo((tq, LANES))
    q = q_ref[...]
    zero = jnp.zeros_like(q)
    qst = jnp.concatenate([jnp.where(lo, q, zero), jnp.where(lo, zero, q)], axis=0)
    qpos = qi * tq + lax.broadcasted_iota(jnp.int32, (tq, 1), 0)
    col = lax.broadcasted_iota(jnp.int32, (1, tk), 1)
    bias = jnp.where(n_full * tk + col <= qpos, 0.0, NEG)
    bias = jnp.concatenate([bias, bias], axis=0)
    ones = jnp.ones((tk, LANES), BF16)

    def tile(j):
        off = pl.multiple_of(j * tk, tk)
        return _dot_nt(qst, k_ref[pl.ds(off, tk), :]), jnp.concatenate([v_ref[pl.ds(off, tk), :], ones], axis=1)

    def walk(step, init):
        def full(j, carry):
            s, vx = tile(j)
            return step(s, carry, vx)

        carry = lax.fori_loop(0, n_full, full, init)
        s, vx = tile(n_full)
        return step(s + bias, carry, vx)

    def finish(acc):
        o = acc[:, 0:LANES] / jnp.maximum(acc[:, LANES:2 * LANES], 1e-30)
        lam = _diff_lambda(lam_ref, lam_init)
        o_ref[...] = _diff_finish(o[0:tq], o[tq:2 * tq], lam, g_ref[...], lam_init).astype(o_ref.dtype)

    ref0 = _lane_bcast0(_dot_nt(qst, k_ref[0:LANES, :]))
    init = (ref0, jnp.zeros((2 * tq, 2 * LANES), F32), jnp.zeros((2 * tq, LANES), F32))
    _, acc, growth = walk(lambda s, carry, vx: _single_pass_step(s, *carry, vx), init)
    fast_ok = jnp.max(growth) <= GROWTH_LIMIT

    @pl.when(fast_ok)
    def _():
        finish(acc)

    @pl.when(jnp.logical_not(fast_ok))
    def _():
        init = (jnp.full((2 * tq, LANES), NEG, F32), jnp.zeros((2 * tq, 2 * LANES), F32))
        finish(walk(lambda s, carry, vx: _online_step(s, *carry, vx), init)[1])


def diff_attn(dq, dk, dv, lam_p, out_g, nseq, lam_init, tq=256, tk=1024):
    t = dq.shape[0]
    s = t // nseq
    nq = s // tq
    return pl.pallas_call(
        functools.partial(_diff_kernel, tq=tq, tk=tk, lam_init=lam_init),
        grid=(nseq, DIFF_HEADS, nq),
        in_specs=[pl.BlockSpec((tq, LANES), lambda b, h, i: (b * nq + i, h)),
                  pl.BlockSpec((s, LANES), lambda b, h, i: (b, h)),
                  pl.BlockSpec((s, LANES), lambda b, h, i: (b, h)),
                  pl.BlockSpec((4, HEAD_DIM), lambda b, h, i: (0, 0)),
                  pl.BlockSpec((1, LANES), lambda b, h, i: (0, 0))],
        out_specs=pl.BlockSpec((tq, LANES), lambda b, h, i: (b * nq + i, h)),
        out_shape=jax.ShapeDtypeStruct((t, 512), BF16),
        compiler_params=_cparams(("arbitrary", "arbitrary", "arbitrary")),
        name="diff_attn",
    )(dq, dk, dv, lam_p, out_g)


QPAD = 8


def _pad_keys(x):
    return jnp.concatenate([x, jnp.zeros((LANES - x.shape[0], x.shape[1]), x.dtype)], axis=0)


def _nsa_sample_kernel(pt_ref, *refs, n_pages, past_len, n_new, n_sel):
    chunks = refs[:n_pages + 1]
    qn_ref, qr_ref, wst_ref, wnew_ref, wt_ref, selm_ref = refs[n_pages + 1:n_pages + 7]
    oc_ref, os_ref, ow_ref, wout_ref = refs[-4:]
    n_real = (past_len + n_new + NSA_BLOCK - 1) // NSA_BLOCK
    nbp = selm_ref.shape[1]
    page = chunks[0].shape[3]

    def twice(x):
        return jnp.concatenate([x, x], axis=0)

    wt = wt_ref[...]
    y = jnp.concatenate([ch[0:2].reshape(4 * HEAD_DIM, page) * wt for ch in chunks], axis=1)
    hi = y.astype(BF16)
    selm = selm_ref[...]
    cmp_t = _dot(hi, selm)
    yk = y[0:2 * HEAD_DIM]
    cmp_k = cmp_t[0:2 * HEAD_DIM] + _dot((yk - hi[0:2 * HEAD_DIM].astype(F32)).astype(BF16), selm)

    tok = lax.broadcasted_iota(jnp.int32, (QPAD, 1), 0)
    qpos = past_len + tok
    rows = NSA_GROUP * QPAD
    qpos_st = jnp.concatenate([qpos] * NSA_GROUP, axis=0)
    qlo = _lane_lo((QPAD, LANES))

    def stack_group(q_ref, hk, zero):
        parts = []
        for c in range(NSA_GROUP):
            h = hk * NSA_GROUP + c
            qp = q_ref[:, (h // 2) * LANES:(h // 2 + 1) * LANES]
            parts.append(jnp.where(qlo, qp, zero) if h % 2 == 0 else jnp.where(qlo, zero, qp))
        return jnp.concatenate(parts, axis=0)

    def emit(o_ref, per_hk):
        for a in range(NSA_HEADS // 2):
            hk, c = (2 * a) // NSA_GROUP, (2 * a) % NSA_GROUP
            o = per_hk[hk]
            o_ref[:, a * LANES:(a + 1) * LANES] = jnp.where(qlo, o[c * QPAD:(c + 1) * QPAD], o[(c + 1) * QPAD:(c + 2) * QPAD])

    blk = lax.broadcasted_iota(jnp.int32, (1, nbp), 1)
    complete = ((blk + 1) * NSA_BLOCK - 1 <= qpos_st) & (blk < n_real)
    cur = qpos // NSA_BLOCK
    per_hk, imps = [], []
    for hk in range(NSA_KV_HEADS):
        kd = twice(cmp_k[hk * HEAD_DIM:(hk + 1) * HEAD_DIM])
        vd = twice(cmp_t[(2 + hk) * HEAD_DIM:(3 + hk) * HEAD_DIM]).astype(BF16)
        s = jnp.dot(stack_group(qn_ref, hk, jnp.zeros((QPAD, LANES), F32)), kd,
                    precision=lax.Precision.HIGHEST, preferred_element_type=F32)
        pc = _masked_softmax(s * ATT_SCALE, complete)
        per_hk.append(_dot_nt(pc.astype(BF16), vd))
        imp = functools.reduce(lambda a, b: a + b, [pc[c * QPAD:(c + 1) * QPAD] for c in range(NSA_GROUP)])
        imp = jnp.where((blk == cur) | (blk == 0), NSA_GROUP + 1.0, imp)
        imp = jnp.where(blk <= cur, imp, -1.0)
        imps.append(jnp.where(blk < n_real, imp, -2.0))
    emit(oc_ref, per_hk)
    sel_all = _top_blocks_by_rank(jnp.concatenate(imps, axis=0), n_sel)
    sels = [sel_all[hk * QPAD:(hk + 1) * QPAD] for hk in range(NSA_KV_HEADS)]

    qst = [stack_group(qr_ref, hk, jnp.zeros((QPAD, LANES), BF16)) for hk in range(NSA_KV_HEADS)]
    key_new = lax.broadcasted_iota(jnp.int32, (1, LANES), 1)
    new_valid = (key_new < n_new) & (past_len + key_new <= qpos_st)

    def attend(kvs, biases, hk):
        k_all = jnp.concatenate([twice(k).astype(BF16) for k, _ in kvs], axis=1)
        v_all = jnp.concatenate([twice(v).astype(BF16) for _, v in kvs], axis=1)
        s = _dot(qst[hk], k_all) + jnp.concatenate(biases, axis=1)
        p = jnp.exp2(s - jnp.max(s, axis=-1, keepdims=True))
        p = p / jnp.maximum(jnp.sum(p, axis=-1, keepdims=True), 1e-30)
        return _dot_nt(p.astype(BF16), v_all)

    per_hk = []
    for hk in range(NSA_KV_HEADS):
        selx = _dot_nt(sels[hk].astype(BF16), selm)
        selx = jnp.concatenate([selx] * NSA_GROUP, axis=0)
        kvs, biases = [], []
        for j, ch in enumerate(chunks):
            kvs.append((ch[2, hk], ch[3, hk]))
            bias = (selx[:, j * page:(j + 1) * page] - 1.0) * (-NEG)
            biases.append(jnp.where(new_valid, bias, NEG) if j == n_pages else bias)
        per_hk.append(attend(kvs, biases, hk))
    emit(os_ref, per_hk)

    wbuf = wst_ref.shape[3]
    kpos = past_len - wbuf + lax.broadcasted_iota(jnp.int32, (1, wbuf), 1)
    bias_st = jnp.where((kpos >= 0) & (kpos <= qpos_st) & (qpos_st - kpos <= NSA_WINDOW), 0.0, NEG)
    bias_new = jnp.where(new_valid & (qpos_st - (past_len + key_new) <= NSA_WINDOW), 0.0, NEG)
    per_hk = []
    for hk in range(NSA_KV_HEADS):
        kvs = [(wst_ref[0, hk], wst_ref[1, hk]), (wnew_ref[0, hk], wnew_ref[1, hk])]
        per_hk.append(attend(kvs, [bias_st, bias_new], hk))
    emit(ow_ref, per_hk)

    lane = lax.broadcasted_iota(jnp.int32, (HEAD_DIM, wbuf), 1)
    for r in range(2):
        for hk in range(NSA_KV_HEADS):
            shifted = pltpu.roll(wst_ref[r, hk], wbuf - n_new, 1)
            tail = jnp.concatenate([jnp.zeros((HEAD_DIM, wbuf - LANES), F32),
                                    pltpu.roll(wnew_ref[r, hk], LANES - n_new, 1)], axis=1)
            wout_ref[r, hk] = jnp.where(lane >= wbuf - n_new, tail, shifted)


def nsa_sample(layer, page_table, cache_t, new_t, q_n, q_rot, win_t, win_new_t, wt, selm, n_new, wout_prev):
    nb_, n_pages = page_table.shape
    page = cache_t.shape[5]
    past_len = n_pages * page
    n_real = (past_len + n_new + NSA_BLOCK - 1) // NSA_BLOCK
    wbuf = win_t.shape[5]
    kv = (NSA_KV_HEADS, HEAD_DIM)

    def page_spec(j):
        return pl.BlockSpec((None, None, 4) + kv + (page,), lambda b, pt: (layer, pt[b, j], 0, 0, 0, 0))

    def seq(shape):
        return pl.BlockSpec((None,) + shape, lambda b, pt: (b,) + (0,) * len(shape))

    def const(shape):
        return pl.BlockSpec(shape, lambda b, pt: (0,) * len(shape))

    win_spec = pl.BlockSpec((None, None, 2) + kv + (wbuf,), lambda b, pt: (layer, b, 0, 0, 0, 0))
    in_specs = [page_spec(j) for j in range(n_pages)] + [
        seq((4,) + kv + (page,)), seq((QPAD, 512)), seq((QPAD, 512)), win_spec,
        seq((2,) + kv + (LANES,)), const(wt.shape), const(selm.shape)]
    args = [page_table] + [cache_t] * n_pages + [new_t, q_n, q_rot, win_t, win_new_t, wt, selm]
    aliases = {}
    if wout_prev is not None:
        in_specs.append(pl.BlockSpec(memory_space=pl.ANY))
        args.append(wout_prev)
        aliases = {len(args) - 1: 3}
    grid_spec = pltpu.PrefetchScalarGridSpec(
        num_scalar_prefetch=1,
        grid=(nb_,),
        in_specs=in_specs,
        out_specs=[seq((QPAD, 512)), seq((QPAD, 512)), seq((QPAD, 512)), win_spec],
    )
    return pl.pallas_call(
        functools.partial(_nsa_sample_kernel, n_pages=n_pages, past_len=past_len, n_new=n_new,
                          n_sel=min(NSA_TOPK, n_real)),
        grid_spec=grid_spec,
        out_shape=[jax.ShapeDtypeStruct((nb_, QPAD, 512), F32)] * 3 + [jax.ShapeDtypeStruct(win_t.shape, F32)],
        input_output_aliases=aliases,
        compiler_params=_cparams(("arbitrary",)),
        name="nsa_sample",
    )(*args)


def _diff_sample_kernel(pt_ref, *refs, n_pages, past_len, n_new, lam_init):
    pages = refs[:n_pages]
    new_ref, q_ref, lam_ref, g_ref, o_ref = refs[n_pages:]
    lo = _lane_lo((QPAD, LANES))
    tok = lax.broadcasted_iota(jnp.int32, (2 * QPAD, 1), 0) % QPAD
    key_new = lax.broadcasted_iota(jnp.int32, (1, LANES), 1)
    new_valid = (key_new < n_new) & (key_new <= tok)
    lam = _diff_lambda(lam_ref, lam_init)
    q = q_ref[...]
    new = new_ref[...]
    zero = jnp.zeros((QPAD, LANES), BF16)
    page = pages[0].shape[0] // (2 * DIFF_HEADS)
    bias = jnp.concatenate([jnp.zeros((2 * QPAD, n_pages * page), F32), jnp.where(new_valid, 0.0, NEG)], axis=1)
    for h in range(DIFF_HEADS):
        qp = q[:, h * LANES:(h + 1) * LANES]
        qst = jnp.concatenate([jnp.where(lo, qp, zero), jnp.where(lo, zero, qp)], axis=0)
        ks = [pages[j][pl.ds(h, page, stride=2 * DIFF_HEADS), :] for j in range(n_pages)]
        ks.append(_pad_keys(new[:, h * LANES:(h + 1) * LANES]))
        vs = [pages[j][pl.ds(DIFF_HEADS + h, page, stride=2 * DIFF_HEADS), :] for j in range(n_pages)]
        vs.append(_pad_keys(new[:, (DIFF_HEADS + h) * LANES:(DIFF_HEADS + h + 1) * LANES]))
        k_all = jnp.concatenate([k.astype(BF16) for k in ks], axis=0)
        v_all = jnp.concatenate([v.astype(BF16) for v in vs], axis=0)
        s = _dot_nt(qst, k_all) + bias
        p = jnp.exp2(s - jnp.max(s, axis=-1, keepdims=True))
        p = p / jnp.maximum(jnp.sum(p, axis=-1, keepdims=True), 1e-30)
        o = _dot((p[0:QPAD] - lam * p[QPAD:2 * QPAD]).astype(BF16), v_all)
        ms = jnp.mean(o * o, axis=-1, keepdims=True)
        o_ref[:, h * LANES:(h + 1) * LANES] = (o * lax.rsqrt(ms + NORM_EPS) * g_ref[...] * (1.0 - lam_init)).astype(o_ref.dtype)


def diff_sample(layer, page_table, cache, new_rows, dq, lam_p, out_g, n_new, lam_init):
    nb_, n_pages = page_table.shape
    page = cache.shape[2] // (2 * DIFF_HEADS)

    def page_spec(j):
        return pl.BlockSpec((None, None, page * 2 * DIFF_HEADS, LANES), lambda b, pt: (layer, pt[b, j], 0, 0))

    grid_spec = pltpu.PrefetchScalarGridSpec(
        num_scalar_prefetch=1,
        grid=(nb_,),
        in_specs=[page_spec(j) for j in range(n_pages)] + [
            pl.BlockSpec((None, QPAD, 1024), lambda b, pt: (b, 0, 0)),
            pl.BlockSpec((None, QPAD, 512), lambda b, pt: (b, 0, 0)),
            pl.BlockSpec((4, HEAD_DIM), lambda b, pt: (0, 0)),
            pl.BlockSpec((1, LANES), lambda b, pt: (0, 0))],
        out_specs=pl.BlockSpec((None, QPAD, 512), lambda b, pt: (b, 0, 0)),
    )
    return pl.pallas_call(
        functools.partial(_diff_sample_kernel, n_pages=n_pages, past_len=n_pages * page, n_new=n_new, lam_init=lam_init),
        grid_spec=grid_spec,
        out_shape=jax.ShapeDtypeStruct((nb_, QPAD, 512), F32),
        compiler_params=_cparams(("arbitrary",)),
        name="diff_sample",
    )(page_table, *([cache] * n_pages), new_rows, dq, lam_p, out_g)


def _mix_weight(w):
    d = w.shape[0]
    splits = (A_WIDTH, A_WIDTH, A_WIDTH, 512, 768, 3 * NSA_HEADS, 512, 512, 512, 2 * D_WIDTH, N_BRANCH * d)
    offs = [0]
    for s in splits:
        offs.append(offs[-1] + s)
    part = {n: w[:, offs[i]:offs[i + 1]] for i, n in enumerate(
        ("ab", "ac", "ax", "nq", "kv", "g", "dq", "dk", "dv", "glu", "merge"))}
    g = part["g"].reshape(d, NSA_HEADS, 3)
    gates = [jnp.repeat(g[:, :, k], HEAD_DIM, axis=1) for k in range(3)]
    pad = jnp.zeros((d, C_MERGE - 768), w.dtype)
    cols = [part["kv"], pad, part["merge"], part["glu"], part["ab"], part["ac"], part["ax"], part["nq"],
            part["dq"], part["dk"], part["dv"]] + gates
    out = jnp.concatenate(cols, axis=1).astype(BF16)
    assert out.shape[1] == MIX_W
    return out


def _rope_tables(pos):
    half = HEAD_DIM // 2
    inv = ROPE_THETA ** (-jnp.arange(half, dtype=F32) / half)
    ang = pos.astype(F32)[:, None] * inv[None, :]
    cos, sin = jnp.cos(ang), jnp.sin(ang)
    return jnp.tile(jnp.concatenate([cos, cos], axis=1), (1, 2)), jnp.tile(jnp.concatenate([-sin, sin], axis=1), (1, 2))


def _tile_gain(g):
    return jnp.tile(g.astype(F32), LANES // HEAD_DIM)


def _expansion(n_tiles, nb_rows, tk):
    j = jnp.arange(n_tiles)[:, None, None]
    n = jnp.arange(nb_rows)[None, :, None]
    k = jnp.arange(tk)[None, None, :]
    return ((j * tk + k) // NSA_BLOCK == n).astype(BF16)


def _tokens_last(x, nb_, n_new, lead):
    y = jnp.moveaxis(x.reshape((n_new, nb_) + lead), 0, -1)
    return jnp.pad(y, [(0, 0)] * (y.ndim - 1) + [(0, LANES - n_new)])


def _to_seq_major(x, nb_, n_new, pad_to=None):
    y = jnp.swapaxes(x.reshape(n_new, nb_, x.shape[-1]), 0, 1)
    if pad_to is not None and pad_to > n_new:
        y = jnp.pad(y, ((0, 0), (0, pad_to - n_new), (0, 0)))
    return y


def _to_step_major(y, n_new):
    return jnp.swapaxes(y[:, :n_new], 0, 1).reshape(-1, y.shape[-1])


def kernel(x_prompt, x_sample, cache_nsa, cache_diff, state_nsa_win, state_conv_a, state_conv_d, page_table,
           c_prompt, c_sample, ada_w, ada_b, norm_g, ffn1_up, ffn1_down, ffn2_up, ffn2_down, w_mix_in,
           conv_a_w, conv_a_b, a_out, nsa_qk_g, nsa_cmp_w, nsa_out, diff_qk_g, diff_lambda, diff_out_g, diff_out,
           conv_d_w, conv_d_b, d_norm_g, d_norm_b, d_out, w_out):
    bp, sp, d = x_prompt.shape
    bs, ns, _ = x_sample.shape
    depth = ada_w.shape[0]
    n_pages = page_table.shape[1]
    page = cache_nsa.shape[2]
    past_len = n_pages * page
    wbuf = state_nsa_win.shape[2]

    rows = bp + bs
    rpad = -(-rows // 8) * 8
    c_all = jnp.pad(jnp.concatenate([c_prompt, c_sample], axis=0), ((0, rpad - rows), (0, 0)))
    ada = ada_proj(c_all, ada_w, ada_b)

    xp = x_prompt.reshape(bp * sp, d)
    xs = jnp.swapaxes(x_sample, 0, 1).reshape(ns * bs, d)
    tm_p, tm_s = 512, ns * bs
    tpg_p = sp // tm_p

    cos_p, sin_p = _rope_tables(jnp.arange(sp))
    cos_s, sin_s = _rope_tables(jnp.repeat(past_len + jnp.arange(ns), bs))
    gmat = (jnp.arange(LANES)[:, None] // HEAD_DIM == jnp.arange(LANES)[None, :] // HEAD_DIM).astype(BF16)
    tk_slc = 512
    emat_p = _expansion(sp // tk_slc, sp // NSA_BLOCK, tk_slc)
    n_real_s = (past_len + ns + NSA_BLOCK - 1) // NSA_BLOCK
    nbp_s = -(-n_real_s // LANES) * LANES
    selm_s = _expansion(1, nbp_s, past_len + page)[0].T
    cache_nsa_t = jnp.transpose(cache_nsa, (0, 1, 3, 4, 5, 2))
    win_state_t = jnp.transpose(state_nsa_win, (0, 1, 3, 4, 5, 2))
    cache_diff2 = cache_diff.reshape(depth, cache_diff.shape[1], page * 2 * DIFF_HEADS, 2 * HEAD_DIM)

    zeros_a = jnp.zeros((bp, 8, A_WIDTH), F32)
    zeros_d = jnp.zeros((bp, 32, D_WIDTH), F32)

    outs = {k: [] for k in ("nsa_p", "nsa_s", "win_p", "ca_p", "ca_s", "cd_p", "cd_s")}
    drows_p = drows_s = win_out_t = None
    for l in range(depth):
        lam_init = 0.8 - 0.6 * math.exp(-0.3 * l)
        mod_p = ada[l, :, 0:bp].reshape(N_ADA, bp, 1, d)
        mod_s = ada[l, :, bp:bp + bs].reshape(N_ADA, 1, bs, d)
        g = norm_g[l]
        w1u, w1d = ffn1_up[l].astype(BF16), ffn1_down[l].astype(BF16)
        w2u, w2d = ffn2_up[l].astype(BF16), ffn2_down[l].astype(BF16)
        wmix = _mix_weight(w_mix_in[l])
        gains = jnp.stack([_tile_gain(nsa_qk_g[l, 0]), _tile_gain(nsa_qk_g[l, 1]), _tile_gain(nsa_qk_g[l, 2]),
                           _tile_gain(nsa_qk_g[l, 3]), _tile_gain(diff_qk_g[l, 0]), _tile_gain(diff_qk_g[l, 1]),
                           jnp.zeros((LANES,), F32), jnp.zeros((LANES,), F32)])
        cw = nsa_cmp_w[l]
        w2 = jnp.concatenate([cw[0], cw[0], cw[1], cw[1]], axis=1)
        wt = jnp.tile(jnp.swapaxes(cw, 1, 2), (1, 1, page // NSA_BLOCK))
        wt = jnp.broadcast_to(wt[:, None], (2, NSA_KV_HEADS, HEAD_DIM, page)).reshape(4 * HEAD_DIM, page)
        wts = [a_out[l].astype(BF16), nsa_out[l].astype(BF16), diff_out[l].astype(BF16), d_out[l].astype(BF16),
               w_out[l].astype(BF16)]
        conv_w = (conv_a_w[l], conv_a_b[l][None], conv_d_w[l], conv_d_b[l][None], d_norm_g[l][None], d_norm_b[l][None])
        lam_p, og = diff_lambda[l], diff_out_g[l][None]

        xp = ffn(xp, mod_p, 0, g[0:1], w1u, w1d, tm_p, tpg_p)
        zp = mix_in(xp, mod_p, g[1:2], wmix, 1024, sp // 1024)
        qn, qr, rows_p, win_p, dq, drows_p, ksd, vsd, dkb, dvb = qk_prep(
            zp, cos_p, sin_p, gains, gmat, tm_p, tpg_p, l, depth, drows_p)
        ua, ud, ca, cd = conv_branches(zp, zeros_a, zeros_d, *conv_w, nseq=bp, tm=tm_p, stride=1)
        cmp_p = nsa_compress(rows_p, w2)
        o_cmp, sel = nsa_cmp_select(qn, cmp_p, bp)
        o_slc = nsa_slc(qr, ksd, vsd, sel, emat_p, bp)
        o_win = nsa_win(qr, win_p, bp)
        uc = diff_attn(dq, dkb, dvb, lam_p, og, bp, lam_init)
        xp = merge_out(xp, mod_p, zp, ua, uc, ud, o_cmp, o_slc, o_win, *wts, tm=tm_p, tiles_per_group=tpg_p)
        xp = ffn(xp, mod_p, 6, g[2:3], w2u, w2d, tm_p, tpg_p)
        outs["nsa_p"].append(rows_p.reshape(bp, sp, 4, NSA_KV_HEADS, HEAD_DIM))
        wkeep = min(NSA_WINDOW, sp)
        outs["win_p"].append(win_p.reshape(bp, sp, 2, NSA_KV_HEADS, HEAD_DIM)[:, sp - wkeep:])
        outs["ca_p"].append(ca)
        outs["cd_p"].append(cd)

        xs = ffn(xs, mod_s, 0, g[0:1], w1u, w1d, tm_s, 1)
        zs = mix_in(xs, mod_s, g[1:2], wmix, tm_s, 1)
        qn, qr, rows_s, win_s, dq, drows_s = qk_prep(zs, cos_s, sin_s, gains, gmat, tm_s, 1, l, depth, drows_s)[:6]
        past_a = jnp.swapaxes(state_conv_a[l], 0, 1).reshape(1, (A_CONV - 1) * bs, A_WIDTH)
        past_d = jnp.swapaxes(state_conv_d[l], 0, 1).reshape(1, (D_CONV - 1) * bs, D_WIDTH)
        ua, ud, ca, cd = conv_branches(zs, past_a, past_d, *conv_w, nseq=1, tm=tm_s, stride=bs)
        rows_b = _to_seq_major(rows_s, bs, ns)
        new_t = _tokens_last(rows_s, bs, ns, (4, NSA_KV_HEADS, HEAD_DIM))
        win_new_t = _tokens_last(win_s, bs, ns, (2, NSA_KV_HEADS, HEAD_DIM))
        o_cmp, o_slc, o_win, win_out_t = nsa_sample(
            l, page_table, cache_nsa_t, new_t, _to_seq_major(qn, bs, ns, QPAD), _to_seq_major(qr, bs, ns, QPAD),
            win_state_t, win_new_t, wt, selm_s, ns, win_out_t)
        drows_b = _to_seq_major(drows_s[l].reshape(ns * bs, 2 * DIFF_HEADS * LANES), bs, ns, QPAD)
        uc = diff_sample(l, page_table, cache_diff2, drows_b, _to_seq_major(dq, bs, ns, QPAD), lam_p, og, ns, lam_init)
        xs = merge_out(xs, mod_s, zs, ua, _to_step_major(uc, ns).astype(BF16), ud, _to_step_major(o_cmp, ns),
                       _to_step_major(o_slc, ns), _to_step_major(o_win, ns), *wts, tm=tm_s, tiles_per_group=1)
        xs = ffn(xs, mod_s, 6, g[2:3], w2u, w2d, tm_s, 1)
        outs["nsa_s"].append(rows_b.reshape(bs, ns, 4, NSA_KV_HEADS, HEAD_DIM))
        outs["ca_s"].append(jnp.swapaxes(ca.reshape(A_CONV - 1, bs, A_WIDTH), 0, 1))
        outs["cd_s"].append(jnp.swapaxes(cd.reshape(D_CONV - 1, bs, D_WIDTH), 0, 1))

    y_p = xp.reshape(bp, sp, d)
    y_s = jnp.swapaxes(xs.reshape(ns, bs, d), 0, 1)
    st = {k: jnp.stack(v, axis=0) for k, v in outs.items()}
    diff_p = drows_p.reshape(depth, bp, sp, 2, DIFF_HEADS, 2 * HEAD_DIM)
    diff_s = jnp.swapaxes(drows_s.reshape(depth, ns, bs, 2, DIFF_HEADS, 2 * HEAD_DIM), 1, 2)
    win_s = jnp.moveaxis(win_out_t, -1, 2)
    return (y_p, y_s, st["nsa_p"], st["nsa_s"], diff_p, diff_s, st["win_p"], win_s,
            st["ca_p"], st["ca_s"], st["cd_p"], st["cd_s"])
```

```python
import functools
import math

import jax
import jax.numpy as jnp
from jax import lax
from jax.experimental import pallas as pl
from jax.experimental.pallas import tpu as pltpu

F32 = jnp.float32
BF16 = jnp.bfloat16

HEAD_DIM = 64
ROPE_THETA = 10000.0
NORM_EPS = 1e-6
A_WIDTH = 512
A_CONV = 3
NSA_HEADS = 8
NSA_KV_HEADS = 2
NSA_GROUP = NSA_HEADS // NSA_KV_HEADS
NSA_BLOCK = 64
NSA_TOPK = 16
NSA_WINDOW = 512
DIFF_HEADS = 4
D_WIDTH = 512
D_CONV = 31
N_BRANCH = 4
N_ADA = 9
LANES = 128
NEG = -1e30
ATT_SCALE = HEAD_DIM ** -0.5
Q_SCALE = ATT_SCALE * math.log2(math.e)
VMEM_LIMIT = 56 * 1024 * 1024

C_KV = 0
C_MERGE = 1024
C_GLU = 5120
C_AB = 6144
C_AC = 6656
C_AX = 7168
C_NQ = 7680
C_DQ = 8192
C_DK = 8704
C_DV = 9216
C_NG = 768
MIX_W = 9728
MIX_TN = 2432


def _cparams(sem):
    return pltpu.CompilerParams(dimension_semantics=sem, vmem_limit_bytes=VMEM_LIMIT)


def _sigmoid(x):
    return 1.0 / (1.0 + jnp.exp(-x))


def _silu(x):
    return x * _sigmoid(x)


def _rows_to(m, tm):
    r = m.shape[0]
    if r == 1 or r == tm:
        return m
    return jnp.concatenate([m] * (tm // r), axis=0)


def _modulate(x, g, shift, scale):
    ms = jnp.mean(x * x, axis=-1, keepdims=True)
    y = x * lax.rsqrt(ms + NORM_EPS) * g
    return y * (1.0 + scale) + shift


def _dot(a, b):
    return jnp.dot(a, b, preferred_element_type=F32)


def _dot_nt(a, b, precision=None):
    return lax.dot_general(a, b, (((1,), (1,)), ((), ())), preferred_element_type=F32, precision=precision)


def _lane_lo(shape):
    return (lax.broadcasted_iota(jnp.int32, shape, len(shape) - 1) % LANES) < HEAD_DIM


def _dup_halves(x, lo):
    xr = pltpu.roll(x, HEAD_DIM, 1)
    return jnp.where(lo, x, xr), jnp.where(lo, xr, x)


def _ada_kernel(c_ref, w_ref, b_ref, o_ref):
    s = _silu(c_ref[...])
    o_ref[...] = _dot(s.astype(BF16), w_ref[...].astype(BF16)) + b_ref[...]


def ada_proj(c_all, ada_w, ada_b):
    r, d = c_all.shape
    nl = ada_w.shape[0]
    return pl.pallas_call(
        _ada_kernel,
        grid=(nl, N_ADA),
        in_specs=[
            pl.BlockSpec((r, d), lambda l, k: (0, 0)),
            pl.BlockSpec((None, d, d), lambda l, k: (l, 0, k)),
            pl.BlockSpec((None, None, 1, d), lambda l, k: (l, k, 0, 0)),
        ],
        out_specs=pl.BlockSpec((None, None, r, d), lambda l, k: (l, k, 0, 0)),
        out_shape=jax.ShapeDtypeStruct((nl, N_ADA, r, d), F32),
        compiler_params=_cparams(("arbitrary", "arbitrary")),
        name="ada_proj",
    )(c_all, ada_w, ada_b.reshape(nl, N_ADA, 1, d))


def _mod_spec(mod, k, tiles_per_group, ngrid):
    _, _, r, d = mod.shape
    if ngrid == 1:
        return pl.BlockSpec((None, None, r, d), lambda i: (k, i // tiles_per_group, 0, 0))
    return pl.BlockSpec((None, None, r, d), lambda i, j: (k, i // tiles_per_group, 0, 0))


def _ffn_kernel(x_ref, sh_ref, sc_ref, gt_ref, g_ref, wa_ref, wv_ref, wd_ref, o_ref, h_ref, acc_ref, *, nj):
    j = pl.program_id(1)
    tm = x_ref.shape[0]

    @pl.when(j == 0)
    def _():
        h = _modulate(x_ref[...], g_ref[...], _rows_to(sh_ref[...], tm), _rows_to(sc_ref[...], tm))
        h_ref[...] = h.astype(BF16)
        acc_ref[...] = jnp.zeros_like(acc_ref)

    h = h_ref[...]
    a = _dot(h, wa_ref[...])
    v = _dot(h, wv_ref[...])
    act = (_silu(a) * v).astype(BF16)
    acc_ref[...] += _dot(act, wd_ref[...])

    @pl.when(j == nj - 1)
    def _():
        o_ref[...] = x_ref[...] + 0.5 * _rows_to(gt_ref[...], tm) * acc_ref[...]


def ffn(x, mod, k0, g, w_up, w_down, tm, tiles_per_group):
    t, d = x.shape
    f = w_down.shape[0]
    nj = 2
    tf = f // nj
    return pl.pallas_call(
        functools.partial(_ffn_kernel, nj=nj),
        grid=(t // tm, nj),
        in_specs=[
            pl.BlockSpec((tm, d), lambda i, j: (i, 0)),
            _mod_spec(mod, k0, tiles_per_group, 2),
            _mod_spec(mod, k0 + 1, tiles_per_group, 2),
            _mod_spec(mod, k0 + 2, tiles_per_group, 2),
            pl.BlockSpec((1, d), lambda i, j: (0, 0)),
            pl.BlockSpec((d, tf), lambda i, j: (0, j)),
            pl.BlockSpec((d, tf), lambda i, j: (0, nj + j)),
            pl.BlockSpec((tf, d), lambda i, j: (j, 0)),
        ],
        out_specs=pl.BlockSpec((tm, d), lambda i, j: (i, 0)),
        out_shape=jax.ShapeDtypeStruct((t, d), F32),
        scratch_shapes=[pltpu.VMEM((tm, d), BF16), pltpu.VMEM((tm, d), F32)],
        compiler_params=_cparams(("arbitrary", "arbitrary")),
        name="ffn",
    )(x, mod, mod, mod, g, w_up, w_up, w_down)


def _mix_kernel(x_ref, sh_ref, sc_ref, g_ref, w_ref, o_ref, h_ref):
    tm = x_ref.shape[0]

    @pl.when(pl.program_id(1) == 0)
    def _():
        h = _modulate(x_ref[...], g_ref[...], _rows_to(sh_ref[...], tm), _rows_to(sc_ref[...], tm))
        h_ref[...] = h.astype(BF16)

    o_ref[...] = _dot(h_ref[...], w_ref[...])


def mix_in(x, mod, g, w_mix, tm, tiles_per_group, tn=MIX_TN):
    t, d = x.shape
    nw = w_mix.shape[1]
    return pl.pallas_call(
        _mix_kernel,
        grid=(t // tm, nw // tn),
        in_specs=[
            pl.BlockSpec((tm, d), lambda i, j: (i, 0)),
            _mod_spec(mod, 3, tiles_per_group, 2),
            _mod_spec(mod, 4, tiles_per_group, 2),
            pl.BlockSpec((1, d), lambda i, j: (0, 0)),
            pl.BlockSpec((d, tn), lambda i, j: (0, j)),
        ],
        out_specs=pl.BlockSpec((tm, tn), lambda i, j: (i, j)),
        out_shape=jax.ShapeDtypeStruct((t, nw), F32),
        scratch_shapes=[pltpu.VMEM((tm, d), BF16)],
        compiler_params=_cparams(("arbitrary", "arbitrary")),
        name="mix_in",
    )(x, mod, mod, g, w_mix)


def _group_norm(x, gain, gmat):
    x2 = x * x
    hi = x2.astype(BF16)
    lo = (x2 - hi.astype(F32)).astype(BF16)
    ss = _dot(hi, gmat) + _dot(lo, gmat)
    return x * lax.rsqrt(ss * (1.0 / HEAD_DIM) + NORM_EPS) * gain


def _rope(x, cos, sin_signed, lo):
    sw = jnp.where(lo, pltpu.roll(x, LANES - HEAD_DIM // 2, 1), pltpu.roll(x, HEAD_DIM // 2, 1))
    return x * cos + sw * sin_signed


def _prep_kernel(kv_ref, nq_ref, dq_ref, dk_ref, dv_ref, cos_ref, sin_ref, gain_ref, gmat_ref,
                 qn_ref, qr_ref, rows_ref, win_ref, dqo_ref, drows_ref, ksd_ref, vsd_ref, dkb_ref, dvb_ref):
    cos = cos_ref[...]
    sin = sin_ref[...]
    gmat = gmat_ref[...]
    lo32 = (lax.broadcasted_iota(jnp.int32, cos.shape, 1) % HEAD_DIM) < HEAD_DIM // 2
    lo64 = _lane_lo(cos.shape)

    def gain(r):
        return gain_ref[r:r + 1, :]

    def piece(ref, p):
        return ref[:, p * LANES:(p + 1) * LANES]

    for p in range(4):
        qn = _group_norm(piece(nq_ref, p), gain(0), gmat)
        qn_ref[:, p * LANES:(p + 1) * LANES] = qn
        qr_ref[:, p * LANES:(p + 1) * LANES] = (_rope(qn, cos, sin, lo32) * Q_SCALE).astype(BF16)
    rows_ref[:, 0:LANES] = _group_norm(piece(kv_ref, 0), gain(1), gmat)
    rows_ref[:, LANES:2 * LANES] = piece(kv_ref, 1)
    k_slc = _rope(_group_norm(piece(kv_ref, 2), gain(2), gmat), cos, sin, lo32)
    v_slc = piece(kv_ref, 3)
    rows_ref[:, 2 * LANES:3 * LANES] = k_slc
    rows_ref[:, 3 * LANES:4 * LANES] = v_slc
    for hk, (kd, vd) in enumerate(zip(_dup_halves(k_slc, lo64), _dup_halves(v_slc, lo64))):
        ksd_ref[:, hk * LANES:(hk + 1) * LANES] = kd.astype(BF16)
        vsd_ref[:, hk * LANES:(hk + 1) * LANES] = vd.astype(BF16)
    win_ref[:, 0:LANES] = _rope(_group_norm(piece(kv_ref, 4), gain(3), gmat), cos, sin, lo32)
    win_ref[:, LANES:2 * LANES] = piece(kv_ref, 5)
    for p in range(4):
        dq = _rope(_group_norm(piece(dq_ref, p), gain(4), gmat), cos, sin, lo32)
        dqo_ref[:, p * LANES:(p + 1) * LANES] = (dq * Q_SCALE).astype(BF16)
        dk = _rope(_group_norm(piece(dk_ref, p), gain(5), gmat), cos, sin, lo32)
        drows_ref[:, p, :] = dk
        dkb_ref[:, p * LANES:(p + 1) * LANES] = dk.astype(BF16)
        drows_ref[:, DIFF_HEADS + p, :] = piece(dv_ref, p)
    dvb_ref[...] = dv_ref[...].astype(BF16)


def qk_prep(z, cos, sin, gains, gmat, tm, tiles_per_seq, layer, n_layers, drows_prev):
    t = z.shape[0]

    def zs(width, col):
        return pl.BlockSpec((tm, width), lambda i: (i, col // width))

    def os_(width):
        return pl.BlockSpec((tm, width), lambda i: (i, 0))

    tab = pl.BlockSpec((tm, LANES), lambda i: (i % tiles_per_seq, 0))
    outs = ((512, F32), (512, BF16), (512, F32), (256, F32), (512, BF16), None,
            (256, BF16), (256, BF16), (512, BF16), (512, BF16))
    drows_shape = jax.ShapeDtypeStruct((n_layers, t, 2 * DIFF_HEADS, LANES), F32)
    drows_spec = pl.BlockSpec((None, tm, 2 * DIFF_HEADS, LANES), lambda i: (layer, i, 0, 0))
    in_specs = [zs(768, C_KV), zs(512, C_NQ), zs(512, C_DQ), zs(512, C_DK), zs(512, C_DV), tab, tab,
                pl.BlockSpec((8, LANES), lambda i: (0, 0)), pl.BlockSpec((LANES, LANES), lambda i: (0, 0))]
    args = [z, z, z, z, z, cos, sin, gains, gmat]
    kern, aliases = _prep_kernel, {}
    if drows_prev is not None:
        in_specs.append(pl.BlockSpec(memory_space=pl.ANY))
        args.append(drows_prev)
        aliases = {len(args) - 1: 5}
        kern = _prep_kernel_aliased
    return pl.pallas_call(
        kern,
        grid=(t // tm,),
        in_specs=in_specs,
        out_specs=[drows_spec if o is None else os_(o[0]) for o in outs],
        out_shape=[drows_shape if o is None else jax.ShapeDtypeStruct((t, o[0]), o[1]) for o in outs],
        input_output_aliases=aliases,
        compiler_params=_cparams(("arbitrary",)),
        name="qk_prep",
    )(*args)


def _prep_kernel_aliased(*refs):
    _prep_kernel(*refs[:9], *refs[10:])


def _conv_kernel(ab_ref, ac_ref, ax_ref, glu_ref, pa_ref, pd_ref, wa_ref, ba_ref, wd_ref, bd_ref, lg_ref, lb_ref,
                 ua_ref, ud_ref, sa_ref, sd_ref, bufa, bufd, *, tm, ha, hd, stride, nt):
    i = pl.program_id(1)

    @pl.when(i == 0)
    def _():
        bufa[0:ha, :] = pa_ref[...]
        bufd[0:hd, :] = pd_ref[...]

    if nt > 1:
        @pl.when(i > 0)
        def _():
            bufa[0:ha, :] = bufa[tm:tm + ha, :]
            bufd[0:hd, :] = bufd[tm:tm + hd, :]

    bufa[ha:ha + tm, :] = ac_ref[...] * ax_ref[...]
    ya = ba_ref[...]
    for k in range(A_CONV):
        ya = ya + wa_ref[k:k + 1, :] * bufa[pl.ds(ha - (A_CONV - 1 - k) * stride, tm), :]
    ua_ref[...] = (ab_ref[...] * ya).astype(ua_ref.dtype)

    glu = glu_ref[...]
    bufd[hd:hd + tm, :] = glu[:, :D_WIDTH] * _sigmoid(glu[:, D_WIDTH:])
    yd = bd_ref[...]
    for k in range(D_CONV):
        yd = yd + wd_ref[k:k + 1, :] * bufd[pl.ds(hd - (D_CONV - 1 - k) * stride, tm), :]
    mu = jnp.mean(yd, axis=-1, keepdims=True)
    var = jnp.mean(jnp.square(yd - mu), axis=-1, keepdims=True)
    y = (yd - mu) * lax.rsqrt(var + NORM_EPS) * lg_ref[...] + lb_ref[...]
    ud_ref[...] = _silu(y).astype(ud_ref.dtype)

    @pl.when(i == nt - 1)
    def _():
        sa_ref[...] = bufa[ha + tm - (A_CONV - 1) * stride:ha + tm, :]
        sd_ref[...] = bufd[hd + tm - (D_CONV - 1) * stride:hd + tm, :]


def conv_branches(z, past_a, past_d, wa, ba, wd, bd, lg, lb, nseq, tm, stride):
    t = z.shape[0]
    nt = t // nseq // tm
    ha, hd = past_a.shape[1], past_d.shape[1]
    if nt > 1:
        assert tm >= ha and tm >= hd

    def zs(width, col):
        return pl.BlockSpec((tm, width), lambda s, i: (s * nt + i, col // width))

    def cs(shape):
        return pl.BlockSpec(shape, lambda s, i: (0, 0))

    sa_rows, sd_rows = (A_CONV - 1) * stride, (D_CONV - 1) * stride
    return pl.pallas_call(
        functools.partial(_conv_kernel, tm=tm, ha=ha, hd=hd, stride=stride, nt=nt),
        grid=(nseq, nt),
        in_specs=[zs(512, C_AB), zs(512, C_AC), zs(512, C_AX), zs(1024, C_GLU),
                  pl.BlockSpec((None, ha, A_WIDTH), lambda s, i: (s, 0, 0)),
                  pl.BlockSpec((None, hd, D_WIDTH), lambda s, i: (s, 0, 0)),
                  cs((A_CONV, A_WIDTH)), cs((1, A_WIDTH)), cs((D_CONV, D_WIDTH)), cs((1, D_WIDTH)),
                  cs((1, D_WIDTH)), cs((1, D_WIDTH))],
        out_specs=[pl.BlockSpec((tm, A_WIDTH), lambda s, i: (s * nt + i, 0)),
                   pl.BlockSpec((tm, D_WIDTH), lambda s, i: (s * nt + i, 0)),
                   pl.BlockSpec((None, sa_rows, A_WIDTH), lambda s, i: (s, 0, 0)),
                   pl.BlockSpec((None, sd_rows, D_WIDTH), lambda s, i: (s, 0, 0))],
        out_shape=[jax.ShapeDtypeStruct((t, A_WIDTH), BF16), jax.ShapeDtypeStruct((t, D_WIDTH), BF16),
                   jax.ShapeDtypeStruct((nseq, sa_rows, A_WIDTH), F32),
                   jax.ShapeDtypeStruct((nseq, sd_rows, D_WIDTH), F32)],
        scratch_shapes=[pltpu.VMEM((ha + tm, A_WIDTH), F32), pltpu.VMEM((hd + tm, D_WIDTH), F32)],
        compiler_params=_cparams(("arbitrary", "arbitrary")),
        name="conv_branches",
    )(z, z, z, z, past_a, past_d, wa, ba, wd, bd, lg, lb)


def _merge_kernel(x_ref, gt_ref, ua_ref, uc_ref, ud_ref, oc_ref, os_ref, ow_ref, ng_ref, ex_ref,
                  m0_ref, m1_ref, m2_ref, m3_ref, wa_ref, wb_ref, wc_ref, wd_ref, wo_ref, o_ref):
    tm = x_ref.shape[0]
    g = _sigmoid(ng_ref[...])
    g_hi = g.astype(BF16)
    g_lo = (g - g_hi.astype(F32)).astype(BF16)
    gx = _dot(g_hi, ex_ref[...]) + _dot(g_lo, ex_ref[...])
    ub = (gx[:, 0:512] * oc_ref[...] + gx[:, 512:1024] * os_ref[...] + gx[:, 1024:1536] * ow_ref[...]).astype(BF16)
    merged = _sigmoid(m0_ref[...]) * _dot(ua_ref[...], wa_ref[...])
    merged += _sigmoid(m1_ref[...]) * _dot(ub, wb_ref[...])
    merged += _sigmoid(m2_ref[...]) * _dot(uc_ref[...], wc_ref[...])
    merged += _sigmoid(m3_ref[...]) * _dot(ud_ref[...], wd_ref[...])
    y = _dot(merged.astype(BF16), wo_ref[...])
    o_ref[...] = x_ref[...] + _rows_to(gt_ref[...], tm) * y


def merge_out(x, mod, z, ua, uc, ud, o_cmp, o_slc, o_win, gate_ex, w_a, w_b, w_c, w_d, w_o, tm, tiles_per_group):
    t, d = x.shape

    def ts(width):
        return pl.BlockSpec((tm, width), lambda i: (i, 0))

    def zs(width, col):
        return pl.BlockSpec((tm, width), lambda i: (i, col // width))

    def ws(shape):
        return pl.BlockSpec(shape, lambda i: (0, 0))

    return pl.pallas_call(
        _merge_kernel,
        grid=(t // tm,),
        in_specs=[ts(d), _mod_spec(mod, 5, tiles_per_group, 1), ts(512), ts(512), ts(512), ts(512), ts(512), ts(512),
                  zs(LANES, C_NG), ws(gate_ex.shape),
                  zs(1024, C_MERGE), zs(1024, C_MERGE + 1024), zs(1024, C_MERGE + 2048), zs(1024, C_MERGE + 3072),
                  ws((512, d)), ws((512, d)), ws((512, d)), ws((512, d)), ws((d, d))],
        out_specs=ts(d),
        out_shape=jax.ShapeDtypeStruct((t, d), F32),
        compiler_params=_cparams(("arbitrary",)),
        name="merge_out",
    )(x, mod, ua, uc, ud, o_cmp, o_slc, o_win, z, gate_ex, z, z, z, z, w_a, w_b, w_c, w_d, w_o)


def _compress_kernel(r_ref, w_ref, o_ref):
    x = r_ref[...]
    nb = x.shape[0] // NSA_BLOCK
    o_ref[...] = jnp.sum(x.reshape(nb, NSA_BLOCK, x.shape[1]) * w_ref[...][None], axis=1)


def nsa_compress(rows, w2, tm=512):
    t = rows.shape[0]
    return pl.pallas_call(
        _compress_kernel,
        grid=(t // tm,),
        in_specs=[pl.BlockSpec((tm, 256), lambda i: (i, 0)), pl.BlockSpec((NSA_BLOCK, 256), lambda i: (0, 0))],
        out_specs=pl.BlockSpec((tm // NSA_BLOCK, 256), lambda i: (i, 0)),
        out_shape=jax.ShapeDtypeStruct((t // NSA_BLOCK, 256), F32),
        compiler_params=_cparams(("arbitrary",)),
        name="nsa_compress",
    )(rows, w2)


def _masked_softmax(s, mask):
    s = jnp.where(mask, s, NEG)
    p = jnp.exp(s - jnp.max(s, axis=-1, keepdims=True))
    p = jnp.where(mask, p, 0.0)
    return p / jnp.maximum(jnp.sum(p, axis=-1, keepdims=True), 1e-30)


def _top_blocks(imp, n_sel):
    imp = imp.T
    idx = lax.broadcasted_iota(jnp.int32, imp.shape, 0)
    big = imp.shape[0]
    sel = jnp.zeros(imp.shape, F32)
    for _ in range(n_sel):
        m = jnp.max(imp, axis=0, keepdims=True)
        first = jnp.min(jnp.where(imp == m, idx, big), axis=0, keepdims=True)
        hit = idx == first
        sel = jnp.where(hit, 1.0, sel)
        imp = jnp.where(hit, -3.0, imp)
    return sel.T


def _top_blocks_by_rank(imp, n_sel):
    nb = imp.shape[1]
    idx = lax.broadcasted_iota(jnp.int32, imp.shape, 1)
    rank = jnp.zeros(imp.shape, F32)
    for d in range(1, nb):
        other = pltpu.roll(imp, d, 1)
        beats = (other > imp) | ((other == imp) & (idx >= d))
        rank = rank + jnp.where(beats, 1.0, 0.0)
    return jnp.where(rank < n_sel, 1.0, 0.0)


def _cmp_select(q, kdup, vdup, qpos, nb, n_sel):
    tq = q.shape[0]
    qlo = _lane_lo((tq, LANES))
    blk = lax.broadcasted_iota(jnp.int32, (1, nb), 1)
    complete = (blk + 1) * NSA_BLOCK - 1 <= qpos
    cur = qpos // NSA_BLOCK
    outs, sels = [], []
    for hk in range(NSA_KV_HEADS):
        imp = jnp.zeros((tq, nb), F32)
        vd = vdup[hk].astype(BF16)
        for a in (2 * hk, 2 * hk + 1):
            qp = q[:, a * LANES:(a + 1) * LANES]
            pair = []
            for par in range(2):
                qm = jnp.where(qlo if par == 0 else ~qlo, qp, 0.0)
                s = _dot_nt(qm, kdup[hk], precision=lax.Precision.HIGHEST)
                pc = _masked_softmax(s * ATT_SCALE, complete)
                imp = imp + pc
                pair.append(_dot(pc.astype(BF16), vd))
            outs.append(jnp.where(qlo, pair[0], pair[1]))
        forced = (blk == cur) | (blk == 0)
        imp = jnp.where(forced, NSA_GROUP + 1.0, imp)
        imp = jnp.where(blk <= cur, imp, -1.0)
        sels.append(_top_blocks(imp, n_sel))
    return jnp.concatenate(outs, axis=1), sels


def _cmpsel_kernel(q_ref, c_ref, o_ref, sel_ref, *, tq, n_sel):
    qpos = pl.program_id(1) * tq + lax.broadcasted_iota(jnp.int32, (tq, 1), 0)
    nb = c_ref.shape[0]
    lo = _lane_lo((nb, LANES))
    kdup = _dup_halves(c_ref[:, 0:LANES], lo)
    vdup = _dup_halves(c_ref[:, LANES:2 * LANES], lo)
    o, sels = _cmp_select(q_ref[...], kdup, vdup, qpos, nb, n_sel)
    o_ref[...] = o
    sel_ref[...] = jnp.concatenate(sels, axis=1).astype(sel_ref.dtype)


def nsa_cmp_select(q_n, cmp, nseq, tq=512):
    t = q_n.shape[0]
    nq = t // nseq // tq
    nb = cmp.shape[0] // nseq
    return pl.pallas_call(
        functools.partial(_cmpsel_kernel, tq=tq, n_sel=min(NSA_TOPK, nb)),
        grid=(nseq, nq),
        in_specs=[pl.BlockSpec((tq, 512), lambda b, i: (b * nq + i, 0)),
                  pl.BlockSpec((nb, 256), lambda b, i: (b, 0))],
        out_specs=[pl.BlockSpec((tq, 512), lambda b, i: (b * nq + i, 0)),
                   pl.BlockSpec((tq, 2 * nb), lambda b, i: (b * nq + i, 0))],
        out_shape=[jax.ShapeDtypeStruct((t, 512), F32), jax.ShapeDtypeStruct((t, 2 * nb), BF16)],
        compiler_params=_cparams(("arbitrary", "arbitrary")),
        name="nsa_cmp_select",
    )(q_n, cmp)


def _online_step(s, m, acc, v_ext, exp_dtype=F32):
    tk = s.shape[1]
    m_new = jnp.maximum(m, jnp.max(s, axis=-1, keepdims=True))
    x = s - jnp.concatenate([m_new] * (tk // LANES), axis=1)
    p = jnp.exp2(x.astype(exp_dtype)).astype(BF16)
    alpha = jnp.exp2(m - m_new)
    if acc.shape[1] != LANES:
        alpha = jnp.concatenate([alpha] * (acc.shape[1] // LANES), axis=1)
    return m_new, alpha * acc + _dot(p, v_ext)


GROWTH_LIMIT = 64.0


def _single_pass_step(s, ref, acc, growth, v_ext):
    tk = s.shape[1]
    p = jnp.exp2(s - jnp.concatenate([ref] * (tk // LANES), axis=1)).astype(BF16)
    row_max = jnp.max(s, axis=-1, keepdims=True)
    growth = jnp.maximum(growth, row_max - ref)
    ref_new = jnp.maximum(ref, row_max)
    alpha = jnp.exp2(ref - ref_new)
    if acc.shape[1] != LANES:
        alpha = jnp.concatenate([alpha] * (acc.shape[1] // LANES), axis=1)
    return ref_new, (acc + _dot(p, v_ext)) * alpha, growth


def _lane_bcast0(x):
    return jnp.broadcast_to(x[:, 0:1], x.shape)


def _slc_kernel(q_ref, k_ref, v_ref, sel_ref, e_ref, o_ref, *, tq, tk, nb):
    qi = pl.program_id(1)
    n_full = (qi * tq) // tk
    rows = NSA_GROUP * tq
    qlo = _lane_lo((tq, LANES))
    klo = _lane_lo((tk, LANES))
    zero = jnp.zeros((tq, LANES), BF16)
    qpos = qi * tq + lax.broadcasted_iota(jnp.int32, (tq, 1), 0)
    col = lax.broadcasted_iota(jnp.int32, (1, tk), 1)
    for hk in range(NSA_KV_HEADS):
        parts = []
        for c in range(NSA_GROUP):
            h = hk * NSA_GROUP + c
            qp = q_ref[:, (h // 2) * LANES:(h // 2 + 1) * LANES]
            parts.append(jnp.where(qlo, qp, zero) if h % 2 == 0 else jnp.where(qlo, zero, qp))
        qst = jnp.concatenate(parts, axis=0)
        sel = sel_ref[:, hk * nb:(hk + 1) * nb]

        def tile(j, diag, qst=qst, sel=sel, hk=hk):
            off = pl.multiple_of(j * tk, tk)
            k = k_ref[pl.ds(off, tk), hk * LANES:(hk + 1) * LANES]
            v = v_ref[pl.ds(off, tk), hk * LANES:(hk + 1) * LANES]
            bias = (_dot(sel, e_ref[j]) - 1.0) * (-NEG)
            if diag:
                bias = jnp.where(j * tk + col <= qpos, bias, NEG)
            s = _dot_nt(qst, k) + jnp.concatenate([bias] * NSA_GROUP, axis=0)
            return s, jnp.where(klo, v, jnp.ones_like(v))

        def finish(acc, hk=hk):
            o = acc / jnp.maximum(pltpu.roll(acc, HEAD_DIM, 1), 1e-30)
            for pair in range(NSA_GROUP // 2):
                a = hk * (NSA_GROUP // 2) + pair
                o_even = o[(2 * pair) * tq:(2 * pair + 1) * tq]
                o_odd = o[(2 * pair + 1) * tq:(2 * pair + 2) * tq]
                o_ref[:, a * LANES:(a + 1) * LANES] = jnp.where(qlo, o_even, pltpu.roll(o_odd, HEAD_DIM, 1))

        def body(j, carry, tile=tile):
            s, vx = tile(j, False)
            return _online_step(s, *carry, vx, exp_dtype=BF16)

        init = (jnp.full((rows, LANES), NEG, F32), jnp.zeros((rows, LANES), F32))
        m, acc = lax.fori_loop(0, n_full, body, init)
        s, vx = tile(n_full, True)
        finish(_online_step(s, m, acc, vx, exp_dtype=BF16)[1])


def nsa_slc(q_rot, ksd, vsd, sel, emat, nseq, tq=256):
    t = q_rot.shape[0]
    s = t // nseq
    nq = s // tq
    nk, nb, tk = emat.shape
    return pl.pallas_call(
        functools.partial(_slc_kernel, tq=tq, tk=tk, nb=nb),
        grid=(nseq, nq),
        in_specs=[pl.BlockSpec((tq, 512), lambda b, i: (b * nq + i, 0)),
                  pl.BlockSpec((s, 256), lambda b, i: (b, 0)),
                  pl.BlockSpec((s, 256), lambda b, i: (b, 0)),
                  pl.BlockSpec((tq, 2 * nb), lambda b, i: (b * nq + i, 0)),
                  pl.BlockSpec((nk, nb, tk), lambda b, i: (0, 0, 0))],
        out_specs=pl.BlockSpec((tq, 512), lambda b, i: (b * nq + i, 0)),
        out_shape=jax.ShapeDtypeStruct((t, 512), F32),
        compiler_params=_cparams(("arbitrary", "arbitrary")),
        name="nsa_slc",
    )(q_rot, ksd, vsd, sel, emat)


def _win_kernel(q_ref, wp_ref, wc_ref, o_ref, *, tq):
    i = pl.program_id(1)
    lo = _lane_lo((tq, LANES))
    qpos = i * tq + lax.broadcasted_iota(jnp.int32, (tq, 1), 0)
    chunks = []
    for off, ref in ((-tq, wp_ref), (0, wc_ref)):
        kpos = i * tq + off + lax.broadcasted_iota(jnp.int32, (1, tq), 1)
        mask = (kpos >= 0) & (kpos <= qpos) & (qpos - kpos <= NSA_WINDOW)
        chunks.append((mask, _dup_halves(ref[:, 0:LANES], lo), _dup_halves(ref[:, LANES:2 * LANES], lo)))
    zero = jnp.zeros((tq, LANES), BF16)
    for a in range(NSA_HEADS // 2):
        pair = []
        for par in range(2):
            hk = (2 * a + par) // NSA_GROUP
            qp = q_ref[:, a * LANES:(a + 1) * LANES]
            qm = jnp.where(lo, qp, zero) if par == 0 else jnp.where(lo, zero, qp)
            ss = [jnp.where(mask, _dot_nt(qm, kd[hk].astype(BF16)), NEG) for mask, kd, _ in chunks]
            m = jnp.maximum(jnp.max(ss[0], axis=-1, keepdims=True), jnp.max(ss[1], axis=-1, keepdims=True))
            ps = [jnp.where(ch[0], jnp.exp2(s - m), 0.0) for s, ch in zip(ss, chunks)]
            l = jnp.sum(ps[0], axis=-1, keepdims=True) + jnp.sum(ps[1], axis=-1, keepdims=True)
            inv = 1.0 / jnp.maximum(l, 1e-30)
            o = _dot((ps[0] * inv).astype(BF16), chunks[0][2][hk].astype(BF16))
            o += _dot((ps[1] * inv).astype(BF16), chunks[1][2][hk].astype(BF16))
            pair.append(o)
        o_ref[:, a * LANES:(a + 1) * LANES] = jnp.where(lo, pair[0], pair[1])


def nsa_win(q_rot, win_rows, nseq, tq=NSA_WINDOW):
    t = q_rot.shape[0]
    nq = t // nseq // tq
    return pl.pallas_call(
        functools.partial(_win_kernel, tq=tq),
        grid=(nseq, nq),
        in_specs=[pl.BlockSpec((tq, 512), lambda b, i: (b * nq + i, 0)),
                  pl.BlockSpec((tq, 256), lambda b, i: (b * nq + jnp.maximum(i - 1, 0), 0)),
                  pl.BlockSpec((tq, 256), lambda b, i: (b * nq + i, 0))],
        out_specs=pl.BlockSpec((tq, 512), lambda b, i: (b * nq + i, 0)),
        out_shape=jax.ShapeDtypeStruct((t, 512), F32),
        compiler_params=_cparams(("arbitrary", "arbitrary")),
        name="nsa_win",
    )(q_rot, win_rows, win_rows)


def _diff_lambda(lam_ref, lam_init):
    lp = lam_ref[...]
    a = jnp.sum(lp[0:1] * lp[1:2], axis=-1, keepdims=True)
    b = jnp.sum(lp[2:3] * lp[3:4], axis=-1, keepdims=True)
    return jnp.exp(a) - jnp.exp(b) + lam_init


def _diff_finish(o1, o2, lam, g, lam_init):
    o = o1 - lam * o2
    ms = jnp.mean(o * o, axis=-1, keepdims=True)
    return o * lax.rsqrt(ms + NORM_EPS) * g * (1.0 - lam_init)


def _diff_kernel(q_ref, k_ref, v_ref, lam_ref, g_ref, o_ref, *, tq, tk, lam_init):
    qi = pl.program_id(2)
    n_full = (qi * tq) // tk
    lo = _lane_lo((tq, LANES))
    q = q_ref[...]
    zero = jnp.zeros_like(q)
    qst = jnp.concatenate([jnp.where(lo, q, zero), jnp.where(lo, zero, q)], axis=0)
    qpos = qi * tq + lax.broadcasted_iota(jnp.int32, (tq, 1), 0)
    col = lax.broadcasted_iota(jnp.int32, (1, tk), 1)
    bias = jnp.where(n_full * tk + col <= qpos, 0.0, NEG)
    bias = jnp.concatenate([bias, bias], axis=0)
    ones = jnp.ones((tk, LANES), BF16)

    def tile(j):
        off = pl.multiple_of(j * tk, tk)
        return _dot_nt(qst, k_ref[pl.ds(off, tk), :]), jnp.concatenate([v_ref[pl.ds(off, tk), :], ones], axis=1)

    def walk(step, init):
        def full(j, carry):
            s, vx = tile(j)
            return step(s, carry, vx)

        carry = lax.fori_loop(0, n_full, full, init)
        s, vx = tile(n_full)
        return step(s + bias, carry, vx)

    def finish(acc):
        o = acc[:, 0:LANES] / jnp.maximum(acc[:, LANES:2 * LANES], 1e-30)
        lam = _diff_lambda(lam_ref, lam_init)
        o_ref[...] = _diff_finish(o[0:tq], o[tq:2 * tq], lam, g_ref[...], lam_init).astype(o_ref.dtype)

    ref0 = _lane_bcast0(_dot_nt(qst, k_ref[0:LANES, :]))
    init = (ref0, jnp.zeros((2 * tq, 2 * LANES), F32), jnp.zeros((2 * tq, LANES), F32))
    _, acc, growth = walk(lambda s, carry, vx: _single_pass_step(s, *carry, vx), init)
    fast_ok = jnp.max(growth) <= GROWTH_LIMIT

    @pl.when(fast_ok)
    def _():
        finish(acc)

    @pl.when(jnp.logical_not(fast_ok))
    def _():
        init = (jnp.full((2 * tq, LANES), NEG, F32), jnp.zeros((2 * tq, 2 * LANES), F32))
        finish(walk(lambda s, carry, vx: _online_step(s, *carry, vx), init)[1])


def diff_attn(dq, dk, dv, lam_p, out_g, nseq, lam_init, tq=256, tk=1024):
    t = dq.shape[0]
    s = t // nseq
    nq = s // tq
    return pl.pallas_call(
        functools.partial(_diff_kernel, tq=tq, tk=tk, lam_init=lam_init),
        grid=(nseq, DIFF_HEADS, nq),
        in_specs=[pl.BlockSpec((tq, LANES), lambda b, h, i: (b * nq + i, h)),
                  pl.BlockSpec((s, LANES), lambda b, h, i: (b, h)),
                  pl.BlockSpec((s, LANES), lambda b, h, i: (b, h)),
                  pl.BlockSpec((4, HEAD_DIM), lambda b, h, i: (0, 0)),
                  pl.BlockSpec((1, LANES), lambda b, h, i: (0, 0))],
        out_specs=pl.BlockSpec((tq, LANES), lambda b, h, i: (b * nq + i, h)),
        out_shape=jax.ShapeDtypeStruct((t, 512), BF16),
        compiler_params=_cparams(("arbitrary", "arbitrary", "arbitrary")),
        name="diff_attn",
    )(dq, dk, dv, lam_p, out_g)


QPAD = 8
SEQS_PER_STEP = 2


def _pad_keys(x):
    return jnp.concatenate([x, jnp.zeros((LANES - x.shape[0], x.shape[1]), x.dtype)], axis=0)


def _nsa_sample_kernel(pt_ref, *refs, n_pages, past_len, n_new, n_sel, n_seq):
    pages = refs[:n_seq * n_pages]
    new_ref, qn_ref, qr_ref, wst_ref, wnew_ref, wt_ref, selm_ref = refs[n_seq * n_pages:n_seq * n_pages + 7]
    oc_ref, os_ref, ow_ref, wout_ref = refs[-4:]
    for s in range(n_seq):
        chunks = tuple(pages[s * n_pages:(s + 1) * n_pages]) + (new_ref.at[s],)
        _nsa_sample_one(chunks, qn_ref.at[s], qr_ref.at[s], wst_ref.at[s], wnew_ref.at[s], wt_ref, selm_ref,
                        oc_ref.at[s], os_ref.at[s], ow_ref.at[s], wout_ref.at[s],
                        n_pages=n_pages, past_len=past_len, n_new=n_new, n_sel=n_sel)


def _nsa_sample_one(chunks, qn_ref, qr_ref, wst_ref, wnew_ref, wt_ref, selm_ref, oc_ref, os_ref, ow_ref, wout_ref,
                    *, n_pages, past_len, n_new, n_sel):
    n_real = (past_len + n_new + NSA_BLOCK - 1) // NSA_BLOCK
    nbp = selm_ref.shape[1]
    page = chunks[0].shape[3]

    def twice(x):
        return jnp.concatenate([x, x], axis=0)

    wt = wt_ref[...]
    y = jnp.concatenate([ch[0:2].reshape(4 * HEAD_DIM, page) * wt for ch in chunks], axis=1)
    hi = y.astype(BF16)
    selm = selm_ref[...]
    cmp_t = _dot(hi, selm)
    yk = y[0:2 * HEAD_DIM]
    cmp_k = cmp_t[0:2 * HEAD_DIM] + _dot((yk - hi[0:2 * HEAD_DIM].astype(F32)).astype(BF16), selm)

    tok = lax.broadcasted_iota(jnp.int32, (QPAD, 1), 0)
    qpos = past_len + tok
    rows = NSA_GROUP * QPAD
    qpos_st = jnp.concatenate([qpos] * NSA_GROUP, axis=0)
    qlo = _lane_lo((QPAD, LANES))

    def stack_group(q_ref, hk, zero):
        parts = []
        for c in range(NSA_GROUP):
            h = hk * NSA_GROUP + c
            qp = q_ref[:, (h // 2) * LANES:(h // 2 + 1) * LANES]
            parts.append(jnp.where(qlo, qp, zero) if h % 2 == 0 else jnp.where(qlo, zero, qp))
        return jnp.concatenate(parts, axis=0)

    def emit(o_ref, per_hk):
        for a in range(NSA_HEADS // 2):
            hk, c = (2 * a) // NSA_GROUP, (2 * a) % NSA_GROUP
            o = per_hk[hk]
            o_ref[:, a * LANES:(a + 1) * LANES] = jnp.where(qlo, o[c * QPAD:(c + 1) * QPAD], o[(c + 1) * QPAD:(c + 2) * QPAD])

    blk = lax.broadcasted_iota(jnp.int32, (1, nbp), 1)
    complete = ((blk + 1) * NSA_BLOCK - 1 <= qpos_st) & (blk < n_real)
    cur = qpos // NSA_BLOCK
    per_hk, imps = [], []
    for hk in range(NSA_KV_HEADS):
        kd = twice(cmp_k[hk * HEAD_DIM:(hk + 1) * HEAD_DIM])
        vd = twice(cmp_t[(2 + hk) * HEAD_DIM:(3 + hk) * HEAD_DIM]).astype(BF16)
        s = jnp.dot(stack_group(qn_ref, hk, jnp.zeros((QPAD, LANES), F32)), kd,
                    precision=lax.Precision.HIGHEST, preferred_element_type=F32)
        pc = _masked_softmax(s * ATT_SCALE, complete)
        per_hk.append(_dot_nt(pc.astype(BF16), vd))
        imp = functools.reduce(lambda a, b: a + b, [pc[c * QPAD:(c + 1) * QPAD] for c in range(NSA_GROUP)])
        imp = jnp.where((blk == cur) | (blk == 0), NSA_GROUP + 1.0, imp)
        imp = jnp.where(blk <= cur, imp, -1.0)
        imps.append(jnp.where(blk < n_real, imp, -2.0))
    emit(oc_ref, per_hk)
    sel_all = _top_blocks_by_rank(jnp.concatenate(imps, axis=0), n_sel)
    sels = [sel_all[hk * QPAD:(hk + 1) * QPAD] for hk in range(NSA_KV_HEADS)]

    qst = [stack_group(qr_ref, hk, jnp.zeros((QPAD, LANES), BF16)) for hk in range(NSA_KV_HEADS)]
    key_new = lax.broadcasted_iota(jnp.int32, (1, LANES), 1)
    new_valid = (key_new < n_new) & (past_len + key_new <= qpos_st)

    def attend(kvs, biases, hk):
        k_all = jnp.concatenate([twice(k).astype(BF16) for k, _ in kvs], axis=1)
        v_all = jnp.concatenate([twice(v).astype(BF16) for _, v in kvs], axis=1)
        s = _dot(qst[hk], k_all) + jnp.concatenate(biases, axis=1)
        p = jnp.exp2(s - jnp.max(s, axis=-1, keepdims=True))
        p = p / jnp.maximum(jnp.sum(p, axis=-1, keepdims=True), 1e-30)
        return _dot_nt(p.astype(BF16), v_all)

    per_hk = []
    for hk in range(NSA_KV_HEADS):
        selx = _dot_nt(sels[hk].astype(BF16), selm)
        selx = jnp.concatenate([selx] * NSA_GROUP, axis=0)
        kvs, biases = [], []
        for j, ch in enumerate(chunks):
            kvs.append((ch[2, hk], ch[3, hk]))
            bias = (selx[:, j * page:(j + 1) * page] - 1.0) * (-NEG)
            biases.append(jnp.where(new_valid, bias, NEG) if j == n_pages else bias)
        per_hk.append(attend(kvs, biases, hk))
    emit(os_ref, per_hk)

    wbuf = wst_ref.shape[3]
    kpos = past_len - wbuf + lax.broadcasted_iota(jnp.int32, (1, wbuf), 1)
    bias_st = jnp.where((kpos >= 0) & (kpos <= qpos_st) & (qpos_st - kpos <= NSA_WINDOW), 0.0, NEG)
    bias_new = jnp.where(new_valid & (qpos_st - (past_len + key_new) <= NSA_WINDOW), 0.0, NEG)
    per_hk = []
    for hk in range(NSA_KV_HEADS):
        kvs = [(wst_ref[0, hk], wst_ref[1, hk]), (wnew_ref[0, hk], wnew_ref[1, hk])]
        per_hk.append(attend(kvs, [bias_st, bias_new], hk))
    emit(ow_ref, per_hk)

    lane = lax.broadcasted_iota(jnp.int32, (HEAD_DIM, wbuf), 1)
    for r in range(2):
        for hk in range(NSA_KV_HEADS):
            shifted = pltpu.roll(wst_ref[r, hk], wbuf - n_new, 1)
            tail = jnp.concatenate([jnp.zeros((HEAD_DIM, wbuf - LANES), F32),
                                    pltpu.roll(wnew_ref[r, hk], LANES - n_new, 1)], axis=1)
            wout_ref[r, hk] = jnp.where(lane >= wbuf - n_new, tail, shifted)


def nsa_sample(layer, page_table, cache_t, new_t, q_n, q_rot, win_t, win_new_t, wt, selm, n_new, wout_prev):
    nb_, n_pages = page_table.shape
    page = cache_t.shape[5]
    past_len = n_pages * page
    n_real = (past_len + n_new + NSA_BLOCK - 1) // NSA_BLOCK
    wbuf = win_t.shape[5]
    kv = (NSA_KV_HEADS, HEAD_DIM)

    n_seq = SEQS_PER_STEP

    def page_spec(s, j):
        return pl.BlockSpec((None, None, 4) + kv + (page,), lambda b, pt: (layer, pt[b * n_seq + s, j], 0, 0, 0, 0))

    def seq(shape):
        return pl.BlockSpec((n_seq,) + shape, lambda b, pt: (b,) + (0,) * len(shape))

    def const(shape):
        return pl.BlockSpec(shape, lambda b, pt: (0,) * len(shape))

    win_spec = pl.BlockSpec((None, n_seq, 2) + kv + (wbuf,), lambda b, pt: (layer, b, 0, 0, 0, 0))
    in_specs = [page_spec(s, j) for s in range(n_seq) for j in range(n_pages)] + [
        seq((4,) + kv + (page,)), seq((QPAD, 512)), seq((QPAD, 512)), win_spec,
        seq((2,) + kv + (LANES,)), const(wt.shape), const(selm.shape)]
    args = [page_table] + [cache_t] * (n_seq * n_pages) + [new_t, q_n, q_rot, win_t, win_new_t, wt, selm]
    aliases = {}
    if wout_prev is not None:
        in_specs.append(pl.BlockSpec(memory_space=pl.ANY))
        args.append(wout_prev)
        aliases = {len(args) - 1: 3}
    grid_spec = pltpu.PrefetchScalarGridSpec(
        num_scalar_prefetch=1,
        grid=(nb_ // n_seq,),
        in_specs=in_specs,
        out_specs=[seq((QPAD, 512)), seq((QPAD, 512)), seq((QPAD, 512)), win_spec],
    )
    return pl.pallas_call(
        functools.partial(_nsa_sample_kernel, n_pages=n_pages, past_len=past_len, n_new=n_new,
                          n_sel=min(NSA_TOPK, n_real), n_seq=n_seq),
        grid_spec=grid_spec,
        out_shape=[jax.ShapeDtypeStruct((nb_, QPAD, 512), F32)] * 3 + [jax.ShapeDtypeStruct(win_t.shape, F32)],
        input_output_aliases=aliases,
        compiler_params=_cparams(("arbitrary",)),
        name="nsa_sample",
    )(*args)


def _diff_sample_kernel(pt_ref, *refs, n_pages, past_len, n_new, lam_init, n_seq):
    new_ref, q_ref, lam_ref, g_ref, o_ref = refs[n_seq * n_pages:]
    for s in range(n_seq):
        _diff_sample_one(refs[s * n_pages:(s + 1) * n_pages], new_ref.at[s], q_ref.at[s], lam_ref, g_ref, o_ref.at[s],
                         n_pages=n_pages, n_new=n_new, lam_init=lam_init)


def _diff_sample_one(pages, new_ref, q_ref, lam_ref, g_ref, o_ref, *, n_pages, n_new, lam_init):
    lo = _lane_lo((QPAD, LANES))
    tok = lax.broadcasted_iota(jnp.int32, (2 * QPAD, 1), 0) % QPAD
    key_new = lax.broadcasted_iota(jnp.int32, (1, LANES), 1)
    new_valid = (key_new < n_new) & (key_new <= tok)
    lam = _diff_lambda(lam_ref, lam_init)
    q = q_ref[...]
    new = new_ref[...]
    zero = jnp.zeros((QPAD, LANES), BF16)
    page = pages[0].shape[0] // (2 * DIFF_HEADS)
    bias = jnp.concatenate([jnp.zeros((2 * QPAD, n_pages * page), F32), jnp.where(new_valid, 0.0, NEG)], axis=1)
    for h in range(DIFF_HEADS):
        qp = q[:, h * LANES:(h + 1) * LANES]
        qst = jnp.concatenate([jnp.where(lo, qp, zero), jnp.where(lo, zero, qp)], axis=0)
        ks = [pages[j][pl.ds(h, page, stride=2 * DIFF_HEADS), :] for j in range(n_pages)]
        ks.append(_pad_keys(new[:, h * LANES:(h + 1) * LANES]))
        vs = [pages[j][pl.ds(DIFF_HEADS + h, page, stride=2 * DIFF_HEADS), :] for j in range(n_pages)]
        vs.append(_pad_keys(new[:, (DIFF_HEADS + h) * LANES:(DIFF_HEADS + h + 1) * LANES]))
        k_all = jnp.concatenate([k.astype(BF16) for k in ks], axis=0)
        v_all = jnp.concatenate([v.astype(BF16) for v in vs], axis=0)
        s = _dot_nt(qst, k_all) + bias
        p = jnp.exp2(s - jnp.max(s, axis=-1, keepdims=True))
        p = p / jnp.maximum(jnp.sum(p, axis=-1, keepdims=True), 1e-30)
        o = _dot((p[0:QPAD] - lam * p[QPAD:2 * QPAD]).astype(BF16), v_all)
        ms = jnp.mean(o * o, axis=-1, keepdims=True)
        o_ref[:, h * LANES:(h + 1) * LANES] = (o * lax.rsqrt(ms + NORM_EPS) * g_ref[...] * (1.0 - lam_init)).astype(o_ref.dtype)


def diff_sample(layer, page_table, cache, new_rows, dq, lam_p, out_g, n_new, lam_init):
    nb_, n_pages = page_table.shape
    page = cache.shape[2] // (2 * DIFF_HEADS)

    n_seq = SEQS_PER_STEP

    def page_spec(s, j):
        return pl.BlockSpec((None, None, page * 2 * DIFF_HEADS, LANES),
                            lambda b, pt: (layer, pt[b * n_seq + s, j], 0, 0))

    grid_spec = pltpu.PrefetchScalarGridSpec(
        num_scalar_prefetch=1,
        grid=(nb_ // n_seq,),
        in_specs=[page_spec(s, j) for s in range(n_seq) for j in range(n_pages)] + [
            pl.BlockSpec((n_seq, QPAD, 1024), lambda b, pt: (b, 0, 0)),
            pl.BlockSpec((n_seq, QPAD, 512), lambda b, pt: (b, 0, 0)),
            pl.BlockSpec((4, HEAD_DIM), lambda b, pt: (0, 0)),
            pl.BlockSpec((1, LANES), lambda b, pt: (0, 0))],
        out_specs=pl.BlockSpec((n_seq, QPAD, 512), lambda b, pt: (b, 0, 0)),
    )
    return pl.pallas_call(
        functools.partial(_diff_sample_kernel, n_pages=n_pages, past_len=n_pages * page, n_new=n_new,
                          lam_init=lam_init, n_seq=n_seq),
        grid_spec=grid_spec,
        out_shape=jax.ShapeDtypeStruct((nb_, QPAD, 512), F32),
        compiler_params=_cparams(("arbitrary",)),
        name="diff_sample",
    )(page_table, *([cache] * (n_seq * n_pages)), new_rows, dq, lam_p, out_g)


def _mix_weight(w):
    d = w.shape[0]
    splits = (A_WIDTH, A_WIDTH, A_WIDTH, 512, 768, 3 * NSA_HEADS, 512, 512, 512, 2 * D_WIDTH, N_BRANCH * d)
    offs = [0]
    for s in splits:
        offs.append(offs[-1] + s)
    part = {n: w[:, offs[i]:offs[i + 1]] for i, n in enumerate(
        ("ab", "ac", "ax", "nq", "kv", "g", "dq", "dk", "dv", "glu", "merge"))}
    pad = jnp.zeros((d, C_MERGE - C_NG - 3 * NSA_HEADS), w.dtype)
    cols = [part["kv"], part["g"], pad, part["merge"], part["glu"], part["ab"], part["ac"], part["ax"], part["nq"],
            part["dq"], part["dk"], part["dv"]]
    out = jnp.concatenate(cols, axis=1).astype(BF16)
    assert out.shape[1] == MIX_W
    return out


def _gate_expansion():
    r = jnp.arange(LANES)[:, None]
    c = jnp.arange(3 * NSA_HEADS * HEAD_DIM)[None, :]
    branch, head = c // (NSA_HEADS * HEAD_DIM), (c % (NSA_HEADS * HEAD_DIM)) // HEAD_DIM
    return (r == head * 3 + branch).astype(BF16)


def _rope_tables(pos):
    half = HEAD_DIM // 2
    inv = ROPE_THETA ** (-jnp.arange(half, dtype=F32) / half)
    ang = pos.astype(F32)[:, None] * inv[None, :]
    cos, sin = jnp.cos(ang), jnp.sin(ang)
    return jnp.tile(jnp.concatenate([cos, cos], axis=1), (1, 2)), jnp.tile(jnp.concatenate([-sin, sin], axis=1), (1, 2))


def _tile_gain(g):
    return jnp.tile(g.astype(F32), LANES // HEAD_DIM)


def _expansion(n_tiles, nb_rows, tk):
    j = jnp.arange(n_tiles)[:, None, None]
    n = jnp.arange(nb_rows)[None, :, None]
    k = jnp.arange(tk)[None, None, :]
    return ((j * tk + k) // NSA_BLOCK == n).astype(BF16)


def _tokens_last(x, nb_, n_new, lead):
    y = jnp.moveaxis(x.reshape((n_new, nb_) + lead), 0, -1)
    return jnp.pad(y, [(0, 0)] * (y.ndim - 1) + [(0, LANES - n_new)])


def _to_seq_major(x, nb_, n_new, pad_to=None):
    y = jnp.swapaxes(x.reshape(n_new, nb_, x.shape[-1]), 0, 1)
    if pad_to is not None and pad_to > n_new:
        y = jnp.pad(y, ((0, 0), (0, pad_to - n_new), (0, 0)))
    return y


def _to_step_major(y, n_new):
    return jnp.swapaxes(y[:, :n_new], 0, 1).reshape(-1, y.shape[-1])


def kernel(x_prompt, x_sample, cache_nsa, cache_diff, state_nsa_win, state_conv_a, state_conv_d, page_table,
           c_prompt, c_sample, ada_w, ada_b, norm_g, ffn1_up, ffn1_down, ffn2_up, ffn2_down, w_mix_in,
           conv_a_w, conv_a_b, a_out, nsa_qk_g, nsa_cmp_w, nsa_out, diff_qk_g, diff_lambda, diff_out_g, diff_out,
           conv_d_w, conv_d_b, d_norm_g, d_norm_b, d_out, w_out):
    bp, sp, d = x_prompt.shape
    bs, ns, _ = x_sample.shape
    depth = ada_w.shape[0]
    n_pages = page_table.shape[1]
    page = cache_nsa.shape[2]
    past_len = n_pages * page
    wbuf = state_nsa_win.shape[2]

    rows = bp + bs
    rpad = -(-rows // 8) * 8
    c_all = jnp.pad(jnp.concatenate([c_prompt, c_sample], axis=0), ((0, rpad - rows), (0, 0)))
    ada = ada_proj(c_all, ada_w, ada_b)

    xp = x_prompt.reshape(bp * sp, d)
    xs = jnp.swapaxes(x_sample, 0, 1).reshape(ns * bs, d)
    tm_p, tm_s = 512, ns * bs
    tpg_p = sp // tm_p

    cos_p, sin_p = _rope_tables(jnp.arange(sp))
    cos_s, sin_s = _rope_tables(jnp.repeat(past_len + jnp.arange(ns), bs))
    gmat = (jnp.arange(LANES)[:, None] // HEAD_DIM == jnp.arange(LANES)[None, :] // HEAD_DIM).astype(BF16)
    gate_ex = _gate_expansion()
    tk_slc = 512
    emat_p = _expansion(sp // tk_slc, sp // NSA_BLOCK, tk_slc)
    n_real_s = (past_len + ns + NSA_BLOCK - 1) // NSA_BLOCK
    nbp_s = -(-n_real_s // LANES) * LANES
    selm_s = _expansion(1, nbp_s, past_len + page)[0].T
    cache_nsa_t = jnp.transpose(cache_nsa, (0, 1, 3, 4, 5, 2))
    win_state_t = jnp.transpose(state_nsa_win, (0, 1, 3, 4, 5, 2))
    cache_diff2 = cache_diff.reshape(depth, cache_diff.shape[1], page * 2 * DIFF_HEADS, 2 * HEAD_DIM)

    zeros_a = jnp.zeros((bp, 8, A_WIDTH), F32)
    zeros_d = jnp.zeros((bp, 32, D_WIDTH), F32)

    outs = {k: [] for k in ("nsa_p", "nsa_s", "win_p", "ca_p", "ca_s", "cd_p", "cd_s")}
    drows_p = drows_s = win_out_t = None
    for l in range(depth):
        lam_init = 0.8 - 0.6 * math.exp(-0.3 * l)
        mod_p = ada[l, :, 0:bp].reshape(N_ADA, bp, 1, d)
        mod_s = ada[l, :, bp:bp + bs].reshape(N_ADA, 1, bs, d)
        g = norm_g[l]
        w1u, w1d = ffn1_up[l].astype(BF16), ffn1_down[l].astype(BF16)
        w2u, w2d = ffn2_up[l].astype(BF16), ffn2_down[l].astype(BF16)
        wmix = _mix_weight(w_mix_in[l])
        gains = jnp.stack([_tile_gain(nsa_qk_g[l, 0]), _tile_gain(nsa_qk_g[l, 1]), _tile_gain(nsa_qk_g[l, 2]),
                           _tile_gain(nsa_qk_g[l, 3]), _tile_gain(diff_qk_g[l, 0]), _tile_gain(diff_qk_g[l, 1]),
                           jnp.zeros((LANES,), F32), jnp.zeros((LANES,), F32)])
        cw = nsa_cmp_w[l]
        w2 = jnp.concatenate([cw[0], cw[0], cw[1], cw[1]], axis=1)
        wt = jnp.tile(jnp.swapaxes(cw, 1, 2), (1, 1, page // NSA_BLOCK))
        wt = jnp.broadcast_to(wt[:, None], (2, NSA_KV_HEADS, HEAD_DIM, page)).reshape(4 * HEAD_DIM, page)
        wts = [a_out[l].astype(BF16), nsa_out[l].astype(BF16), diff_out[l].astype(BF16), d_out[l].astype(BF16),
               w_out[l].astype(BF16)]
        conv_w = (conv_a_w[l], conv_a_b[l][None], conv_d_w[l], conv_d_b[l][None], d_norm_g[l][None], d_norm_b[l][None])
        lam_p, og = diff_lambda[l], diff_out_g[l][None]

        xp = ffn(xp, mod_p, 0, g[0:1], w1u, w1d, tm_p, tpg_p)
        zp = mix_in(xp, mod_p, g[1:2], wmix, 1024, sp // 1024)
        qn, qr, rows_p, win_p, dq, drows_p, ksd, vsd, dkb, dvb = qk_prep(
            zp, cos_p, sin_p, gains, gmat, tm_p, tpg_p, l, depth, drows_p)
        ua, ud, ca, cd = conv_branches(zp, zeros_a, zeros_d, *conv_w, nseq=bp, tm=tm_p, stride=1)
        cmp_p = nsa_compress(rows_p, w2)
        o_cmp, sel = nsa_cmp_select(qn, cmp_p, bp)
        o_slc = nsa_slc(qr, ksd, vsd, sel, emat_p, bp)
        o_win = nsa_win(qr, win_p, bp)
        uc = diff_attn(dq, dkb, dvb, lam_p, og, bp, lam_init)
        xp = merge_out(xp, mod_p, zp, ua, uc, ud, o_cmp, o_slc, o_win, gate_ex, *wts, tm=tm_p, tiles_per_group=tpg_p)
        xp = ffn(xp, mod_p, 6, g[2:3], w2u, w2d, tm_p, tpg_p)
        outs["nsa_p"].append(rows_p.reshape(bp, sp, 4, NSA_KV_HEADS, HEAD_DIM))
        wkeep = min(NSA_WINDOW, sp)
        outs["win_p"].append(win_p.reshape(bp, sp, 2, NSA_KV_HEADS, HEAD_DIM)[:, sp - wkeep:])
        outs["ca_p"].append(ca)
        outs["cd_p"].append(cd)

        xs = ffn(xs, mod_s, 0, g[0:1], w1u, w1d, tm_s, 1)
        zs = mix_in(xs, mod_s, g[1:2], wmix, tm_s, 1)
        qn, qr, rows_s, win_s, dq, drows_s = qk_prep(zs, cos_s, sin_s, gains, gmat, tm_s, 1, l, depth, drows_s)[:6]
        past_a = jnp.swapaxes(state_conv_a[l], 0, 1).reshape(1, (A_CONV - 1) * bs, A_WIDTH)
        past_d = jnp.swapaxes(state_conv_d[l], 0, 1).reshape(1, (D_CONV - 1) * bs, D_WIDTH)
        ua, ud, ca, cd = conv_branches(zs, past_a, past_d, *conv_w, nseq=1, tm=tm_s, stride=bs)
        rows_b = _to_seq_major(rows_s, bs, ns)
        new_t = _tokens_last(rows_s, bs, ns, (4, NSA_KV_HEADS, HEAD_DIM))
        win_new_t = _tokens_last(win_s, bs, ns, (2, NSA_KV_HEADS, HEAD_DIM))
        o_cmp, o_slc, o_win, win_out_t = nsa_sample(
            l, page_table, cache_nsa_t, new_t, _to_seq_major(qn, bs, ns, QPAD), _to_seq_major(qr, bs, ns, QPAD),
            win_state_t, win_new_t, wt, selm_s, ns, win_out_t)
        drows_b = _to_seq_major(drows_s[l].reshape(ns * bs, 2 * DIFF_HEADS * LANES), bs, ns, QPAD)
        uc = diff_sample(l, page_table, cache_diff2, drows_b, _to_seq_major(dq, bs, ns, QPAD), lam_p, og, ns, lam_init)
        xs = merge_out(xs, mod_s, zs, ua, _to_step_major(uc, ns).astype(BF16), ud, _to_step_major(o_cmp, ns),
                       _to_step_major(o_slc, ns), _to_step_major(o_win, ns), gate_ex, *wts, tm=tm_s, tiles_per_group=1)
        xs = ffn(xs, mod_s, 6, g[2:3], w2u, w2d, tm_s, 1)
        outs["nsa_s"].append(rows_b.reshape(bs, ns, 4, NSA_KV_HEADS, HEAD_DIM))
        outs["ca_s"].append(jnp.swapaxes(ca.reshape(A_CONV - 1, bs, A_WIDTH), 0, 1))
        outs["cd_s"].append(jnp.swapaxes(cd.reshape(D_CONV - 1, bs, D_WIDTH), 0, 1))

    y_p = xp.reshape(bp, sp, d)
    y_s = jnp.swapaxes(xs.reshape(ns, bs, d), 0, 1)
    st = {k: jnp.stack(v, axis=0) for k, v in outs.items()}
    diff_p = drows_p.reshape(depth, bp, sp, 2, DIFF_HEADS, 2 * HEAD_DIM)
    diff_s = jnp.swapaxes(drows_s.reshape(depth, ns, bs, 2, DIFF_HEADS, 2 * HEAD_DIM), 1, 2)
    win_s = jnp.moveaxis(win_out_t, -1, 2)
    return (y_p, y_s, st["nsa_p"], st["nsa_s"], diff_p, diff_s, st["win_p"], win_s,
            st["ca_p"], st["ca_s"], st["cd_p"], st["cd_s"])
```

```python
import functools
import math

import jax
import jax.numpy as jnp
from jax import lax
from jax.experimental import pallas as pl
from jax.experimental.pallas import tpu as pltpu

F32 = jnp.float32
BF16 = jnp.bfloat16

HEAD_DIM = 64
ROPE_THETA = 10000.0
NORM_EPS = 1e-6
A_WIDTH = 512
A_CONV = 3
NSA_HEADS = 8
NSA_KV_HEADS = 2
NSA_GROUP = NSA_HEADS // NSA_KV_HEADS
NSA_BLOCK = 64
NSA_TOPK = 16
NSA_WINDOW = 512
DIFF_HEADS = 4
D_WIDTH = 512
D_CONV = 31
N_BRANCH = 4
N_ADA = 9
LANES = 128
NEG = -1e30
ATT_SCALE = HEAD_DIM ** -0.5
Q_SCALE = ATT_SCALE * math.log2(math.e)
VMEM_LIMIT = 56 * 1024 * 1024

C_KV = 0
C_MERGE = 1024
C_GLU = 5120
C_AB = 6144
C_AC = 6656
C_AX = 7168
C_NQ = 7680
C_DQ = 8192
C_DK = 8704
C_DV = 9216
C_NG = 768
MIX_W = 9728
MIX_TN = 4864
FFN_NJ = 1


def _cparams(sem):
    return pltpu.CompilerParams(dimension_semantics=sem, vmem_limit_bytes=VMEM_LIMIT)


def _sigmoid(x):
    return 1.0 / (1.0 + jnp.exp(-x))


def _silu(x):
    return x * _sigmoid(x)


def _rows_to(m, tm):
    r = m.shape[0]
    if r == 1 or r == tm:
        return m
    return jnp.concatenate([m] * (tm // r), axis=0)


def _modulate(x, g, shift, scale):
    ms = jnp.mean(x * x, axis=-1, keepdims=True)
    y = x * lax.rsqrt(ms + NORM_EPS) * g
    return y * (1.0 + scale) + shift


def _dot(a, b):
    return jnp.dot(a, b, preferred_element_type=F32)


def _dot_nt(a, b, precision=None):
    return lax.dot_general(a, b, (((1,), (1,)), ((), ())), preferred_element_type=F32, precision=precision)


def _lane_lo(shape):
    return (lax.broadcasted_iota(jnp.int32, shape, len(shape) - 1) % LANES) < HEAD_DIM


def _dup_halves(x, lo):
    xr = pltpu.roll(x, HEAD_DIM, 1)
    return jnp.where(lo, x, xr), jnp.where(lo, xr, x)


def _ada_kernel(c_ref, w_ref, b_ref, o_ref):
    s = _silu(c_ref[...])
    o_ref[...] = _dot(s.astype(BF16), w_ref[...].astype(BF16)) + b_ref[...]


def ada_proj(c_all, ada_w, ada_b):
    r, d = c_all.shape
    nl = ada_w.shape[0]
    return pl.pallas_call(
        _ada_kernel,
        grid=(nl, N_ADA),
        in_specs=[
            pl.BlockSpec((r, d), lambda l, k: (0, 0)),
            pl.BlockSpec((None, d, d), lambda l, k: (l, 0, k)),
            pl.BlockSpec((None, None, 1, d), lambda l, k: (l, k, 0, 0)),
        ],
        out_specs=pl.BlockSpec((None, None, r, d), lambda l, k: (l, k, 0, 0)),
        out_shape=jax.ShapeDtypeStruct((nl, N_ADA, r, d), F32),
        compiler_params=_cparams(("arbitrary", "arbitrary")),
        name="ada_proj",
    )(c_all, ada_w, ada_b.reshape(nl, N_ADA, 1, d))


def _mod_spec(mod, k, tiles_per_group, ngrid):
    _, _, r, d = mod.shape
    if ngrid == 1:
        return pl.BlockSpec((None, None, r, d), lambda i: (k, i // tiles_per_group, 0, 0))
    return pl.BlockSpec((None, None, r, d), lambda i, j: (k, i // tiles_per_group, 0, 0))


def _ffn_kernel(x_ref, sh_ref, sc_ref, gt_ref, g_ref, wa_ref, wv_ref, wd_ref, o_ref, h_ref, acc_ref, *, nj):
    j = pl.program_id(1)
    tm = x_ref.shape[0]

    @pl.when(j == 0)
    def _():
        h = _modulate(x_ref[...], g_ref[...], _rows_to(sh_ref[...], tm), _rows_to(sc_ref[...], tm))
        h_ref[...] = h.astype(BF16)
        acc_ref[...] = jnp.zeros_like(acc_ref)

    h = h_ref[...]
    a = _dot(h, wa_ref[...])
    v = _dot(h, wv_ref[...])
    act = (_silu(a) * v).astype(BF16)
    acc_ref[...] += _dot(act, wd_ref[...])

    @pl.when(j == nj - 1)
    def _():
        o_ref[...] = x_ref[...] + 0.5 * _rows_to(gt_ref[...], tm) * acc_ref[...]


def ffn(x, mod, k0, g, w_up, w_down, tm, tiles_per_group):
    t, d = x.shape
    f = w_down.shape[0]
    nj = FFN_NJ
    tf = f // nj
    return pl.pallas_call(
        functools.partial(_ffn_kernel, nj=nj),
        grid=(t // tm, nj),
        in_specs=[
            pl.BlockSpec((tm, d), lambda i, j: (i, 0)),
            _mod_spec(mod, k0, tiles_per_group, 2),
            _mod_spec(mod, k0 + 1, tiles_per_group, 2),
            _mod_spec(mod, k0 + 2, tiles_per_group, 2),
            pl.BlockSpec((1, d), lambda i, j: (0, 0)),
            pl.BlockSpec((d, tf), lambda i, j: (0, j)),
            pl.BlockSpec((d, tf), lambda i, j: (0, nj + j)),
            pl.BlockSpec((tf, d), lambda i, j: (j, 0)),
        ],
        out_specs=pl.BlockSpec((tm, d), lambda i, j: (i, 0)),
        out_shape=jax.ShapeDtypeStruct((t, d), F32),
        scratch_shapes=[pltpu.VMEM((tm, d), BF16), pltpu.VMEM((tm, d), F32)],
        compiler_params=_cparams(("arbitrary", "arbitrary")),
        name="ffn",
    )(x, mod, mod, mod, g, w_up, w_up, w_down)


def _mix_kernel(x_ref, sh_ref, sc_ref, g_ref, w_ref, o_ref, h_ref):
    tm = x_ref.shape[0]

    @pl.when(pl.program_id(1) == 0)
    def _():
        h = _modulate(x_ref[...], g_ref[...], _rows_to(sh_ref[...], tm), _rows_to(sc_ref[...], tm))
        h_ref[...] = h.astype(BF16)

    o_ref[...] = _dot(h_ref[...], w_ref[...])


def mix_in(x, mod, g, w_mix, tm, tiles_per_group, tn=MIX_TN):
    t, d = x.shape
    nw = w_mix.shape[1]
    return pl.pallas_call(
        _mix_kernel,
        grid=(t // tm, nw // tn),
        in_specs=[
            pl.BlockSpec((tm, d), lambda i, j: (i, 0)),
            _mod_spec(mod, 3, tiles_per_group, 2),
            _mod_spec(mod, 4, tiles_per_group, 2),
            pl.BlockSpec((1, d), lambda i, j: (0, 0)),
            pl.BlockSpec((d, tn), lambda i, j: (0, j)),
        ],
        out_specs=pl.BlockSpec((tm, tn), lambda i, j: (i, j)),
        out_shape=jax.ShapeDtypeStruct((t, nw), F32),
        scratch_shapes=[pltpu.VMEM((tm, d), BF16)],
        compiler_params=_cparams(("arbitrary", "arbitrary")),
        name="mix_in",
    )(x, mod, mod, g, w_mix)


def _group_norm(x, gain, gmat):
    x2 = x * x
    hi = x2.astype(BF16)
    lo = (x2 - hi.astype(F32)).astype(BF16)
    ss = _dot(hi, gmat) + _dot(lo, gmat)
    return x * lax.rsqrt(ss * (1.0 / HEAD_DIM) + NORM_EPS) * gain


def _rope(x, cos, sin_signed, lo):
    sw = jnp.where(lo, pltpu.roll(x, LANES - HEAD_DIM // 2, 1), pltpu.roll(x, HEAD_DIM // 2, 1))
    return x * cos + sw * sin_signed


def _prep_kernel(kv_ref, nq_ref, dq_ref, dk_ref, dv_ref, cos_ref, sin_ref, gain_ref, gmat_ref,
                 qn_ref, qr_ref, rows_ref, win_ref, dqo_ref, drows_ref, ksd_ref, vsd_ref, dkb_ref, dvb_ref):
    cos = cos_ref[...]
    sin = sin_ref[...]
    gmat = gmat_ref[...]
    lo32 = (lax.broadcasted_iota(jnp.int32, cos.shape, 1) % HEAD_DIM) < HEAD_DIM // 2
    lo64 = _lane_lo(cos.shape)

    def gain(r):
        return gain_ref[r:r + 1, :]

    def piece(ref, p):
        return ref[:, p * LANES:(p + 1) * LANES]

    for p in range(4):
        qn = _group_norm(piece(nq_ref, p), gain(0), gmat)
        qn_ref[:, p * LANES:(p + 1) * LANES] = qn
        qr_ref[:, p * LANES:(p + 1) * LANES] = (_rope(qn, cos, sin, lo32) * Q_SCALE).astype(BF16)
    rows_ref[:, 0:LANES] = _group_norm(piece(kv_ref, 0), gain(1), gmat)
    rows_ref[:, LANES:2 * LANES] = piece(kv_ref, 1)
    k_slc = _rope(_group_norm(piece(kv_ref, 2), gain(2), gmat), cos, sin, lo32)
    v_slc = piece(kv_ref, 3)
    rows_ref[:, 2 * LANES:3 * LANES] = k_slc
    rows_ref[:, 3 * LANES:4 * LANES] = v_slc
    for hk, (kd, vd) in enumerate(zip(_dup_halves(k_slc, lo64), _dup_halves(v_slc, lo64))):
        ksd_ref[:, hk * LANES:(hk + 1) * LANES] = kd.astype(BF16)
        vsd_ref[:, hk * LANES:(hk + 1) * LANES] = vd.astype(BF16)
    win_ref[:, 0:LANES] = _rope(_group_norm(piece(kv_ref, 4), gain(3), gmat), cos, sin, lo32)
    win_ref[:, LANES:2 * LANES] = piece(kv_ref, 5)
    for p in range(4):
        dq = _rope(_group_norm(piece(dq_ref, p), gain(4), gmat), cos, sin, lo32)
        dqo_ref[:, p * LANES:(p + 1) * LANES] = (dq * Q_SCALE).astype(BF16)
        dk = _rope(_group_norm(piece(dk_ref, p), gain(5), gmat), cos, sin, lo32)
        drows_ref[:, p, :] = dk
        dkb_ref[:, p * LANES:(p + 1) * LANES] = dk.astype(BF16)
        drows_ref[:, DIFF_HEADS + p, :] = piece(dv_ref, p)
    dvb_ref[...] = dv_ref[...].astype(BF16)


def qk_prep(z, cos, sin, gains, gmat, tm, tiles_per_seq, layer, n_layers, drows_prev):
    t = z.shape[0]

    def zs(width, col):
        return pl.BlockSpec((tm, width), lambda i: (i, col // width))

    def os_(width):
        return pl.BlockSpec((tm, width), lambda i: (i, 0))

    tab = pl.BlockSpec((tm, LANES), lambda i: (i % tiles_per_seq, 0))
    outs = ((512, F32), (512, BF16), (512, F32), (256, F32), (512, BF16), None,
            (256, BF16), (256, BF16), (512, BF16), (512, BF16))
    drows_shape = jax.ShapeDtypeStruct((n_layers, t, 2 * DIFF_HEADS, LANES), F32)
    drows_spec = pl.BlockSpec((None, tm, 2 * DIFF_HEADS, LANES), lambda i: (layer, i, 0, 0))
    in_specs = [zs(768, C_KV), zs(512, C_NQ), zs(512, C_DQ), zs(512, C_DK), zs(512, C_DV), tab, tab,
                pl.BlockSpec((8, LANES), lambda i: (0, 0)), pl.BlockSpec((LANES, LANES), lambda i: (0, 0))]
    args = [z, z, z, z, z, cos, sin, gains, gmat]
    kern, aliases = _prep_kernel, {}
    if drows_prev is not None:
        in_specs.append(pl.BlockSpec(memory_space=pl.ANY))
        args.append(drows_prev)
        aliases = {len(args) - 1: 5}
        kern = _prep_kernel_aliased
    return pl.pallas_call(
        kern,
        grid=(t // tm,),
        in_specs=in_specs,
        out_specs=[drows_spec if o is None else os_(o[0]) for o in outs],
        out_shape=[drows_shape if o is None else jax.ShapeDtypeStruct((t, o[0]), o[1]) for o in outs],
        input_output_aliases=aliases,
        compiler_params=_cparams(("arbitrary",)),
        name="qk_prep",
    )(*args)


def _prep_kernel_aliased(*refs):
    _prep_kernel(*refs[:9], *refs[10:])


def _conv_kernel(ab_ref, ac_ref, ax_ref, glu_ref, pa_ref, pd_ref, wa_ref, ba_ref, wd_ref, bd_ref, lg_ref, lb_ref,
                 ua_ref, ud_ref, sa_ref, sd_ref, bufa, bufd, *, tm, ha, hd, stride, nt):
    i = pl.program_id(1)

    @pl.when(i == 0)
    def _():
        bufa[0:ha, :] = pa_ref[...]
        bufd[0:hd, :] = pd_ref[...]

    if nt > 1:
        @pl.when(i > 0)
        def _():
            bufa[0:ha, :] = bufa[tm:tm + ha, :]
            bufd[0:hd, :] = bufd[tm:tm + hd, :]

    bufa[ha:ha + tm, :] = ac_ref[...] * ax_ref[...]
    ya = ba_ref[...]
    for k in range(A_CONV):
        ya = ya + wa_ref[k:k + 1, :] * bufa[pl.ds(ha - (A_CONV - 1 - k) * stride, tm), :]
    ua_ref[...] = (ab_ref[...] * ya).astype(ua_ref.dtype)

    glu = glu_ref[...]
    bufd[hd:hd + tm, :] = glu[:, :D_WIDTH] * _sigmoid(glu[:, D_WIDTH:])
    yd = bd_ref[...]
    for k in range(D_CONV):
        yd = yd + wd_ref[k:k + 1, :] * bufd[pl.ds(hd - (D_CONV - 1 - k) * stride, tm), :]
    mu = jnp.mean(yd, axis=-1, keepdims=True)
    var = jnp.mean(jnp.square(yd - mu), axis=-1, keepdims=True)
    y = (yd - mu) * lax.rsqrt(var + NORM_EPS) * lg_ref[...] + lb_ref[...]
    ud_ref[...] = _silu(y).astype(ud_ref.dtype)

    @pl.when(i == nt - 1)
    def _():
        sa_ref[...] = bufa[ha + tm - (A_CONV - 1) * stride:ha + tm, :]
        sd_ref[...] = bufd[hd + tm - (D_CONV - 1) * stride:hd + tm, :]


def conv_branches(z, past_a, past_d, wa, ba, wd, bd, lg, lb, nseq, tm, stride):
    t = z.shape[0]
    nt = t // nseq // tm
    ha, hd = past_a.shape[1], past_d.shape[1]
    if nt > 1:
        assert tm >= ha and tm >= hd

    def zs(width, col):
        return pl.BlockSpec((tm, width), lambda s, i: (s * nt + i, col // width))

    def cs(shape):
        return pl.BlockSpec(shape, lambda s, i: (0, 0))

    sa_rows, sd_rows = (A_CONV - 1) * stride, (D_CONV - 1) * stride
    return pl.pallas_call(
        functools.partial(_conv_kernel, tm=tm, ha=ha, hd=hd, stride=stride, nt=nt),
        grid=(nseq, nt),
        in_specs=[zs(512, C_AB), zs(512, C_AC), zs(512, C_AX), zs(1024, C_GLU),
                  pl.BlockSpec((None, ha, A_WIDTH), lambda s, i: (s, 0, 0)),
                  pl.BlockSpec((None, hd, D_WIDTH), lambda s, i: (s, 0, 0)),
                  cs((A_CONV, A_WIDTH)), cs((1, A_WIDTH)), cs((D_CONV, D_WIDTH)), cs((1, D_WIDTH)),
                  cs((1, D_WIDTH)), cs((1, D_WIDTH))],
        out_specs=[pl.BlockSpec((tm, A_WIDTH), lambda s, i: (s * nt + i, 0)),
                   pl.BlockSpec((tm, D_WIDTH), lambda s, i: (s * nt + i, 0)),
                   pl.BlockSpec((None, sa_rows, A_WIDTH), lambda s, i: (s, 0, 0)),
                   pl.BlockSpec((None, sd_rows, D_WIDTH), lambda s, i: (s, 0, 0))],
        out_shape=[jax.ShapeDtypeStruct((t, A_WIDTH), BF16), jax.ShapeDtypeStruct((t, D_WIDTH), BF16),
                   jax.ShapeDtypeStruct((nseq, sa_rows, A_WIDTH), F32),
                   jax.ShapeDtypeStruct((nseq, sd_rows, D_WIDTH), F32)],
        scratch_shapes=[pltpu.VMEM((ha + tm, A_WIDTH), F32), pltpu.VMEM((hd + tm, D_WIDTH), F32)],
        compiler_params=_cparams(("arbitrary", "arbitrary")),
        name="conv_branches",
    )(z, z, z, z, past_a, past_d, wa, ba, wd, bd, lg, lb)


def _merge_kernel(x_ref, gt_ref, ua_ref, uc_ref, ud_ref, oc_ref, os_ref, ow_ref, ng_ref, ex_ref,
                  m0_ref, m1_ref, m2_ref, m3_ref, wa_ref, wb_ref, wc_ref, wd_ref, wo_ref, o_ref):
    tm = x_ref.shape[0]
    g = _sigmoid(ng_ref[...])
    g_hi = g.astype(BF16)
    g_lo = (g - g_hi.astype(F32)).astype(BF16)
    gx = _dot(g_hi, ex_ref[...]) + _dot(g_lo, ex_ref[...])
    ub = (gx[:, 0:512] * oc_ref[...] + gx[:, 512:1024] * os_ref[...] + gx[:, 1024:1536] * ow_ref[...]).astype(BF16)
    merged = _sigmoid(m0_ref[...]) * _dot(ua_ref[...], wa_ref[...])
    merged += _sigmoid(m1_ref[...]) * _dot(ub, wb_ref[...])
    merged += _sigmoid(m2_ref[...]) * _dot(uc_ref[...], wc_ref[...])
    merged += _sigmoid(m3_ref[...]) * _dot(ud_ref[...], wd_ref[...])
    y = _dot(merged.astype(BF16), wo_ref[...])
    o_ref[...] = x_ref[...] + _rows_to(gt_ref[...], tm) * y


def merge_out(x, mod, z, ua, uc, ud, o_cmp, o_slc, o_win, gate_ex, w_a, w_b, w_c, w_d, w_o, tm, tiles_per_group):
    t, d = x.shape

    def ts(width):
        return pl.BlockSpec((tm, width), lambda i: (i, 0))

    def zs(width, col):
        return pl.BlockSpec((tm, width), lambda i: (i, col // width))

    def ws(shape):
        return pl.BlockSpec(shape, lambda i: (0, 0))

    return pl.pallas_call(
        _merge_kernel,
        grid=(t // tm,),
        in_specs=[ts(d), _mod_spec(mod, 5, tiles_per_group, 1), ts(512), ts(512), ts(512), ts(512), ts(512), ts(512),
                  zs(LANES, C_NG), ws(gate_ex.shape),
                  zs(1024, C_MERGE), zs(1024, C_MERGE + 1024), zs(1024, C_MERGE + 2048), zs(1024, C_MERGE + 3072),
                  ws((512, d)), ws((512, d)), ws((512, d)), ws((512, d)), ws((d, d))],
        out_specs=ts(d),
        out_shape=jax.ShapeDtypeStruct((t, d), F32),
        compiler_params=_cparams(("arbitrary",)),
        name="merge_out",
    )(x, mod, ua, uc, ud, o_cmp, o_slc, o_win, z, gate_ex, z, z, z, z, w_a, w_b, w_c, w_d, w_o)


def _compress_kernel(r_ref, w_ref, o_ref):
    x = r_ref[...]
    nb = x.shape[0] // NSA_BLOCK
    o_ref[...] = jnp.sum(x.reshape(nb, NSA_BLOCK, x.shape[1]) * w_ref[...][None], axis=1)


def nsa_compress(rows, w2, tm=512):
    t = rows.shape[0]
    return pl.pallas_call(
        _compress_kernel,
        grid=(t // tm,),
        in_specs=[pl.BlockSpec((tm, 256), lambda i: (i, 0)), pl.BlockSpec((NSA_BLOCK, 256), lambda i: (0, 0))],
        out_specs=pl.BlockSpec((tm // NSA_BLOCK, 256), lambda i: (i, 0)),
        out_shape=jax.ShapeDtypeStruct((t // NSA_BLOCK, 256), F32),
        compiler_params=_cparams(("arbitrary",)),
        name="nsa_compress",
    )(rows, w2)


def _masked_softmax(s, mask):
    s = jnp.where(mask, s, NEG)
    p = jnp.exp(s - jnp.max(s, axis=-1, keepdims=True))
    p = jnp.where(mask, p, 0.0)
    return p / jnp.maximum(jnp.sum(p, axis=-1, keepdims=True), 1e-30)


def _top_blocks(imp, n_sel):
    imp = imp.T
    idx = lax.broadcasted_iota(jnp.int32, imp.shape, 0)
    big = imp.shape[0]
    sel = jnp.zeros(imp.shape, F32)
    for _ in range(n_sel):
        m = jnp.max(imp, axis=0, keepdims=True)
        first = jnp.min(jnp.where(imp == m, idx, big), axis=0, keepdims=True)
        hit = idx == first
        sel = jnp.where(hit, 1.0, sel)
        imp = jnp.where(hit, -3.0, imp)
    return sel.T


def _top_blocks_by_rank(imp, n_sel):
    nb = imp.shape[1]
    idx = lax.broadcasted_iota(jnp.int32, imp.shape, 1)
    rank = jnp.zeros(imp.shape, F32)
    for d in range(1, nb):
        other = pltpu.roll(imp, d, 1)
        beats = (other > imp) | ((other == imp) & (idx >= d))
        rank = rank + jnp.where(beats, 1.0, 0.0)
    return jnp.where(rank < n_sel, 1.0, 0.0)


def _cmp_select(q, kdup, vdup, qpos, nb, n_sel):
    tq = q.shape[0]
    qlo = _lane_lo((tq, LANES))
    blk = lax.broadcasted_iota(jnp.int32, (1, nb), 1)
    complete = (blk + 1) * NSA_BLOCK - 1 <= qpos
    cur = qpos // NSA_BLOCK
    outs, sels = [], []
    for hk in range(NSA_KV_HEADS):
        imp = jnp.zeros((tq, nb), F32)
        vd = vdup[hk].astype(BF16)
        for a in (2 * hk, 2 * hk + 1):
            qp = q[:, a * LANES:(a + 1) * LANES]
            pair = []
            for par in range(2):
                qm = jnp.where(qlo if par == 0 else ~qlo, qp, 0.0)
                s = _dot_nt(qm, kdup[hk], precision=lax.Precision.HIGHEST)
                pc = _masked_softmax(s * ATT_SCALE, complete)
                imp = imp + pc
                pair.append(_dot(pc.astype(BF16), vd))
            outs.append(jnp.where(qlo, pair[0], pair[1]))
        forced = (blk == cur) | (blk == 0)
        imp = jnp.where(forced, NSA_GROUP + 1.0, imp)
        imp = jnp.where(blk <= cur, imp, -1.0)
        sels.append(_top_blocks(imp, n_sel))
    return jnp.concatenate(outs, axis=1), sels


def _cmpsel_kernel(q_ref, c_ref, o_ref, sel_ref, *, tq, n_sel):
    qpos = pl.program_id(1) * tq + lax.broadcasted_iota(jnp.int32, (tq, 1), 0)
    nb = c_ref.shape[0]
    lo = _lane_lo((nb, LANES))
    kdup = _dup_halves(c_ref[:, 0:LANES], lo)
    vdup = _dup_halves(c_ref[:, LANES:2 * LANES], lo)
    o, sels = _cmp_select(q_ref[...], kdup, vdup, qpos, nb, n_sel)
    o_ref[...] = o
    sel_ref[...] = jnp.concatenate(sels, axis=1).astype(sel_ref.dtype)


def nsa_cmp_select(q_n, cmp, nseq, tq=512):
    t = q_n.shape[0]
    nq = t // nseq // tq
    nb = cmp.shape[0] // nseq
    return pl.pallas_call(
        functools.partial(_cmpsel_kernel, tq=tq, n_sel=min(NSA_TOPK, nb)),
        grid=(nseq, nq),
        in_specs=[pl.BlockSpec((tq, 512), lambda b, i: (b * nq + i, 0)),
                  pl.BlockSpec((nb, 256), lambda b, i: (b, 0))],
        out_specs=[pl.BlockSpec((tq, 512), lambda b, i: (b * nq + i, 0)),
                   pl.BlockSpec((tq, 2 * nb), lambda b, i: (b * nq + i, 0))],
        out_shape=[jax.ShapeDtypeStruct((t, 512), F32), jax.ShapeDtypeStruct((t, 2 * nb), BF16)],
        compiler_params=_cparams(("arbitrary", "arbitrary")),
        name="nsa_cmp_select",
    )(q_n, cmp)


def _online_step(s, m, acc, v_ext, exp_dtype=F32):
    tk = s.shape[1]
    m_new = jnp.maximum(m, jnp.max(s, axis=-1, keepdims=True))
    x = s - jnp.concatenate([m_new] * (tk // LANES), axis=1)
    p = jnp.exp2(x.astype(exp_dtype)).astype(BF16)
    alpha = jnp.exp2(m - m_new)
    if acc.shape[1] != LANES:
        alpha = jnp.concatenate([alpha] * (acc.shape[1] // LANES), axis=1)
    return m_new, alpha * acc + _dot(p, v_ext)


GROWTH_LIMIT = 64.0


def _single_pass_step(s, ref, acc, growth, v_ext):
    tk = s.shape[1]
    p = jnp.exp2(s - jnp.concatenate([ref] * (tk // LANES), axis=1)).astype(BF16)
    row_max = jnp.max(s, axis=-1, keepdims=True)
    growth = jnp.maximum(growth, row_max - ref)
    ref_new = jnp.maximum(ref, row_max)
    alpha = jnp.exp2(ref - ref_new)
    if acc.shape[1] != LANES:
        alpha = jnp.concatenate([alpha] * (acc.shape[1] // LANES), axis=1)
    return ref_new, (acc + _dot(p, v_ext)) * alpha, growth


def _lane_bcast0(x):
    return jnp.broadcast_to(x[:, 0:1], x.shape)


def _slc_kernel(q_ref, k_ref, v_ref, sel_ref, e_ref, o_ref, *, tq, tk, nb):
    qi = pl.program_id(1)
    n_full = (qi * tq) // tk
    rows = NSA_GROUP * tq
    qlo = _lane_lo((tq, LANES))
    klo = _lane_lo((tk, LANES))
    zero = jnp.zeros((tq, LANES), BF16)
    qpos = qi * tq + lax.broadcasted_iota(jnp.int32, (tq, 1), 0)
    col = lax.broadcasted_iota(jnp.int32, (1, tk), 1)
    for hk in range(NSA_KV_HEADS):
        parts = []
        for c in range(NSA_GROUP):
            h = hk * NSA_GROUP + c
            qp = q_ref[:, (h // 2) * LANES:(h // 2 + 1) * LANES]
            parts.append(jnp.where(qlo, qp, zero) if h % 2 == 0 else jnp.where(qlo, zero, qp))
        qst = jnp.concatenate(parts, axis=0)
        sel = sel_ref[:, hk * nb:(hk + 1) * nb]

        def tile(j, diag, qst=qst, sel=sel, hk=hk):
            off = pl.multiple_of(j * tk, tk)
            k = k_ref[pl.ds(off, tk), hk * LANES:(hk + 1) * LANES]
            v = v_ref[pl.ds(off, tk), hk * LANES:(hk + 1) * LANES]
            bias = (_dot(sel, e_ref[j]) - 1.0) * (-NEG)
            if diag:
                bias = jnp.where(j * tk + col <= qpos, bias, NEG)
            s = _dot_nt(qst, k) + jnp.concatenate([bias] * NSA_GROUP, axis=0)
            return s, jnp.where(klo, v, jnp.ones_like(v))

        def finish(acc, hk=hk):
            o = acc / jnp.maximum(pltpu.roll(acc, HEAD_DIM, 1), 1e-30)
            for pair in range(NSA_GROUP // 2):
                a = hk * (NSA_GROUP // 2) + pair
                o_even = o[(2 * pair) * tq:(2 * pair + 1) * tq]
                o_odd = o[(2 * pair + 1) * tq:(2 * pair + 2) * tq]
                o_ref[:, a * LANES:(a + 1) * LANES] = jnp.where(qlo, o_even, pltpu.roll(o_odd, HEAD_DIM, 1))

        def body(j, carry, tile=tile):
            s, vx = tile(j, False)
            return _online_step(s, *carry, vx, exp_dtype=BF16)

        init = (jnp.full((rows, LANES), NEG, F32), jnp.zeros((rows, LANES), F32))
        m, acc = lax.fori_loop(0, n_full, body, init)
        s, vx = tile(n_full, True)
        finish(_online_step(s, m, acc, vx, exp_dtype=BF16)[1])


def nsa_slc(q_rot, ksd, vsd, sel, emat, nseq, tq=256):
    t = q_rot.shape[0]
    s = t // nseq
    nq = s // tq
    nk, nb, tk = emat.shape
    return pl.pallas_call(
        functools.partial(_slc_kernel, tq=tq, tk=tk, nb=nb),
        grid=(nseq, nq),
        in_specs=[pl.BlockSpec((tq, 512), lambda b, i: (b * nq + i, 0)),
                  pl.BlockSpec((s, 256), lambda b, i: (b, 0)),
                  pl.BlockSpec((s, 256), lambda b, i: (b, 0)),
                  pl.BlockSpec((tq, 2 * nb), lambda b, i: (b * nq + i, 0)),
                  pl.BlockSpec((nk, nb, tk), lambda b, i: (0, 0, 0))],
        out_specs=pl.BlockSpec((tq, 512), lambda b, i: (b * nq + i, 0)),
        out_shape=jax.ShapeDtypeStruct((t, 512), F32),
        compiler_params=_cparams(("arbitrary", "arbitrary")),
        name="nsa_slc",
    )(q_rot, ksd, vsd, sel, emat)


def _win_kernel(q_ref, wp_ref, wc_ref, o_ref, *, tq):
    i = pl.program_id(1)
    lo = _lane_lo((tq, LANES))
    qpos = i * tq + lax.broadcasted_iota(jnp.int32, (tq, 1), 0)
    chunks = []
    for off, ref in ((-tq, wp_ref), (0, wc_ref)):
        kpos = i * tq + off + lax.broadcasted_iota(jnp.int32, (1, tq), 1)
        mask = (kpos >= 0) & (kpos <= qpos) & (qpos - kpos <= NSA_WINDOW)
        chunks.append((mask, _dup_halves(ref[:, 0:LANES], lo), _dup_halves(ref[:, LANES:2 * LANES], lo)))
    zero = jnp.zeros((tq, LANES), BF16)
    for a in range(NSA_HEADS // 2):
        pair = []
        for par in range(2):
            hk = (2 * a + par) // NSA_GROUP
            qp = q_ref[:, a * LANES:(a + 1) * LANES]
            qm = jnp.where(lo, qp, zero) if par == 0 else jnp.where(lo, zero, qp)
            ss = [jnp.where(mask, _dot_nt(qm, kd[hk].astype(BF16)), NEG) for mask, kd, _ in chunks]
            m = jnp.maximum(jnp.max(ss[0], axis=-1, keepdims=True), jnp.max(ss[1], axis=-1, keepdims=True))
            ps = [jnp.where(ch[0], jnp.exp2(s - m), 0.0) for s, ch in zip(ss, chunks)]
            l = jnp.sum(ps[0], axis=-1, keepdims=True) + jnp.sum(ps[1], axis=-1, keepdims=True)
            inv = 1.0 / jnp.maximum(l, 1e-30)
            o = _dot((ps[0] * inv).astype(BF16), chunks[0][2][hk].astype(BF16))
            o += _dot((ps[1] * inv).astype(BF16), chunks[1][2][hk].astype(BF16))
            pair.append(o)
        o_ref[:, a * LANES:(a + 1) * LANES] = jnp.where(lo, pair[0], pair[1])


def nsa_win(q_rot, win_rows, nseq, tq=NSA_WINDOW):
    t = q_rot.shape[0]
    nq = t // nseq // tq
    return pl.pallas_call(
        functools.partial(_win_kernel, tq=tq),
        grid=(nseq, nq),
        in_specs=[pl.BlockSpec((tq, 512), lambda b, i: (b * nq + i, 0)),
                  pl.BlockSpec((tq, 256), lambda b, i: (b * nq + jnp.maximum(i - 1, 0), 0)),
                  pl.BlockSpec((tq, 256), lambda b, i: (b * nq + i, 0))],
        out_specs=pl.BlockSpec((tq, 512), lambda b, i: (b * nq + i, 0)),
        out_shape=jax.ShapeDtypeStruct((t, 512), F32),
        compiler_params=_cparams(("arbitrary", "arbitrary")),
        name="nsa_win",
    )(q_rot, win_rows, win_rows)


def _diff_lambda(lam_ref, lam_init):
    lp = lam_ref[...]
    a = jnp.sum(lp[0:1] * lp[1:2], axis=-1, keepdims=True)
    b = jnp.sum(lp[2:3] * lp[3:4], axis=-1, keepdims=True)
    return jnp.exp(a) - jnp.exp(b) + lam_init


def _diff_finish(o1, o2, lam, g, lam_init):
    o = o1 - lam * o2
    ms = jnp.mean(o * o, axis=-1, keepdims=True)
    return o * lax.rsqrt(ms + NORM_EPS) * g * (1.0 - lam_init)


def _diff_kernel(q_ref, k_ref, v_ref, lam_ref, g_ref, o_ref, *, tq, tk, lam_init):
    qi = pl.program_id(2)
    n_full = (qi * tq) // tk
    lo = _lane_lo((tq, LANES))
    q = q_ref[...]
    zero = jnp.zeros_like(q)
    qst = jnp.concatenate([jnp.where(lo, q, zero), jnp.where(lo, zero, q)], axis=0)
    qpos = qi * tq + lax.broadcasted_iota(jnp.int32, (tq, 1), 0)
    col = lax.broadcasted_iota(jnp.int32, (1, tk), 1)
    bias = jnp.where(n_full * tk + col <= qpos, 0.0, NEG)
    bias = jnp.concatenate([bias, bias], axis=0)
    ones = jnp.ones((tk, LANES), BF16)

    def tile(j):
        off = pl.multiple_of(j * tk, tk)
        return _dot_nt(qst, k_ref[pl.ds(off, tk), :]), jnp.concatenate([v_ref[pl.ds(off, tk), :], ones], axis=1)

    def walk(step, init):
        def full(j, carry):
            s, vx = tile(j)
            return step(s, carry, vx)

        carry = lax.fori_loop(0, n_full, full, init)
        s, vx = tile(n_full)
        return step(s + bias, carry, vx)

    def finish(acc):
        o = acc[:, 0:LANES] / jnp.maximum(acc[:, LANES:2 * LANES], 1e-30)
        lam = _diff_lambda(lam_ref, lam_init)
        o_ref[...] = _diff_finish(o[0:tq], o[tq:2 * tq], lam, g_ref[...], lam_init).astype(o_ref.dtype)

    ref0 = _lane_bcast0(_dot_nt(qst, k_ref[0:LANES, :]))
    init = (ref0, jnp.zeros((2 * tq, 2 * LANES), F32), jnp.zeros((2 * tq, LANES), F32))
    _, acc, growth = walk(lambda s, carry, vx: _single_pass_step(s, *carry, vx), init)
    fast_ok = jnp.max(growth) <= GROWTH_LIMIT

    @pl.when(fast_ok)
    def _():
        finish(acc)

    @pl.when(jnp.logical_not(fast_ok))
    def _():
        init = (jnp.full((2 * tq, LANES), NEG, F32), jnp.zeros((2 * tq, 2 * LANES), F32))
        finish(walk(lambda s, carry, vx: _online_step(s, *carry, vx), init)[1])


def diff_attn(dq, dk, dv, lam_p, out_g, nseq, lam_init, tq=256, tk=1024):
    t = dq.shape[0]
    s = t // nseq
    nq = s // tq
    return pl.pallas_call(
        functools.partial(_diff_kernel, tq=tq, tk=tk, lam_init=lam_init),
        grid=(nseq, DIFF_HEADS, nq),
        in_specs=[pl.BlockSpec((tq, LANES), lambda b, h, i: (b * nq + i, h)),
                  pl.BlockSpec((s, LANES), lambda b, h, i: (b, h)),
                  pl.BlockSpec((s, LANES), lambda b, h, i: (b, h)),
                  pl.BlockSpec((4, HEAD_DIM), lambda b, h, i: (0, 0)),
                  pl.BlockSpec((1, LANES), lambda b, h, i: (0, 0))],
        out_specs=pl.BlockSpec((tq, LANES), lambda b, h, i: (b * nq + i, h)),
        out_shape=jax.ShapeDtypeStruct((t, 512), BF16),
        compiler_params=_cparams(("arbitrary", "arbitrary", "arbitrary")),
        name="diff_attn",
    )(dq, dk, dv, lam_p, out_g)


QPAD = 8
SEQS_PER_STEP = 2


def _pad_keys(x):
    return jnp.concatenate([x, jnp.zeros((LANES - x.shape[0], x.shape[1]), x.dtype)], axis=0)


def _nsa_sample_kernel(pt_ref, *refs, n_pages, past_len, n_new, n_sel, n_seq):
    pages = refs[:n_seq * n_pages]
    new_ref, qn_ref, qr_ref, wst_ref, wnew_ref, wt_ref, selm_ref = refs[n_seq * n_pages:n_seq * n_pages + 7]
    oc_ref, os_ref, ow_ref, wout_ref = refs[-4:]
    for s in range(n_seq):
        chunks = tuple(pages[s * n_pages:(s + 1) * n_pages]) + (new_ref.at[s],)
        _nsa_sample_one(chunks, qn_ref.at[s], qr_ref.at[s], wst_ref.at[s], wnew_ref.at[s], wt_ref, selm_ref,
                        oc_ref.at[s], os_ref.at[s], ow_ref.at[s], wout_ref.at[s],
                        n_pages=n_pages, past_len=past_len, n_new=n_new, n_sel=n_sel)


def _nsa_sample_one(chunks, qn_ref, qr_ref, wst_ref, wnew_ref, wt_ref, selm_ref, oc_ref, os_ref, ow_ref, wout_ref,
                    *, n_pages, past_len, n_new, n_sel):
    n_real = (past_len + n_new + NSA_BLOCK - 1) // NSA_BLOCK
    nbp = selm_ref.shape[1]
    page = chunks[0].shape[3]

    def twice(x):
        return jnp.concatenate([x, x], axis=0)

    wt = wt_ref[...]
    y = jnp.concatenate([ch[0:2].reshape(4 * HEAD_DIM, page) * wt for ch in chunks], axis=1)
    hi = y.astype(BF16)
    selm = selm_ref[...]
    cmp_t = _dot(hi, selm)
    yk = y[0:2 * HEAD_DIM]
    cmp_k = cmp_t[0:2 * HEAD_DIM] + _dot((yk - hi[0:2 * HEAD_DIM].astype(F32)).astype(BF16), selm)

    tok = lax.broadcasted_iota(jnp.int32, (QPAD, 1), 0)
    qpos = past_len + tok
    rows = NSA_GROUP * QPAD
    qpos_st = jnp.concatenate([qpos] * NSA_GROUP, axis=0)
    qlo = _lane_lo((QPAD, LANES))

    def stack_group(q_ref, hk, zero):
        parts = []
        for c in range(NSA_GROUP):
            h = hk * NSA_GROUP + c
            qp = q_ref[:, (h // 2) * LANES:(h // 2 + 1) * LANES]
            parts.append(jnp.where(qlo, qp, zero) if h % 2 == 0 else jnp.where(qlo, zero, qp))
        return jnp.concatenate(parts, axis=0)

    def emit(o_ref, per_hk):
        for a in range(NSA_HEADS // 2):
            hk, c = (2 * a) // NSA_GROUP, (2 * a) % NSA_GROUP
            o = per_hk[hk]
            o_ref[:, a * LANES:(a + 1) * LANES] = jnp.where(qlo, o[c * QPAD:(c + 1) * QPAD], o[(c + 1) * QPAD:(c + 2) * QPAD])

    blk = lax.broadcasted_iota(jnp.int32, (1, nbp), 1)
    complete = ((blk + 1) * NSA_BLOCK - 1 <= qpos_st) & (blk < n_real)
    cur = qpos // NSA_BLOCK
    per_hk, imps = [], []
    for hk in range(NSA_KV_HEADS):
        kd = twice(cmp_k[hk * HEAD_DIM:(hk + 1) * HEAD_DIM])
        vd = twice(cmp_t[(2 + hk) * HEAD_DIM:(3 + hk) * HEAD_DIM]).astype(BF16)
        s = jnp.dot(stack_group(qn_ref, hk, jnp.zeros((QPAD, LANES), F32)), kd,
                    precision=lax.Precision.HIGHEST, preferred_element_type=F32)
        pc = _masked_softmax(s * ATT_SCALE, complete)
        per_hk.append(_dot_nt(pc.astype(BF16), vd))
        imp = functools.reduce(lambda a, b: a + b, [pc[c * QPAD:(c + 1) * QPAD] for c in range(NSA_GROUP)])
        imp = jnp.where((blk == cur) | (blk == 0), NSA_GROUP + 1.0, imp)
        imp = jnp.where(blk <= cur, imp, -1.0)
        imps.append(jnp.where(blk < n_real, imp, -2.0))
    emit(oc_ref, per_hk)
    sel_all = _top_blocks_by_rank(jnp.concatenate(imps, axis=0), n_sel)
    sels = [sel_all[hk * QPAD:(hk + 1) * QPAD] for hk in range(NSA_KV_HEADS)]

    qst = [stack_group(qr_ref, hk, jnp.zeros((QPAD, LANES), BF16)) for hk in range(NSA_KV_HEADS)]
    key_new = lax.broadcasted_iota(jnp.int32, (1, LANES), 1)
    new_valid = (key_new < n_new) & (past_len + key_new <= qpos_st)

    def attend(kvs, biases, hk):
        k_all = jnp.concatenate([twice(k).astype(BF16) for k, _ in kvs], axis=1)
        v_all = jnp.concatenate([twice(v).astype(BF16) for _, v in kvs], axis=1)
        s = _dot(qst[hk], k_all) + jnp.concatenate(biases, axis=1)
        p = jnp.exp2(s - jnp.max(s, axis=-1, keepdims=True))
        p = p / jnp.maximum(jnp.sum(p, axis=-1, keepdims=True), 1e-30)
        return _dot_nt(p.astype(BF16), v_all)

    per_hk = []
    for hk in range(NSA_KV_HEADS):
        selx = _dot_nt(sels[hk].astype(BF16), selm)
        selx = jnp.concatenate([selx] * NSA_GROUP, axis=0)
        kvs, biases = [], []
        for j, ch in enumerate(chunks):
            kvs.append((ch[2, hk], ch[3, hk]))
            bias = (selx[:, j * page:(j + 1) * page] - 1.0) * (-NEG)
            biases.append(jnp.where(new_valid, bias, NEG) if j == n_pages else bias)
        per_hk.append(attend(kvs, biases, hk))
    emit(os_ref, per_hk)

    wbuf = wst_ref.shape[3]
    kpos = past_len - wbuf + lax.broadcasted_iota(jnp.int32, (1, wbuf), 1)
    bias_st = jnp.where((kpos >= 0) & (kpos <= qpos_st) & (qpos_st - kpos <= NSA_WINDOW), 0.0, NEG)
    bias_new = jnp.where(new_valid & (qpos_st - (past_len + key_new) <= NSA_WINDOW), 0.0, NEG)
    per_hk = []
    for hk in range(NSA_KV_HEADS):
        kvs = [(wst_ref[0, hk], wst_ref[1, hk]), (wnew_ref[0, hk], wnew_ref[1, hk])]
        per_hk.append(attend(kvs, [bias_st, bias_new], hk))
    emit(ow_ref, per_hk)

    lane = lax.broadcasted_iota(jnp.int32, (HEAD_DIM, wbuf), 1)
    for r in range(2):
        for hk in range(NSA_KV_HEADS):
            shifted = pltpu.roll(wst_ref[r, hk], wbuf - n_new, 1)
            tail = jnp.concatenate([jnp.zeros((HEAD_DIM, wbuf - LANES), F32),
                                    pltpu.roll(wnew_ref[r, hk], LANES - n_new, 1)], axis=1)
            wout_ref[r, hk] = jnp.where(lane >= wbuf - n_new, tail, shifted)


def nsa_sample(layer, page_table, cache_t, new_t, q_n, q_rot, win_t, win_new_t, wt, selm, n_new, wout_prev):
    nb_, n_pages = page_table.shape
    page = cache_t.shape[5]
    past_len = n_pages * page
    n_real = (past_len + n_new + NSA_BLOCK - 1) // NSA_BLOCK
    wbuf = win_t.shape[5]
    kv = (NSA_KV_HEADS, HEAD_DIM)

    n_seq = SEQS_PER_STEP

    def page_spec(s, j):
        return pl.BlockSpec((None, None, 4) + kv + (page,), lambda b, pt: (layer, pt[b * n_seq + s, j], 0, 0, 0, 0))

    def seq(shape):
        return pl.BlockSpec((n_seq,) + shape, lambda b, pt: (b,) + (0,) * len(shape))

    def const(shape):
        return pl.BlockSpec(shape, lambda b, pt: (0,) * len(shape))

    win_spec = pl.BlockSpec((None, n_seq, 2) + kv + (wbuf,), lambda b, pt: (layer, b, 0, 0, 0, 0))
    in_specs = [page_spec(s, j) for s in range(n_seq) for j in range(n_pages)] + [
        seq((4,) + kv + (page,)), seq((QPAD, 512)), seq((QPAD, 512)), win_spec,
        seq((2,) + kv + (LANES,)), const(wt.shape), const(selm.shape)]
    args = [page_table] + [cache_t] * (n_seq * n_pages) + [new_t, q_n, q_rot, win_t, win_new_t, wt, selm]
    aliases = {}
    if wout_prev is not None:
        in_specs.append(pl.BlockSpec(memory_space=pl.ANY))
        args.append(wout_prev)
        aliases = {len(args) - 1: 3}
    grid_spec = pltpu.PrefetchScalarGridSpec(
        num_scalar_prefetch=1,
        grid=(nb_ // n_seq,),
        in_specs=in_specs,
        out_specs=[seq((QPAD, 512)), seq((QPAD, 512)), seq((QPAD, 512)), win_spec],
    )
    return pl.pallas_call(
        functools.partial(_nsa_sample_kernel, n_pages=n_pages, past_len=past_len, n_new=n_new,
                          n_sel=min(NSA_TOPK, n_real), n_seq=n_seq),
        grid_spec=grid_spec,
        out_shape=[jax.ShapeDtypeStruct((nb_, QPAD, 512), F32)] * 3 + [jax.ShapeDtypeStruct(win_t.shape, F32)],
        input_output_aliases=aliases,
        compiler_params=_cparams(("arbitrary",)),
        name="nsa_sample",
    )(*args)


def _diff_sample_kernel(pt_ref, *refs, n_pages, past_len, n_new, lam_init, n_seq):
    new_ref, q_ref, lam_ref, g_ref, o_ref = refs[n_seq * n_pages:]
    for s in range(n_seq):
        _diff_sample_one(refs[s * n_pages:(s + 1) * n_pages], new_ref.at[s], q_ref.at[s], lam_ref, g_ref, o_ref.at[s],
                         n_pages=n_pages, n_new=n_new, lam_init=lam_init)


def _diff_sample_one(pages, new_ref, q_ref, lam_ref, g_ref, o_ref, *, n_pages, n_new, lam_init):
    lo = _lane_lo((QPAD, LANES))
    tok = lax.broadcasted_iota(jnp.int32, (2 * QPAD, 1), 0) % QPAD
    key_new = lax.broadcasted_iota(jnp.int32, (1, LANES), 1)
    new_valid = (key_new < n_new) & (key_new <= tok)
    lam = _diff_lambda(lam_ref, lam_init)
    q = q_ref[...]
    new = new_ref[...]
    zero = jnp.zeros((QPAD, LANES), BF16)
    page = pages[0].shape[0] // (2 * DIFF_HEADS)
    bias = jnp.concatenate([jnp.zeros((2 * QPAD, n_pages * page), F32), jnp.where(new_valid, 0.0, NEG)], axis=1)
    for h in range(DIFF_HEADS):
        qp = q[:, h * LANES:(h + 1) * LANES]
        qst = jnp.concatenate([jnp.where(lo, qp, zero), jnp.where(lo, zero, qp)], axis=0)
        ks = [pages[j][pl.ds(h, page, stride=2 * DIFF_HEADS), :] for j in range(n_pages)]
        ks.append(_pad_keys(new[:, h * LANES:(h + 1) * LANES]))
        vs = [pages[j][pl.ds(DIFF_HEADS + h, page, stride=2 * DIFF_HEADS), :] for j in range(n_pages)]
        vs.append(_pad_keys(new[:, (DIFF_HEADS + h) * LANES:(DIFF_HEADS + h + 1) * LANES]))
        k_all = jnp.concatenate([k.astype(BF16) for k in ks], axis=0)
        v_all = jnp.concatenate([v.astype(BF16) for v in vs], axis=0)
        s = _dot_nt(qst, k_all) + bias
        p = jnp.exp2(s - jnp.max(s, axis=-1, keepdims=True))
        p = p / jnp.maximum(jnp.sum(p, axis=-1, keepdims=True), 1e-30)
        o = _dot((p[0:QPAD] - lam * p[QPAD:2 * QPAD]).astype(BF16), v_all)
        ms = jnp.mean(o * o, axis=-1, keepdims=True)
        o_ref[:, h * LANES:(h + 1) * LANES] = (o * lax.rsqrt(ms + NORM_EPS) * g_ref[...] * (1.0 - lam_init)).astype(o_ref.dtype)


def diff_sample(layer, page_table, cache, new_rows, dq, lam_p, out_g, n_new, lam_init):
    nb_, n_pages = page_table.shape
    page = cache.shape[2] // (2 * DIFF_HEADS)

    n_seq = SEQS_PER_STEP

    def page_spec(s, j):
        return pl.BlockSpec((None, None, page * 2 * DIFF_HEADS, LANES),
                            lambda b, pt: (layer, pt[b * n_seq + s, j], 0, 0))

    grid_spec = pltpu.PrefetchScalarGridSpec(
        num_scalar_prefetch=1,
        grid=(nb_ // n_seq,),
        in_specs=[page_spec(s, j) for s in range(n_seq) for j in range(n_pages)] + [
            pl.BlockSpec((n_seq, QPAD, 1024), lambda b, pt: (b, 0, 0)),
            pl.BlockSpec((n_seq, QPAD, 512), lambda b, pt: (b, 0, 0)),
            pl.BlockSpec((4, HEAD_DIM), lambda b, pt: (0, 0)),
            pl.BlockSpec((1, LANES), lambda b, pt: (0, 0))],
        out_specs=pl.BlockSpec((n_seq, QPAD, 512), lambda b, pt: (b, 0, 0)),
    )
    return pl.pallas_call(
        functools.partial(_diff_sample_kernel, n_pages=n_pages, past_len=n_pages * page, n_new=n_new,
                          lam_init=lam_init, n_seq=n_seq),
        grid_spec=grid_spec,
        out_shape=jax.ShapeDtypeStruct((nb_, QPAD, 512), F32),
        compiler_params=_cparams(("arbitrary",)),
        name="diff_sample",
    )(page_table, *([cache] * (n_seq * n_pages)), new_rows, dq, lam_p, out_g)


def _mix_weight(w):
    d = w.shape[0]
    splits = (A_WIDTH, A_WIDTH, A_WIDTH, 512, 768, 3 * NSA_HEADS, 512, 512, 512, 2 * D_WIDTH, N_BRANCH * d)
    offs = [0]
    for s in splits:
        offs.append(offs[-1] + s)
    part = {n: w[:, offs[i]:offs[i + 1]] for i, n in enumerate(
        ("ab", "ac", "ax", "nq", "kv", "g", "dq", "dk", "dv", "glu", "merge"))}
    pad = jnp.zeros((d, C_MERGE - C_NG - 3 * NSA_HEADS), w.dtype)
    cols = [part["kv"], part["g"], pad, part["merge"], part["glu"], part["ab"], part["ac"], part["ax"], part["nq"],
            part["dq"], part["dk"], part["dv"]]
    out = jnp.concatenate(cols, axis=1).astype(BF16)
    assert out.shape[1] == MIX_W
    return out


def _gate_expansion():
    r = jnp.arange(LANES)[:, None]
    c = jnp.arange(3 * NSA_HEADS * HEAD_DIM)[None, :]
    branch, head = c // (NSA_HEADS * HEAD_DIM), (c % (NSA_HEADS * HEAD_DIM)) // HEAD_DIM
    return (r == head * 3 + branch).astype(BF16)


def _rope_tables(pos):
    half = HEAD_DIM // 2
    inv = ROPE_THETA ** (-jnp.arange(half, dtype=F32) / half)
    ang = pos.astype(F32)[:, None] * inv[None, :]
    cos, sin = jnp.cos(ang), jnp.sin(ang)
    return jnp.tile(jnp.concatenate([cos, cos], axis=1), (1, 2)), jnp.tile(jnp.concatenate([-sin, sin], axis=1), (1, 2))


def _tile_gain(g):
    return jnp.tile(g.astype(F32), LANES // HEAD_DIM)


def _expansion(n_tiles, nb_rows, tk):
    j = jnp.arange(n_tiles)[:, None, None]
    n = jnp.arange(nb_rows)[None, :, None]
    k = jnp.arange(tk)[None, None, :]
    return ((j * tk + k) // NSA_BLOCK == n).astype(BF16)


def _tokens_last(x, nb_, n_new, lead):
    y = jnp.moveaxis(x.reshape((n_new, nb_) + lead), 0, -1)
    return jnp.pad(y, [(0, 0)] * (y.ndim - 1) + [(0, LANES - n_new)])


def _to_seq_major(x, nb_, n_new, pad_to=None):
    y = jnp.swapaxes(x.reshape(n_new, nb_, x.shape[-1]), 0, 1)
    if pad_to is not None and pad_to > n_new:
        y = jnp.pad(y, ((0, 0), (0, pad_to - n_new), (0, 0)))
    return y


def _to_step_major(y, n_new):
    return jnp.swapaxes(y[:, :n_new], 0, 1).reshape(-1, y.shape[-1])


def kernel(x_prompt, x_sample, cache_nsa, cache_diff, state_nsa_win, state_conv_a, state_conv_d, page_table,
           c_prompt, c_sample, ada_w, ada_b, norm_g, ffn1_up, ffn1_down, ffn2_up, ffn2_down, w_mix_in,
           conv_a_w, conv_a_b, a_out, nsa_qk_g, nsa_cmp_w, nsa_out, diff_qk_g, diff_lambda, diff_out_g, diff_out,
           conv_d_w, conv_d_b, d_norm_g, d_norm_b, d_out, w_out):
    bp, sp, d = x_prompt.shape
    bs, ns, _ = x_sample.shape
    depth = ada_w.shape[0]
    n_pages = page_table.shape[1]
    page = cache_nsa.shape[2]
    past_len = n_pages * page
    wbuf = state_nsa_win.shape[2]

    rows = bp + bs
    rpad = -(-rows // 8) * 8
    c_all = jnp.pad(jnp.concatenate([c_prompt, c_sample], axis=0), ((0, rpad - rows), (0, 0)))
    ada = ada_proj(c_all, ada_w, ada_b)

    xp = x_prompt.reshape(bp * sp, d)
    xs = jnp.swapaxes(x_sample, 0, 1).reshape(ns * bs, d)
    tm_p, tm_s = 512, ns * bs
    tpg_p = sp // tm_p

    cos_p, sin_p = _rope_tables(jnp.arange(sp))
    cos_s, sin_s = _rope_tables(jnp.repeat(past_len + jnp.arange(ns), bs))
    gmat = (jnp.arange(LANES)[:, None] // HEAD_DIM == jnp.arange(LANES)[None, :] // HEAD_DIM).astype(BF16)
    gate_ex = _gate_expansion()
    tk_slc = 512
    emat_p = _expansion(sp // tk_slc, sp // NSA_BLOCK, tk_slc)
    n_real_s = (past_len + ns + NSA_BLOCK - 1) // NSA_BLOCK
    nbp_s = -(-n_real_s // LANES) * LANES
    selm_s = _expansion(1, nbp_s, past_len + page)[0].T
    cache_nsa_t = jnp.transpose(cache_nsa, (0, 1, 3, 4, 5, 2))
    win_state_t = jnp.transpose(state_nsa_win, (0, 1, 3, 4, 5, 2))
    cache_diff2 = cache_diff.reshape(depth, cache_diff.shape[1], page * 2 * DIFF_HEADS, 2 * HEAD_DIM)

    zeros_a = jnp.zeros((bp, 8, A_WIDTH), F32)
    zeros_d = jnp.zeros((bp, 32, D_WIDTH), F32)

    outs = {k: [] for k in ("nsa_p", "nsa_s", "win_p", "ca_p", "ca_s", "cd_p", "cd_s")}
    drows_p = drows_s = win_out_t = None
    for l in range(depth):
        lam_init = 0.8 - 0.6 * math.exp(-0.3 * l)
        mod_p = ada[l, :, 0:bp].reshape(N_ADA, bp, 1, d)
        mod_s = ada[l, :, bp:bp + bs].reshape(N_ADA, 1, bs, d)
        g = norm_g[l]
        w1u, w1d = ffn1_up[l].astype(BF16), ffn1_down[l].astype(BF16)
        w2u, w2d = ffn2_up[l].astype(BF16), ffn2_down[l].astype(BF16)
        wmix = _mix_weight(w_mix_in[l])
        gains = jnp.stack([_tile_gain(nsa_qk_g[l, 0]), _tile_gain(nsa_qk_g[l, 1]), _tile_gain(nsa_qk_g[l, 2]),
                           _tile_gain(nsa_qk_g[l, 3]), _tile_gain(diff_qk_g[l, 0]), _tile_gain(diff_qk_g[l, 1]),
                           jnp.zeros((LANES,), F32), jnp.zeros((LANES,), F32)])
        cw = nsa_cmp_w[l]
        w2 = jnp.concatenate([cw[0], cw[0], cw[1], cw[1]], axis=1)
        wt = jnp.tile(jnp.swapaxes(cw, 1, 2), (1, 1, page // NSA_BLOCK))
        wt = jnp.broadcast_to(wt[:, None], (2, NSA_KV_HEADS, HEAD_DIM, page)).reshape(4 * HEAD_DIM, page)
        wts = [a_out[l].astype(BF16), nsa_out[l].astype(BF16), diff_out[l].astype(BF16), d_out[l].astype(BF16),
               w_out[l].astype(BF16)]
        conv_w = (conv_a_w[l], conv_a_b[l][None], conv_d_w[l], conv_d_b[l][None], d_norm_g[l][None], d_norm_b[l][None])
        lam_p, og = diff_lambda[l], diff_out_g[l][None]

        xp = ffn(xp, mod_p, 0, g[0:1], w1u, w1d, tm_p, tpg_p)
        zp = mix_in(xp, mod_p, g[1:2], wmix, tm_p, tpg_p)
        qn, qr, rows_p, win_p, dq, drows_p, ksd, vsd, dkb, dvb = qk_prep(
            zp, cos_p, sin_p, gains, gmat, tm_p, tpg_p, l, depth, drows_p)
        ua, ud, ca, cd = conv_branches(zp, zeros_a, zeros_d, *conv_w, nseq=bp, tm=tm_p, stride=1)
        cmp_p = nsa_compress(rows_p, w2)
        o_cmp, sel = nsa_cmp_select(qn, cmp_p, bp)
        o_slc = nsa_slc(qr, ksd, vsd, sel, emat_p, bp)
        o_win = nsa_win(qr, win_p, bp)
        uc = diff_attn(dq, dkb, dvb, lam_p, og, bp, lam_init)
        xp = merge_out(xp, mod_p, zp, ua, uc, ud, o_cmp, o_slc, o_win, gate_ex, *wts, tm=tm_p, tiles_per_group=tpg_p)
        xp = ffn(xp, mod_p, 6, g[2:3], w2u, w2d, tm_p, tpg_p)
        outs["nsa_p"].append(rows_p.reshape(bp, sp, 4, NSA_KV_HEADS, HEAD_DIM))
        wkeep = min(NSA_WINDOW, sp)
        outs["win_p"].append(win_p.reshape(bp, sp, 2, NSA_KV_HEADS, HEAD_DIM)[:, sp - wkeep:])
        outs["ca_p"].append(ca)
        outs["cd_p"].append(cd)

        xs = ffn(xs, mod_s, 0, g[0:1], w1u, w1d, tm_s, 1)
        zs = mix_in(xs, mod_s, g[1:2], wmix, tm_s, 1)
        qn, qr, rows_s, win_s, dq, drows_s = qk_prep(zs, cos_s, sin_s, gains, gmat, tm_s, 1, l, depth, drows_s)[:6]
        past_a = jnp.swapaxes(state_conv_a[l], 0, 1).reshape(1, (A_CONV - 1) * bs, A_WIDTH)
        past_d = jnp.swapaxes(state_conv_d[l], 0, 1).reshape(1, (D_CONV - 1) * bs, D_WIDTH)
        ua, ud, ca, cd = conv_branches(zs, past_a, past_d, *conv_w, nseq=1, tm=tm_s, stride=bs)
        rows_b = _to_seq_major(rows_s, bs, ns)
        new_t = _tokens_last(rows_s, bs, ns, (4, NSA_KV_HEADS, HEAD_DIM))
        win_new_t = _tokens_last(win_s, bs, ns, (2, NSA_KV_HEADS, HEAD_DIM))
        o_cmp, o_slc, o_win, win_out_t = nsa_sample(
            l, page_table, cache_nsa_t, new_t, _to_seq_major(qn, bs, ns, QPAD), _to_seq_major(qr, bs, ns, QPAD),
            win_state_t, win_new_t, wt, selm_s, ns, win_out_t)
        drows_b = _to_seq_major(drows_s[l].reshape(ns * bs, 2 * DIFF_HEADS * LANES), bs, ns, QPAD)
        uc = diff_sample(l, page_table, cache_diff2, drows_b, _to_seq_major(dq, bs, ns, QPAD), lam_p, og, ns, lam_init)
        xs = merge_out(xs, mod_s, zs, ua, _to_step_major(uc, ns).astype(BF16), ud, _to_step_major(o_cmp, ns),
                       _to_step_major(o_slc, ns), _to_step_major(o_win, ns), gate_ex, *wts, tm=tm_s, tiles_per_group=1)
        xs = ffn(xs, mod_s, 6, g[2:3], w2u, w2d, tm_s, 1)
        outs["nsa_s"].append(rows_b.reshape(bs, ns, 4, NSA_KV_HEADS, HEAD_DIM))
        outs["ca_s"].append(jnp.swapaxes(ca.reshape(A_CONV - 1, bs, A_WIDTH), 0, 1))
        outs["cd_s"].append(jnp.swapaxes(cd.reshape(D_CONV - 1, bs, D_WIDTH), 0, 1))

    y_p = xp.reshape(bp, sp, d)
    y_s = jnp.swapaxes(xs.reshape(ns, bs, d), 0, 1)
    st = {k: jnp.stack(v, axis=0) for k, v in outs.items()}
    diff_p = drows_p.reshape(depth, bp, sp, 2, DIFF_HEADS, 2 * HEAD_DIM)
    diff_s = jnp.swapaxes(drows_s.reshape(depth, ns, bs, 2, DIFF_HEADS, 2 * HEAD_DIM), 1, 2)
    win_s = jnp.moveaxis(win_out_t, -1, 2)
    return (y_p, y_s, st["nsa_p"], st["nsa_s"], diff_p, diff_s, st["win_p"], win_s,
            st["ca_p"], st["ca_s"], st["cd_p"], st["cd_s"])
```

```python
import functools
import math

import jax
import jax.numpy as jnp
from jax import lax
from jax.experimental import pallas as pl
from jax.experimental.pallas import tpu as pltpu

F32 = jnp.float32
BF16 = jnp.bfloat16

HEAD_DIM = 64
ROPE_THETA = 10000.0
NORM_EPS = 1e-6
A_WIDTH = 512
A_CONV = 3
NSA_HEADS = 8
NSA_KV_HEADS = 2
NSA_GROUP = NSA_HEADS // NSA_KV_HEADS
NSA_BLOCK = 64
NSA_TOPK = 16
NSA_WINDOW = 512
DIFF_HEADS = 4
D_WIDTH = 512
D_CONV = 31
N_BRANCH = 4
N_ADA = 9
LANES = 128
NEG = -1e30
ATT_SCALE = HEAD_DIM ** -0.5
Q_SCALE = ATT_SCALE * math.log2(math.e)
VMEM_LIMIT = 56 * 1024 * 1024

C_KV = 0
C_MERGE = 1024
C_GLU = 5120
C_AB = 6144
C_AC = 6656
C_AX = 7168
C_NQ = 7680
C_DQ = 8192
C_DK = 8704
C_DV = 9216
C_NG = 768
MIX_W = 9728
MIX_TN = 4864
FFN_NJ = 1


def _cparams(sem):
    return pltpu.CompilerParams(dimension_semantics=sem, vmem_limit_bytes=VMEM_LIMIT)


def _sigmoid(x):
    return 1.0 / (1.0 + jnp.exp(-x))


def _silu(x):
    return x * _sigmoid(x)


def _rows_to(m, tm):
    r = m.shape[0]
    if r == 1 or r == tm:
        return m
    return jnp.concatenate([m] * (tm // r), axis=0)


def _modulate(x, g, shift, scale):
    ms = jnp.mean(x * x, axis=-1, keepdims=True)
    y = x * lax.rsqrt(ms + NORM_EPS) * g
    return y * (1.0 + scale) + shift


def _dot(a, b):
    return jnp.dot(a, b, preferred_element_type=F32)


def _dot_nt(a, b, precision=None):
    return lax.dot_general(a, b, (((1,), (1,)), ((), ())), preferred_element_type=F32, precision=precision)


def _lane_lo(shape):
    return (lax.broadcasted_iota(jnp.int32, shape, len(shape) - 1) % LANES) < HEAD_DIM


def _dup_halves(x, lo):
    xr = pltpu.roll(x, HEAD_DIM, 1)
    return jnp.where(lo, x, xr), jnp.where(lo, xr, x)


def _ada_kernel(c_ref, w_ref, b_ref, o_ref):
    s = _silu(c_ref[...])
    o_ref[...] = _dot(s.astype(BF16), w_ref[...].astype(BF16)) + b_ref[...]


def ada_proj(c_all, ada_w, ada_b):
    r, d = c_all.shape
    nl = ada_w.shape[0]
    return pl.pallas_call(
        _ada_kernel,
        grid=(nl, N_ADA),
        in_specs=[
            pl.BlockSpec((r, d), lambda l, k: (0, 0)),
            pl.BlockSpec((None, d, d), lambda l, k: (l, 0, k)),
            pl.BlockSpec((None, None, 1, d), lambda l, k: (l, k, 0, 0)),
        ],
        out_specs=pl.BlockSpec((None, None, r, d), lambda l, k: (l, k, 0, 0)),
        out_shape=jax.ShapeDtypeStruct((nl, N_ADA, r, d), F32),
        compiler_params=_cparams(("arbitrary", "arbitrary")),
        name="ada_proj",
    )(c_all, ada_w, ada_b.reshape(nl, N_ADA, 1, d))


def _mod_spec(mod, k, tiles_per_group, ngrid):
    _, _, r, d = mod.shape
    if ngrid == 1:
        return pl.BlockSpec((None, None, r, d), lambda i: (k, i // tiles_per_group, 0, 0))
    return pl.BlockSpec((None, None, r, d), lambda i, j: (k, i // tiles_per_group, 0, 0))


def _ffn_kernel(x_ref, sh_ref, sc_ref, gt_ref, g_ref, wa_ref, wv_ref, wd_ref, o_ref, h_ref, acc_ref, *, nj):
    j = pl.program_id(1)
    tm = x_ref.shape[0]

    @pl.when(j == 0)
    def _():
        h = _modulate(x_ref[...], g_ref[...], _rows_to(sh_ref[...], tm), _rows_to(sc_ref[...], tm))
        h_ref[...] = h.astype(BF16)
        acc_ref[...] = jnp.zeros_like(acc_ref)

    h = h_ref[...]
    a = _dot(h, wa_ref[...])
    v = _dot(h, wv_ref[...])
    act = (_silu(a) * v).astype(BF16)
    acc_ref[...] += _dot(act, wd_ref[...])

    @pl.when(j == nj - 1)
    def _():
        o_ref[...] = x_ref[...] + 0.5 * _rows_to(gt_ref[...], tm) * acc_ref[...]


def ffn(x, mod, k0, g, w_up, w_down, tm, tiles_per_group):
    t, d = x.shape
    f = w_down.shape[0]
    nj = FFN_NJ
    tf = f // nj
    return pl.pallas_call(
        functools.partial(_ffn_kernel, nj=nj),
        grid=(t // tm, nj),
        in_specs=[
            pl.BlockSpec((tm, d), lambda i, j: (i, 0)),
            _mod_spec(mod, k0, tiles_per_group, 2),
            _mod_spec(mod, k0 + 1, tiles_per_group, 2),
            _mod_spec(mod, k0 + 2, tiles_per_group, 2),
            pl.BlockSpec((1, d), lambda i, j: (0, 0)),
            pl.BlockSpec((d, tf), lambda i, j: (0, j)),
            pl.BlockSpec((d, tf), lambda i, j: (0, nj + j)),
            pl.BlockSpec((tf, d), lambda i, j: (j, 0)),
        ],
        out_specs=pl.BlockSpec((tm, d), lambda i, j: (i, 0)),
        out_shape=jax.ShapeDtypeStruct((t, d), F32),
        scratch_shapes=[pltpu.VMEM((tm, d), BF16), pltpu.VMEM((tm, d), F32)],
        compiler_params=_cparams(("arbitrary", "arbitrary")),
        name="ffn",
    )(x, mod, mod, mod, g, w_up, w_up, w_down)


def _mix_kernel(x_ref, sh_ref, sc_ref, g_ref, w_ref, o_ref):
    tm = x_ref.shape[0]
    h = _modulate(x_ref[...], g_ref[...], _rows_to(sh_ref[...], tm), _rows_to(sc_ref[...], tm))
    o_ref[...] = _dot(h.astype(BF16), w_ref[...])


def mix_in(x, mod, g, w_mix, tm, tiles_per_group, tn=MIX_TN):
    t, d = x.shape
    nw = w_mix.shape[1]
    _, _, r, _ = mod.shape

    def mod_spec(k):
        return pl.BlockSpec((None, None, r, d), lambda j, i: (k, i // tiles_per_group, 0, 0))

    return pl.pallas_call(
        _mix_kernel,
        grid=(nw // tn, t // tm),
        in_specs=[
            pl.BlockSpec((tm, d), lambda j, i: (i, 0)),
            mod_spec(3),
            mod_spec(4),
            pl.BlockSpec((1, d), lambda j, i: (0, 0)),
            pl.BlockSpec((d, tn), lambda j, i: (0, j)),
        ],
        out_specs=pl.BlockSpec((tm, tn), lambda j, i: (i, j)),
        out_shape=jax.ShapeDtypeStruct((t, nw), F32),
        compiler_params=_cparams(("arbitrary", "arbitrary")),
        name="mix_in",
    )(x, mod, mod, g, w_mix)


def _group_norm(x, gain, gmat):
    x2 = x * x
    hi = x2.astype(BF16)
    lo = (x2 - hi.astype(F32)).astype(BF16)
    ss = _dot(hi, gmat) + _dot(lo, gmat)
    return x * lax.rsqrt(ss * (1.0 / HEAD_DIM) + NORM_EPS) * gain


def _rope(x, cos, sin_signed, lo):
    sw = jnp.where(lo, pltpu.roll(x, LANES - HEAD_DIM // 2, 1), pltpu.roll(x, HEAD_DIM // 2, 1))
    return x * cos + sw * sin_signed


def _prep_kernel(kv_ref, nq_ref, dq_ref, dk_ref, dv_ref, cos_ref, sin_ref, gain_ref, gmat_ref,
                 qn_ref, qr_ref, rows_ref, win_ref, dqo_ref, drows_ref, ksd_ref, vsd_ref, dkb_ref, dvb_ref):
    cos = cos_ref[...]
    sin = sin_ref[...]
    gmat = gmat_ref[...]
    lo32 = (lax.broadcasted_iota(jnp.int32, cos.shape, 1) % HEAD_DIM) < HEAD_DIM // 2
    lo64 = _lane_lo(cos.shape)

    def gain(r):
        return gain_ref[r:r + 1, :]

    def piece(ref, p):
        return ref[:, p * LANES:(p + 1) * LANES]

    for p in range(4):
        qn = _group_norm(piece(nq_ref, p), gain(0), gmat)
        qn_ref[:, p * LANES:(p + 1) * LANES] = qn
        qr_ref[:, p * LANES:(p + 1) * LANES] = (_rope(qn, cos, sin, lo32) * Q_SCALE).astype(BF16)
    rows_ref[:, 0:LANES] = _group_norm(piece(kv_ref, 0), gain(1), gmat)
    rows_ref[:, LANES:2 * LANES] = piece(kv_ref, 1)
    k_slc = _rope(_group_norm(piece(kv_ref, 2), gain(2), gmat), cos, sin, lo32)
    v_slc = piece(kv_ref, 3)
    rows_ref[:, 2 * LANES:3 * LANES] = k_slc
    rows_ref[:, 3 * LANES:4 * LANES] = v_slc
    for hk, (kd, vd) in enumerate(zip(_dup_halves(k_slc, lo64), _dup_halves(v_slc, lo64))):
        ksd_ref[:, hk * LANES:(hk + 1) * LANES] = kd.astype(BF16)
        vsd_ref[:, hk * LANES:(hk + 1) * LANES] = vd.astype(BF16)
    win_ref[:, 0:LANES] = _rope(_group_norm(piece(kv_ref, 4), gain(3), gmat), cos, sin, lo32)
    win_ref[:, LANES:2 * LANES] = piece(kv_ref, 5)
    for p in range(4):
        dq = _rope(_group_norm(piece(dq_ref, p), gain(4), gmat), cos, sin, lo32)
        dqo_ref[:, p * LANES:(p + 1) * LANES] = (dq * Q_SCALE).astype(BF16)
        dk = _rope(_group_norm(piece(dk_ref, p), gain(5), gmat), cos, sin, lo32)
        drows_ref[:, p, :] = dk
        dkb_ref[:, p * LANES:(p + 1) * LANES] = dk.astype(BF16)
        drows_ref[:, DIFF_HEADS + p, :] = piece(dv_ref, p)
    dvb_ref[...] = dv_ref[...].astype(BF16)


def qk_prep(z, cos, sin, gains, gmat, tm, tiles_per_seq, layer, n_layers, drows_prev):
    t = z.shape[0]

    def zs(width, col):
        return pl.BlockSpec((tm, width), lambda i: (i, col // width))

    def os_(width):
        return pl.BlockSpec((tm, width), lambda i: (i, 0))

    tab = pl.BlockSpec((tm, LANES), lambda i: (i % tiles_per_seq, 0))
    outs = ((512, F32), (512, BF16), (512, F32), (256, F32), (512, BF16), None,
            (256, BF16), (256, BF16), (512, BF16), (512, BF16))
    drows_shape = jax.ShapeDtypeStruct((n_layers, t, 2 * DIFF_HEADS, LANES), F32)
    drows_spec = pl.BlockSpec((None, tm, 2 * DIFF_HEADS, LANES), lambda i: (layer, i, 0, 0))
    in_specs = [zs(768, C_KV), zs(512, C_NQ), zs(512, C_DQ), zs(512, C_DK), zs(512, C_DV), tab, tab,
                pl.BlockSpec((8, LANES), lambda i: (0, 0)), pl.BlockSpec((LANES, LANES), lambda i: (0, 0))]
    args = [z, z, z, z, z, cos, sin, gains, gmat]
    kern, aliases = _prep_kernel, {}
    if drows_prev is not None:
        in_specs.append(pl.BlockSpec(memory_space=pl.ANY))
        args.append(drows_prev)
        aliases = {len(args) - 1: 5}
        kern = _prep_kernel_aliased
    return pl.pallas_call(
        kern,
        grid=(t // tm,),
        in_specs=in_specs,
        out_specs=[drows_spec if o is None else os_(o[0]) for o in outs],
        out_shape=[drows_shape if o is None else jax.ShapeDtypeStruct((t, o[0]), o[1]) for o in outs],
        input_output_aliases=aliases,
        compiler_params=_cparams(("arbitrary",)),
        name="qk_prep",
    )(*args)


def _prep_kernel_aliased(*refs):
    _prep_kernel(*refs[:9], *refs[10:])


def _conv_kernel(ab_ref, ac_ref, ax_ref, glu_ref, pa_ref, pd_ref, wa_ref, ba_ref, wd_ref, bd_ref, lg_ref, lb_ref,
                 ua_ref, ud_ref, sa_ref, sd_ref, bufa, bufd, *, tm, ha, hd, stride, nt):
    i = pl.program_id(1)

    @pl.when(i == 0)
    def _():
        bufa[0:ha, :] = pa_ref[...]
        bufd[0:hd, :] = pd_ref[...]

    if nt > 1:
        @pl.when(i > 0)
        def _():
            bufa[0:ha, :] = bufa[tm:tm + ha, :]
            bufd[0:hd, :] = bufd[tm:tm + hd, :]

    bufa[ha:ha + tm, :] = ac_ref[...] * ax_ref[...]
    ya = ba_ref[...]
    for k in range(A_CONV):
        ya = ya + wa_ref[k:k + 1, :] * bufa[pl.ds(ha - (A_CONV - 1 - k) * stride, tm), :]
    ua_ref[...] = (ab_ref[...] * ya).astype(ua_ref.dtype)

    glu = glu_ref[...]
    bufd[hd:hd + tm, :] = glu[:, :D_WIDTH] * _sigmoid(glu[:, D_WIDTH:])
    yd = bd_ref[...]
    for k in range(D_CONV):
        yd = yd + wd_ref[k:k + 1, :] * bufd[pl.ds(hd - (D_CONV - 1 - k) * stride, tm), :]
    mu = jnp.mean(yd, axis=-1, keepdims=True)
    var = jnp.mean(jnp.square(yd - mu), axis=-1, keepdims=True)
    y = (yd - mu) * lax.rsqrt(var + NORM_EPS) * lg_ref[...] + lb_ref[...]
    ud_ref[...] = _silu(y).astype(ud_ref.dtype)

    @pl.when(i == nt - 1)
    def _():
        sa_ref[...] = bufa[ha + tm - (A_CONV - 1) * stride:ha + tm, :]
        sd_ref[...] = bufd[hd + tm - (D_CONV - 1) * stride:hd + tm, :]


def conv_branches(z, past_a, past_d, wa, ba, wd, bd, lg, lb, nseq, tm, stride):
    t = z.shape[0]
    nt = t // nseq // tm
    ha, hd = past_a.shape[1], past_d.shape[1]
    if nt > 1:
        assert tm >= ha and tm >= hd

    def zs(width, col):
        return pl.BlockSpec((tm, width), lambda s, i: (s * nt + i, col // width))

    def cs(shape):
        return pl.BlockSpec(shape, lambda s, i: (0, 0))

    sa_rows, sd_rows = (A_CONV - 1) * stride, (D_CONV - 1) * stride
    return pl.pallas_call(
        functools.partial(_conv_kernel, tm=tm, ha=ha, hd=hd, stride=stride, nt=nt),
        grid=(nseq, nt),
        in_specs=[zs(512, C_AB), zs(512, C_AC), zs(512, C_AX), zs(1024, C_GLU),
                  pl.BlockSpec((None, ha, A_WIDTH), lambda s, i: (s, 0, 0)),
                  pl.BlockSpec((None, hd, D_WIDTH), lambda s, i: (s, 0, 0)),
                  cs((A_CONV, A_WIDTH)), cs((1, A_WIDTH)), cs((D_CONV, D_WIDTH)), cs((1, D_WIDTH)),
                  cs((1, D_WIDTH)), cs((1, D_WIDTH))],
        out_specs=[pl.BlockSpec((tm, A_WIDTH), lambda s, i: (s * nt + i, 0)),
                   pl.BlockSpec((tm, D_WIDTH), lambda s, i: (s * nt + i, 0)),
                   pl.BlockSpec((None, sa_rows, A_WIDTH), lambda s, i: (s, 0, 0)),
                   pl.BlockSpec((None, sd_rows, D_WIDTH), lambda s, i: (s, 0, 0))],
        out_shape=[jax.ShapeDtypeStruct((t, A_WIDTH), BF16), jax.ShapeDtypeStruct((t, D_WIDTH), BF16),
                   jax.ShapeDtypeStruct((nseq, sa_rows, A_WIDTH), F32),
                   jax.ShapeDtypeStruct((nseq, sd_rows, D_WIDTH), F32)],
        scratch_shapes=[pltpu.VMEM((ha + tm, A_WIDTH), F32), pltpu.VMEM((hd + tm, D_WIDTH), F32)],
        compiler_params=_cparams(("arbitrary", "arbitrary")),
        name="conv_branches",
    )(z, z, z, z, past_a, past_d, wa, ba, wd, bd, lg, lb)


def _merge_kernel(x_ref, gt_ref, ua_ref, uc_ref, ud_ref, oc_ref, os_ref, ow_ref, ng_ref, ex_ref,
                  m0_ref, m1_ref, m2_ref, m3_ref, wa_ref, wb_ref, wc_ref, wd_ref, wo_ref, o_ref):
    tm = x_ref.shape[0]
    g = _sigmoid(ng_ref[...])
    g_hi = g.astype(BF16)
    g_lo = (g - g_hi.astype(F32)).astype(BF16)
    gx = _dot(g_hi, ex_ref[...]) + _dot(g_lo, ex_ref[...])
    ub = (gx[:, 0:512] * oc_ref[...] + gx[:, 512:1024] * os_ref[...] + gx[:, 1024:1536] * ow_ref[...]).astype(BF16)
    merged = _sigmoid(m0_ref[...]) * _dot(ua_ref[...], wa_ref[...])
    merged += _sigmoid(m1_ref[...]) * _dot(ub, wb_ref[...])
    merged += _sigmoid(m2_ref[...]) * _dot(uc_ref[...], wc_ref[...])
    merged += _sigmoid(m3_ref[...]) * _dot(ud_ref[...], wd_ref[...])
    y = _dot(merged.astype(BF16), wo_ref[...])
    o_ref[...] = x_ref[...] + _rows_to(gt_ref[...], tm) * y


def merge_out(x, mod, z, ua, uc, ud, o_cmp, o_slc, o_win, gate_ex, w_a, w_b, w_c, w_d, w_o, tm, tiles_per_group):
    t, d = x.shape

    def ts(width):
        return pl.BlockSpec((tm, width), lambda i: (i, 0))

    def zs(width, col):
        return pl.BlockSpec((tm, width), lambda i: (i, col // width))

    def ws(shape):
        return pl.BlockSpec(shape, lambda i: (0, 0))

    return pl.pallas_call(
        _merge_kernel,
        grid=(t // tm,),
        in_specs=[ts(d), _mod_spec(mod, 5, tiles_per_group, 1), ts(512), ts(512), ts(512), ts(512), ts(512), ts(512),
                  zs(LANES, C_NG), ws(gate_ex.shape),
                  zs(1024, C_MERGE), zs(1024, C_MERGE + 1024), zs(1024, C_MERGE + 2048), zs(1024, C_MERGE + 3072),
                  ws((512, d)), ws((512, d)), ws((512, d)), ws((512, d)), ws((d, d))],
        out_specs=ts(d),
        out_shape=jax.ShapeDtypeStruct((t, d), F32),
        compiler_params=_cparams(("arbitrary",)),
        name="merge_out",
    )(x, mod, ua, uc, ud, o_cmp, o_slc, o_win, z, gate_ex, z, z, z, z, w_a, w_b, w_c, w_d, w_o)


def _compress_kernel(r_ref, w_ref, o_ref):
    x = r_ref[...]
    nb = x.shape[0] // NSA_BLOCK
    o_ref[...] = jnp.sum(x.reshape(nb, NSA_BLOCK, x.shape[1]) * w_ref[...][None], axis=1)


def nsa_compress(rows, w2, tm=512):
    t = rows.shape[0]
    return pl.pallas_call(
        _compress_kernel,
        grid=(t // tm,),
        in_specs=[pl.BlockSpec((tm, 256), lambda i: (i, 0)), pl.BlockSpec((NSA_BLOCK, 256), lambda i: (0, 0))],
        out_specs=pl.BlockSpec((tm // NSA_BLOCK, 256), lambda i: (i, 0)),
        out_shape=jax.ShapeDtypeStruct((t // NSA_BLOCK, 256), F32),
        compiler_params=_cparams(("arbitrary",)),
        name="nsa_compress",
    )(rows, w2)


def _masked_softmax(s, mask):
    s = jnp.where(mask, s, NEG)
    p = jnp.exp(s - jnp.max(s, axis=-1, keepdims=True))
    p = jnp.where(mask, p, 0.0)
    return p / jnp.maximum(jnp.sum(p, axis=-1, keepdims=True), 1e-30)


def _top_blocks(imp, n_sel):
    imp = imp.T
    idx = lax.broadcasted_iota(jnp.int32, imp.shape, 0)
    big = imp.shape[0]
    sel = jnp.zeros(imp.shape, F32)
    for _ in range(n_sel):
        m = jnp.max(imp, axis=0, keepdims=True)
        first = jnp.min(jnp.where(imp == m, idx, big), axis=0, keepdims=True)
        hit = idx == first
        sel = jnp.where(hit, 1.0, sel)
        imp = jnp.where(hit, -3.0, imp)
    return sel.T


def _top_blocks_by_rank(imp, n_sel):
    nb = imp.shape[1]
    idx = lax.broadcasted_iota(jnp.int32, imp.shape, 1)
    rank = jnp.zeros(imp.shape, F32)
    for d in range(1, nb):
        other = pltpu.roll(imp, d, 1)
        beats = (other > imp) | ((other == imp) & (idx >= d))
        rank = rank + jnp.where(beats, 1.0, 0.0)
    return jnp.where(rank < n_sel, 1.0, 0.0)


def _cmp_select(q, kdup, vdup, qpos, nb, n_sel):
    tq = q.shape[0]
    qlo = _lane_lo((tq, LANES))
    blk = lax.broadcasted_iota(jnp.int32, (1, nb), 1)
    complete = (blk + 1) * NSA_BLOCK - 1 <= qpos
    cur = qpos // NSA_BLOCK
    outs, sels = [], []
    for hk in range(NSA_KV_HEADS):
        imp = jnp.zeros((tq, nb), F32)
        vd = vdup[hk].astype(BF16)
        for a in (2 * hk, 2 * hk + 1):
            qp = q[:, a * LANES:(a + 1) * LANES]
            pair = []
            for par in range(2):
                qm = jnp.where(qlo if par == 0 else ~qlo, qp, 0.0)
                s = _dot_nt(qm, kdup[hk], precision=lax.Precision.HIGHEST)
                pc = _masked_softmax(s * ATT_SCALE, complete)
                imp = imp + pc
                pair.append(_dot(pc.astype(BF16), vd))
            outs.append(jnp.where(qlo, pair[0], pair[1]))
        forced = (blk == cur) | (blk == 0)
        imp = jnp.where(forced, NSA_GROUP + 1.0, imp)
        imp = jnp.where(blk <= cur, imp, -1.0)
        sels.append(_top_blocks(imp, n_sel))
    return jnp.concatenate(outs, axis=1), sels


def _cmpsel_kernel(q_ref, c_ref, o_ref, sel_ref, *, tq, n_sel):
    qpos = pl.program_id(1) * tq + lax.broadcasted_iota(jnp.int32, (tq, 1), 0)
    nb = c_ref.shape[0]
    lo = _lane_lo((nb, LANES))
    kdup = _dup_halves(c_ref[:, 0:LANES], lo)
    vdup = _dup_halves(c_ref[:, LANES:2 * LANES], lo)
    o, sels = _cmp_select(q_ref[...], kdup, vdup, qpos, nb, n_sel)
    o_ref[...] = o
    sel_ref[...] = jnp.concatenate(sels, axis=1).astype(sel_ref.dtype)


def nsa_cmp_select(q_n, cmp, nseq, tq=512):
    t = q_n.shape[0]
    nq = t // nseq // tq
    nb = cmp.shape[0] // nseq
    return pl.pallas_call(
        functools.partial(_cmpsel_kernel, tq=tq, n_sel=min(NSA_TOPK, nb)),
        grid=(nseq, nq),
        in_specs=[pl.BlockSpec((tq, 512), lambda b, i: (b * nq + i, 0)),
                  pl.BlockSpec((nb, 256), lambda b, i: (b, 0))],
        out_specs=[pl.BlockSpec((tq, 512), lambda b, i: (b * nq + i, 0)),
                   pl.BlockSpec((tq, 2 * nb), lambda b, i: (b * nq + i, 0))],
        out_shape=[jax.ShapeDtypeStruct((t, 512), F32), jax.ShapeDtypeStruct((t, 2 * nb), BF16)],
        compiler_params=_cparams(("arbitrary", "arbitrary")),
        name="nsa_cmp_select",
    )(q_n, cmp)


def _online_step(s, m, acc, v_ext, exp_dtype=F32):
    tk = s.shape[1]
    m_new = jnp.maximum(m, jnp.max(s, axis=-1, keepdims=True))
    x = s - jnp.concatenate([m_new] * (tk // LANES), axis=1)
    p = jnp.exp2(x.astype(exp_dtype)).astype(BF16)
    alpha = jnp.exp2(m - m_new)
    if acc.shape[1] != LANES:
        alpha = jnp.concatenate([alpha] * (acc.shape[1] // LANES), axis=1)
    return m_new, alpha * acc + _dot(p, v_ext)


GROWTH_LIMIT = 64.0


def _single_pass_step(s, ref, acc, growth, v_ext):
    tk = s.shape[1]
    p = jnp.exp2(s - jnp.concatenate([ref] * (tk // LANES), axis=1)).astype(BF16)
    row_max = jnp.max(s, axis=-1, keepdims=True)
    growth = jnp.maximum(growth, row_max - ref)
    ref_new = jnp.maximum(ref, row_max)
    alpha = jnp.exp2(ref - ref_new)
    if acc.shape[1] != LANES:
        alpha = jnp.concatenate([alpha] * (acc.shape[1] // LANES), axis=1)
    return ref_new, (acc + _dot(p, v_ext)) * alpha, growth


def _lane_bcast0(x):
    return jnp.broadcast_to(x[:, 0:1], x.shape)


def _slc_kernel(q_ref, k_ref, v_ref, sel_ref, e_ref, o_ref, *, tq, tk, nb):
    qi = pl.program_id(1)
    n_full = (qi * tq) // tk
    rows = NSA_GROUP * tq
    qlo = _lane_lo((tq, LANES))
    klo = _lane_lo((tk, LANES))
    zero = jnp.zeros((tq, LANES), BF16)
    qpos = qi * tq + lax.broadcasted_iota(jnp.int32, (tq, 1), 0)
    col = lax.broadcasted_iota(jnp.int32, (1, tk), 1)
    for hk in range(NSA_KV_HEADS):
        parts = []
        for c in range(NSA_GROUP):
            h = hk * NSA_GROUP + c
            qp = q_ref[:, (h // 2) * LANES:(h // 2 + 1) * LANES]
            parts.append(jnp.where(qlo, qp, zero) if h % 2 == 0 else jnp.where(qlo, zero, qp))
        qst = jnp.concatenate(parts, axis=0)
        sel = sel_ref[:, hk * nb:(hk + 1) * nb]

        def tile(j, diag, qst=qst, sel=sel, hk=hk):
            off = pl.multiple_of(j * tk, tk)
            k = k_ref[pl.ds(off, tk), hk * LANES:(hk + 1) * LANES]
            v = v_ref[pl.ds(off, tk), hk * LANES:(hk + 1) * LANES]
            bias = (_dot(sel, e_ref[j]) - 1.0) * (-NEG)
            if diag:
                bias = jnp.where(j * tk + col <= qpos, bias, NEG)
            s = _dot_nt(qst, k) + jnp.concatenate([bias] * NSA_GROUP, axis=0)
            return s, jnp.where(klo, v, jnp.ones_like(v))

        def finish(acc, hk=hk):
            o = acc / jnp.maximum(pltpu.roll(acc, HEAD_DIM, 1), 1e-30)
            for pair in range(NSA_GROUP // 2):
                a = hk * (NSA_GROUP // 2) + pair
                o_even = o[(2 * pair) * tq:(2 * pair + 1) * tq]
                o_odd = o[(2 * pair + 1) * tq:(2 * pair + 2) * tq]
                o_ref[:, a * LANES:(a + 1) * LANES] = jnp.where(qlo, o_even, pltpu.roll(o_odd, HEAD_DIM, 1))

        def body(j, carry, tile=tile):
            s, vx = tile(j, False)
            return _online_step(s, *carry, vx, exp_dtype=BF16)

        init = (jnp.full((rows, LANES), NEG, F32), jnp.zeros((rows, LANES), F32))
        m, acc = lax.fori_loop(0, n_full, body, init)
        s, vx = tile(n_full, True)
        finish(_online_step(s, m, acc, vx, exp_dtype=BF16)[1])


def nsa_slc(q_rot, ksd, vsd, sel, emat, nseq, tq=256):
    t = q_rot.shape[0]
    s = t // nseq
    nq = s // tq
    nk, nb, tk = emat.shape
    return pl.pallas_call(
        functools.partial(_slc_kernel, tq=tq, tk=tk, nb=nb),
        grid=(nseq, nq),
        in_specs=[pl.BlockSpec((tq, 512), lambda b, i: (b * nq + i, 0)),
                  pl.BlockSpec((s, 256), lambda b, i: (b, 0)),
                  pl.BlockSpec((s, 256), lambda b, i: (b, 0)),
                  pl.BlockSpec((tq, 2 * nb), lambda b, i: (b * nq + i, 0)),
                  pl.BlockSpec((nk, nb, tk), lambda b, i: (0, 0, 0))],
        out_specs=pl.BlockSpec((tq, 512), lambda b, i: (b * nq + i, 0)),
        out_shape=jax.ShapeDtypeStruct((t, 512), F32),
        compiler_params=_cparams(("arbitrary", "arbitrary")),
        name="nsa_slc",
    )(q_rot, ksd, vsd, sel, emat)


def _win_kernel(q_ref, wp_ref, wc_ref, o_ref, *, tq):
    i = pl.program_id(1)
    lo = _lane_lo((tq, LANES))
    qpos = i * tq + lax.broadcasted_iota(jnp.int32, (tq, 1), 0)
    chunks = []
    for off, ref in ((-tq, wp_ref), (0, wc_ref)):
        kpos = i * tq + off + lax.broadcasted_iota(jnp.int32, (1, tq), 1)
        bias = jnp.where((kpos >= 0) & (kpos <= qpos) & (qpos - kpos <= NSA_WINDOW), 0.0, NEG)
        kd = [k.astype(BF16) for k in _dup_halves(ref[:, 0:LANES], lo)]
        vx = [jnp.where(lo, v, 1.0).astype(BF16) for v in _dup_halves(ref[:, LANES:2 * LANES], lo)]
        chunks.append((bias, kd, vx))
    zero = jnp.zeros((tq, LANES), BF16)
    for a in range(NSA_HEADS // 2):
        pair = []
        for par in range(2):
            hk = (2 * a + par) // NSA_GROUP
            qp = q_ref[:, a * LANES:(a + 1) * LANES]
            qm = jnp.where(lo, qp, zero) if par == 0 else jnp.where(lo, zero, qp)
            ss = [_dot_nt(qm, kd[hk]) + bias for bias, kd, _ in chunks]
            m = jnp.maximum(jnp.max(ss[0], axis=-1, keepdims=True), jnp.max(ss[1], axis=-1, keepdims=True))
            acc = _dot(jnp.exp2(ss[0] - m).astype(BF16), chunks[0][2][hk])
            acc += _dot(jnp.exp2(ss[1] - m).astype(BF16), chunks[1][2][hk])
            pair.append(acc / jnp.maximum(pltpu.roll(acc, HEAD_DIM, 1), 1e-30))
        o_ref[:, a * LANES:(a + 1) * LANES] = jnp.where(lo, pair[0], pltpu.roll(pair[1], HEAD_DIM, 1))


def nsa_win(q_rot, win_rows, nseq, tq=NSA_WINDOW):
    t = q_rot.shape[0]
    nq = t // nseq // tq
    return pl.pallas_call(
        functools.partial(_win_kernel, tq=tq),
        grid=(nseq, nq),
        in_specs=[pl.BlockSpec((tq, 512), lambda b, i: (b * nq + i, 0)),
                  pl.BlockSpec((tq, 256), lambda b, i: (b * nq + jnp.maximum(i - 1, 0), 0)),
                  pl.BlockSpec((tq, 256), lambda b, i: (b * nq + i, 0))],
        out_specs=pl.BlockSpec((tq, 512), lambda b, i: (b * nq + i, 0)),
        out_shape=jax.ShapeDtypeStruct((t, 512), F32),
        compiler_params=_cparams(("arbitrary", "arbitrary")),
        name="nsa_win",
    )(q_rot, win_rows, win_rows)


def _diff_lambda(lam_ref, lam_init):
    lp = lam_ref[...]
    a = jnp.sum(lp[0:1] * lp[1:2], axis=-1, keepdims=True)
    b = jnp.sum(lp[2:3] * lp[3:4], axis=-1, keepdims=True)
    return jnp.exp(a) - jnp.exp(b) + lam_init


def _diff_finish(o1, o2, lam, g, lam_init):
    o = o1 - lam * o2
    ms = jnp.mean(o * o, axis=-1, keepdims=True)
    return o * lax.rsqrt(ms + NORM_EPS) * g * (1.0 - lam_init)


def _diff_kernel(q_ref, k_ref, v_ref, lam_ref, g_ref, o_ref, *, tq, tk, lam_init):
    qi = pl.program_id(2)
    n_full = (qi * tq) // tk
    lo = _lane_lo((tq, LANES))
    q = q_ref[...]
    zero = jnp.zeros_like(q)
    qst = jnp.concatenate([jnp.where(lo, q, zero), jnp.where(lo, zero, q)], axis=0)
    qpos = qi * tq + lax.broadcasted_iota(jnp.int32, (tq, 1), 0)
    col = lax.broadcasted_iota(jnp.int32, (1, tk), 1)
    bias = jnp.where(n_full * tk + col <= qpos, 0.0, NEG)
    bias = jnp.concatenate([bias, bias], axis=0)
    ones = jnp.ones((tk, LANES), BF16)

    def tile(j):
        off = pl.multiple_of(j * tk, tk)
        return _dot_nt(qst, k_ref[pl.ds(off, tk), :]), jnp.concatenate([v_ref[pl.ds(off, tk), :], ones], axis=1)

    def walk(step, init):
        def full(j, carry):
            s, vx = tile(j)
            return step(s, carry, vx)

        carry = lax.fori_loop(0, n_full, full, init)
        s, vx = tile(n_full)
        return step(s + bias, carry, vx)

    def finish(acc):
        o = acc[:, 0:LANES] / jnp.maximum(acc[:, LANES:2 * LANES], 1e-30)
        lam = _diff_lambda(lam_ref, lam_init)
        o_ref[...] = _diff_finish(o[0:tq], o[tq:2 * tq], lam, g_ref[...], lam_init).astype(o_ref.dtype)

    ref0 = _lane_bcast0(_dot_nt(qst, k_ref[0:LANES, :]))
    init = (ref0, jnp.zeros((2 * tq, 2 * LANES), F32), jnp.zeros((2 * tq, LANES), F32))
    _, acc, growth = walk(lambda s, carry, vx: _single_pass_step(s, *carry, vx), init)
    fast_ok = jnp.max(growth) <= GROWTH_LIMIT

    @pl.when(fast_ok)
    def _():
        finish(acc)

    @pl.when(jnp.logical_not(fast_ok))
    def _():
        init = (jnp.full((2 * tq, LANES), NEG, F32), jnp.zeros((2 * tq, 2 * LANES), F32))
        finish(walk(lambda s, carry, vx: _online_step(s, *carry, vx), init)[1])


def diff_attn(dq, dk, dv, lam_p, out_g, nseq, lam_init, tq=256, tk=1024):
    t = dq.shape[0]
    s = t // nseq
    nq = s // tq
    return pl.pallas_call(
        functools.partial(_diff_kernel, tq=tq, tk=tk, lam_init=lam_init),
        grid=(nseq, DIFF_HEADS, nq),
        in_specs=[pl.BlockSpec((tq, LANES), lambda b, h, i: (b * nq + i, h)),
                  pl.BlockSpec((s, LANES), lambda b, h, i: (b, h)),
                  pl.BlockSpec((s, LANES), lambda b, h, i: (b, h)),
                  pl.BlockSpec((4, HEAD_DIM), lambda b, h, i: (0, 0)),
                  pl.BlockSpec((1, LANES), lambda b, h, i: (0, 0))],
        out_specs=pl.BlockSpec((tq, LANES), lambda b, h, i: (b * nq + i, h)),
        out_shape=jax.ShapeDtypeStruct((t, 512), BF16),
        compiler_params=_cparams(("arbitrary", "arbitrary", "arbitrary")),
        name="diff_attn",
    )(dq, dk, dv, lam_p, out_g)


QPAD = 8
SEQS_PER_STEP = 2


def _pad_keys(x):
    return jnp.concatenate([x, jnp.zeros((LANES - x.shape[0], x.shape[1]), x.dtype)], axis=0)


def _nsa_sample_kernel(pt_ref, *refs, n_pages, past_len, n_new, n_sel, n_seq):
    pages = refs[:n_seq * n_pages]
    new_ref, qn_ref, qr_ref, wst_ref, wnew_ref, wt_ref, selm_ref = refs[n_seq * n_pages:n_seq * n_pages + 7]
    oc_ref, os_ref, ow_ref, wout_ref = refs[-4:]
    for s in range(n_seq):
        chunks = tuple(pages[s * n_pages:(s + 1) * n_pages]) + (new_ref.at[s],)
        _nsa_sample_one(chunks, qn_ref.at[s], qr_ref.at[s], wst_ref.at[s], wnew_ref.at[s], wt_ref, selm_ref,
                        oc_ref.at[s], os_ref.at[s], ow_ref.at[s], wout_ref.at[s],
                        n_pages=n_pages, past_len=past_len, n_new=n_new, n_sel=n_sel)


def _nsa_sample_one(chunks, qn_ref, qr_ref, wst_ref, wnew_ref, wt_ref, selm_ref, oc_ref, os_ref, ow_ref, wout_ref,
                    *, n_pages, past_len, n_new, n_sel):
    n_real = (past_len + n_new + NSA_BLOCK - 1) // NSA_BLOCK
    nbp = selm_ref.shape[1]
    page = chunks[0].shape[3]

    def twice(x):
        return jnp.concatenate([x, x], axis=0)

    wt = wt_ref[...]
    y = jnp.concatenate([ch[0:2].reshape(4 * HEAD_DIM, page) * wt for ch in chunks], axis=1)
    hi = y.astype(BF16)
    selm = selm_ref[...]
    cmp_t = _dot(hi, selm)
    yk = y[0:2 * HEAD_DIM]
    cmp_k = cmp_t[0:2 * HEAD_DIM] + _dot((yk - hi[0:2 * HEAD_DIM].astype(F32)).astype(BF16), selm)

    tok = lax.broadcasted_iota(jnp.int32, (QPAD, 1), 0)
    qpos = past_len + tok
    rows = NSA_GROUP * QPAD
    qpos_st = jnp.concatenate([qpos] * NSA_GROUP, axis=0)
    qlo = _lane_lo((QPAD, LANES))

    def stack_group(q_ref, hk, zero):
        parts = []
        for c in range(NSA_GROUP):
            h = hk * NSA_GROUP + c
            qp = q_ref[:, (h // 2) * LANES:(h // 2 + 1) * LANES]
            parts.append(jnp.where(qlo, qp, zero) if h % 2 == 0 else jnp.where(qlo, zero, qp))
        return jnp.concatenate(parts, axis=0)

    def emit(o_ref, per_hk):
        for a in range(NSA_HEADS // 2):
            hk, c = (2 * a) // NSA_GROUP, (2 * a) % NSA_GROUP
            o = per_hk[hk]
            o_ref[:, a * LANES:(a + 1) * LANES] = jnp.where(qlo, o[c * QPAD:(c + 1) * QPAD], o[(c + 1) * QPAD:(c + 2) * QPAD])

    blk = lax.broadcasted_iota(jnp.int32, (1, nbp), 1)
    complete = ((blk + 1) * NSA_BLOCK - 1 <= qpos_st) & (blk < n_real)
    cur = qpos // NSA_BLOCK
    per_hk, imps = [], []
    for hk in range(NSA_KV_HEADS):
        kd = twice(cmp_k[hk * HEAD_DIM:(hk + 1) * HEAD_DIM])
        vd = twice(cmp_t[(2 + hk) * HEAD_DIM:(3 + hk) * HEAD_DIM]).astype(BF16)
        s = jnp.dot(stack_group(qn_ref, hk, jnp.zeros((QPAD, LANES), F32)), kd,
                    precision=lax.Precision.HIGHEST, preferred_element_type=F32)
        pc = _masked_softmax(s * ATT_SCALE, complete)
        per_hk.append(_dot_nt(pc.astype(BF16), vd))
        imp = functools.reduce(lambda a, b: a + b, [pc[c * QPAD:(c + 1) * QPAD] for c in range(NSA_GROUP)])
        imp = jnp.where((blk == cur) | (blk == 0), NSA_GROUP + 1.0, imp)
        imp = jnp.where(blk <= cur, imp, -1.0)
        imps.append(jnp.where(blk < n_real, imp, -2.0))
    emit(oc_ref, per_hk)
    sel_all = _top_blocks_by_rank(jnp.concatenate(imps, axis=0), n_sel)
    sels = [sel_all[hk * QPAD:(hk + 1) * QPAD] for hk in range(NSA_KV_HEADS)]

    qst = [stack_group(qr_ref, hk, jnp.zeros((QPAD, LANES), BF16)) for hk in range(NSA_KV_HEADS)]
    key_new = lax.broadcasted_iota(jnp.int32, (1, LANES), 1)
    new_valid = (key_new < n_new) & (past_len + key_new <= qpos_st)

    def attend(kvs, biases, hk):
        k_all = jnp.concatenate([twice(k).astype(BF16) for k, _ in kvs], axis=1)
        v_all = jnp.concatenate([twice(v).astype(BF16) for _, v in kvs], axis=1)
        s = _dot(qst[hk], k_all) + jnp.concatenate(biases, axis=1)
        p = jnp.exp2(s - jnp.max(s, axis=-1, keepdims=True))
        p = p / jnp.maximum(jnp.sum(p, axis=-1, keepdims=True), 1e-30)
        return _dot_nt(p.astype(BF16), v_all)

    per_hk = []
    for hk in range(NSA_KV_HEADS):
        selx = _dot_nt(sels[hk].astype(BF16), selm)
        selx = jnp.concatenate([selx] * NSA_GROUP, axis=0)
        kvs, biases = [], []
        for j, ch in enumerate(chunks):
            kvs.append((ch[2, hk], ch[3, hk]))
            bias = (selx[:, j * page:(j + 1) * page] - 1.0) * (-NEG)
            biases.append(jnp.where(new_valid, bias, NEG) if j == n_pages else bias)
        per_hk.append(attend(kvs, biases, hk))
    emit(os_ref, per_hk)

    wbuf = wst_ref.shape[3]
    kpos = past_len - wbuf + lax.broadcasted_iota(jnp.int32, (1, wbuf), 1)
    bias_st = jnp.where((kpos >= 0) & (kpos <= qpos_st) & (qpos_st - kpos <= NSA_WINDOW), 0.0, NEG)
    bias_new = jnp.where(new_valid & (qpos_st - (past_len + key_new) <= NSA_WINDOW), 0.0, NEG)
    per_hk = []
    for hk in range(NSA_KV_HEADS):
        kvs = [(wst_ref[0, hk], wst_ref[1, hk]), (wnew_ref[0, hk], wnew_ref[1, hk])]
        per_hk.append(attend(kvs, [bias_st, bias_new], hk))
    emit(ow_ref, per_hk)

    lane = lax.broadcasted_iota(jnp.int32, (HEAD_DIM, wbuf), 1)
    for r in range(2):
        for hk in range(NSA_KV_HEADS):
            shifted = pltpu.roll(wst_ref[r, hk], wbuf - n_new, 1)
            tail = jnp.concatenate([jnp.zeros((HEAD_DIM, wbuf - LANES), F32),
                                    pltpu.roll(wnew_ref[r, hk], LANES - n_new, 1)], axis=1)
            wout_ref[r, hk] = jnp.where(lane >= wbuf - n_new, tail, shifted)


def nsa_sample(layer, page_table, cache_t, new_t, q_n, q_rot, win_t, win_new_t, wt, selm, n_new, wout_prev):
    nb_, n_pages = page_table.shape
    page = cache_t.shape[5]
    past_len = n_pages * page
    n_real = (past_len + n_new + NSA_BLOCK - 1) // NSA_BLOCK
    wbuf = win_t.shape[5]
    kv = (NSA_KV_HEADS, HEAD_DIM)

    n_seq = SEQS_PER_STEP

    def page_spec(s, j):
        return pl.BlockSpec((None, None, 4) + kv + (page,), lambda b, pt: (layer, pt[b * n_seq + s, j], 0, 0, 0, 0))

    def seq(shape):
        return pl.BlockSpec((n_seq,) + shape, lambda b, pt: (b,) + (0,) * len(shape))

    def const(shape):
        return pl.BlockSpec(shape, lambda b, pt: (0,) * len(shape))

    win_spec = pl.BlockSpec((None, n_seq, 2) + kv + (wbuf,), lambda b, pt: (layer, b, 0, 0, 0, 0))
    in_specs = [page_spec(s, j) for s in range(n_seq) for j in range(n_pages)] + [
        seq((4,) + kv + (page,)), seq((QPAD, 512)), seq((QPAD, 512)), win_spec,
        seq((2,) + kv + (LANES,)), const(wt.shape), const(selm.shape)]
    args = [page_table] + [cache_t] * (n_seq * n_pages) + [new_t, q_n, q_rot, win_t, win_new_t, wt, selm]
    aliases = {}
    if wout_prev is not None:
        in_specs.append(pl.BlockSpec(memory_space=pl.ANY))
        args.append(wout_prev)
        aliases = {len(args) - 1: 3}
    grid_spec = pltpu.PrefetchScalarGridSpec(
        num_scalar_prefetch=1,
        grid=(nb_ // n_seq,),
        in_specs=in_specs,
        out_specs=[seq((QPAD, 512)), seq((QPAD, 512)), seq((QPAD, 512)), win_spec],
    )
    return pl.pallas_call(
        functools.partial(_nsa_sample_kernel, n_pages=n_pages, past_len=past_len, n_new=n_new,
                          n_sel=min(NSA_TOPK, n_real), n_seq=n_seq),
        grid_spec=grid_spec,
        out_shape=[jax.ShapeDtypeStruct((nb_, QPAD, 512), F32)] * 3 + [jax.ShapeDtypeStruct(win_t.shape, F32)],
        input_output_aliases=aliases,
        compiler_params=_cparams(("arbitrary",)),
        name="nsa_sample",
    )(*args)


def _diff_sample_kernel(pt_ref, *refs, n_pages, past_len, n_new, lam_init, n_seq):
    new_ref, q_ref, lam_ref, g_ref, o_ref = refs[n_seq * n_pages:]
    for s in range(n_seq):
        _diff_sample_one(refs[s * n_pages:(s + 1) * n_pages], new_ref.at[s], q_ref.at[s], lam_ref, g_ref, o_ref.at[s],
                         n_pages=n_pages, n_new=n_new, lam_init=lam_init)


def _diff_sample_one(pages, new_ref, q_ref, lam_ref, g_ref, o_ref, *, n_pages, n_new, lam_init):
    lo = _lane_lo((QPAD, LANES))
    tok = lax.broadcasted_iota(jnp.int32, (2 * QPAD, 1), 0) % QPAD
    key_new = lax.broadcasted_iota(jnp.int32, (1, LANES), 1)
    new_valid = (key_new < n_new) & (key_new <= tok)
    lam = _diff_lambda(lam_ref, lam_init)
    q = q_ref[...]
    new = new_ref[...]
    zero = jnp.zeros((QPAD, LANES), BF16)
    page = pages[0].shape[0] // (2 * DIFF_HEADS)
    bias = jnp.concatenate([jnp.zeros((2 * QPAD, n_pages * page), F32), jnp.where(new_valid, 0.0, NEG)], axis=1)
    for h in range(DIFF_HEADS):
        qp = q[:, h * LANES:(h + 1) * LANES]
        qst = jnp.concatenate([jnp.where(lo, qp, zero), jnp.where(lo, zero, qp)], axis=0)
        ks = [pages[j][pl.ds(h, page, stride=2 * DIFF_HEADS), :] for j in range(n_pages)]
        ks.append(_pad_keys(new[:, h * LANES:(h + 1) * LANES]))
        vs = [pages[j][pl.ds(DIFF_HEADS + h, page, stride=2 * DIFF_HEADS), :] for j in range(n_pages)]
        vs.append(_pad_keys(new[:, (DIFF_HEADS + h) * LANES:(DIFF_HEADS + h + 1) * LANES]))
        k_all = jnp.concatenate([k.astype(BF16) for k in ks], axis=0)
        v_all = jnp.concatenate([v.astype(BF16) for v in vs], axis=0)
        s = _dot_nt(qst, k_all) + bias
        p = jnp.exp2(s - jnp.max(s, axis=-1, keepdims=True))
        p = p / jnp.maximum(jnp.sum(p, axis=-1, keepdims=True), 1e-30)
        o = _dot((p[0:QPAD] - lam * p[QPAD:2 * QPAD]).astype(BF16), v_all)
        ms = jnp.mean(o * o, axis=-1, keepdims=True)
        o_ref[:, h * LANES:(h + 1) * LANES] = (o * lax.rsqrt(ms + NORM_EPS) * g_ref[...] * (1.0 - lam_init)).astype(o_ref.dtype)


def diff_sample(layer, page_table, cache, new_rows, dq, lam_p, out_g, n_new, lam_init):
    nb_, n_pages = page_table.shape
    page = cache.shape[2] // (2 * DIFF_HEADS)

    n_seq = SEQS_PER_STEP

    def page_spec(s, j):
        return pl.BlockSpec((None, None, page * 2 * DIFF_HEADS, LANES),
                            lambda b, pt: (layer, pt[b * n_seq + s, j], 0, 0))

    grid_spec = pltpu.PrefetchScalarGridSpec(
        num_scalar_prefetch=1,
        grid=(nb_ // n_seq,),
        in_specs=[page_spec(s, j) for s in range(n_seq) for j in range(n_pages)] + [
            pl.BlockSpec((n_seq, QPAD, 1024), lambda b, pt: (b, 0, 0)),
            pl.BlockSpec((n_seq, QPAD, 512), lambda b, pt: (b, 0, 0)),
            pl.BlockSpec((4, HEAD_DIM), lambda b, pt: (0, 0)),
            pl.BlockSpec((1, LANES), lambda b, pt: (0, 0))],
        out_specs=pl.BlockSpec((n_seq, QPAD, 512), lambda b, pt: (b, 0, 0)),
    )
    return pl.pallas_call(
        functools.partial(_diff_sample_kernel, n_pages=n_pages, past_len=n_pages * page, n_new=n_new,
                          lam_init=lam_init, n_seq=n_seq),
        grid_spec=grid_spec,
        out_shape=jax.ShapeDtypeStruct((nb_, QPAD, 512), F32),
        compiler_params=_cparams(("arbitrary",)),
        name="diff_sample",
    )(page_table, *([cache] * (n_seq * n_pages)), new_rows, dq, lam_p, out_g)


def _mix_weight(w):
    d = w.shape[0]
    splits = (A_WIDTH, A_WIDTH, A_WIDTH, 512, 768, 3 * NSA_HEADS, 512, 512, 512, 2 * D_WIDTH, N_BRANCH * d)
    offs = [0]
    for s in splits:
        offs.append(offs[-1] + s)
    part = {n: w[:, offs[i]:offs[i + 1]] for i, n in enumerate(
        ("ab", "ac", "ax", "nq", "kv", "g", "dq", "dk", "dv", "glu", "merge"))}
    pad = jnp.zeros((d, C_MERGE - C_NG - 3 * NSA_HEADS), w.dtype)
    cols = [part["kv"], part["g"], pad, part["merge"], part["glu"], part["ab"], part["ac"], part["ax"], part["nq"],
            part["dq"], part["dk"], part["dv"]]
    out = jnp.concatenate(cols, axis=1).astype(BF16)
    assert out.shape[1] == MIX_W
    return out


def _gate_expansion():
    r = jnp.arange(LANES)[:, None]
    c = jnp.arange(3 * NSA_HEADS * HEAD_DIM)[None, :]
    branch, head = c // (NSA_HEADS * HEAD_DIM), (c % (NSA_HEADS * HEAD_DIM)) // HEAD_DIM
    return (r == head * 3 + branch).astype(BF16)


def _rope_tables(pos):
    half = HEAD_DIM // 2
    inv = ROPE_THETA ** (-jnp.arange(half, dtype=F32) / half)
    ang = pos.astype(F32)[:, None] * inv[None, :]
    cos, sin = jnp.cos(ang), jnp.sin(ang)
    return jnp.tile(jnp.concatenate([cos, cos], axis=1), (1, 2)), jnp.tile(jnp.concatenate([-sin, sin], axis=1), (1, 2))


def _tile_gain(g):
    return jnp.tile(g.astype(F32), LANES // HEAD_DIM)


def _expansion(n_tiles, nb_rows, tk):
    j = jnp.arange(n_tiles)[:, None, None]
    n = jnp.arange(nb_rows)[None, :, None]
    k = jnp.arange(tk)[None, None, :]
    return ((j * tk + k) // NSA_BLOCK == n).astype(BF16)


def _tokens_last(x, nb_, n_new, lead):
    y = jnp.moveaxis(x.reshape((n_new, nb_) + lead), 0, -1)
    return jnp.pad(y, [(0, 0)] * (y.ndim - 1) + [(0, LANES - n_new)])


def _to_seq_major(x, nb_, n_new, pad_to=None):
    y = jnp.swapaxes(x.reshape(n_new, nb_, x.shape[-1]), 0, 1)
    if pad_to is not None and pad_to > n_new:
        y = jnp.pad(y, ((0, 0), (0, pad_to - n_new), (0, 0)))
    return y


def _to_step_major(y, n_new):
    return jnp.swapaxes(y[:, :n_new], 0, 1).reshape(-1, y.shape[-1])


def kernel(x_prompt, x_sample, cache_nsa, cache_diff, state_nsa_win, state_conv_a, state_conv_d, page_table,
           c_prompt, c_sample, ada_w, ada_b, norm_g, ffn1_up, ffn1_down, ffn2_up, ffn2_down, w_mix_in,
           conv_a_w, conv_a_b, a_out, nsa_qk_g, nsa_cmp_w, nsa_out, diff_qk_g, diff_lambda, diff_out_g, diff_out,
           conv_d_w, conv_d_b, d_norm_g, d_norm_b, d_out, w_out):
    bp, sp, d = x_prompt.shape
    bs, ns, _ = x_sample.shape
    depth = ada_w.shape[0]
    n_pages = page_table.shape[1]
    page = cache_nsa.shape[2]
    past_len = n_pages * page
    wbuf = state_nsa_win.shape[2]

    rows = bp + bs
    rpad = -(-rows // 8) * 8
    c_all = jnp.pad(jnp.concatenate([c_prompt, c_sample], axis=0), ((0, rpad - rows), (0, 0)))
    ada = ada_proj(c_all, ada_w, ada_b)

    xp = x_prompt.reshape(bp * sp, d)
    xs = jnp.swapaxes(x_sample, 0, 1).reshape(ns * bs, d)
    tm_p, tm_s = 512, ns * bs
    tpg_p = sp // tm_p

    cos_p, sin_p = _rope_tables(jnp.arange(sp))
    cos_s, sin_s = _rope_tables(jnp.repeat(past_len + jnp.arange(ns), bs))
    gmat = (jnp.arange(LANES)[:, None] // HEAD_DIM == jnp.arange(LANES)[None, :] // HEAD_DIM).astype(BF16)
    gate_ex = _gate_expansion()
    tk_slc = 512
    emat_p = _expansion(sp // tk_slc, sp // NSA_BLOCK, tk_slc)
    n_real_s = (past_len + ns + NSA_BLOCK - 1) // NSA_BLOCK
    nbp_s = -(-n_real_s // LANES) * LANES
    selm_s = _expansion(1, nbp_s, past_len + page)[0].T
    cache_nsa_t = jnp.transpose(cache_nsa, (0, 1, 3, 4, 5, 2))
    win_state_t = jnp.transpose(state_nsa_win, (0, 1, 3, 4, 5, 2))
    cache_diff2 = cache_diff.reshape(depth, cache_diff.shape[1], page * 2 * DIFF_HEADS, 2 * HEAD_DIM)

    zeros_a = jnp.zeros((bp, 8, A_WIDTH), F32)
    zeros_d = jnp.zeros((bp, 32, D_WIDTH), F32)

    outs = {k: [] for k in ("nsa_p", "nsa_s", "win_p", "ca_p", "ca_s", "cd_p", "cd_s")}
    drows_p = drows_s = win_out_t = None
    for l in range(depth):
        lam_init = 0.8 - 0.6 * math.exp(-0.3 * l)
        mod_p = ada[l, :, 0:bp].reshape(N_ADA, bp, 1, d)
        mod_s = ada[l, :, bp:bp + bs].reshape(N_ADA, 1, bs, d)
        g = norm_g[l]
        w1u, w1d = ffn1_up[l].astype(BF16), ffn1_down[l].astype(BF16)
        w2u, w2d = ffn2_up[l].astype(BF16), ffn2_down[l].astype(BF16)
        wmix = _mix_weight(w_mix_in[l])
        gains = jnp.stack([_tile_gain(nsa_qk_g[l, 0]), _tile_gain(nsa_qk_g[l, 1]), _tile_gain(nsa_qk_g[l, 2]),
                           _tile_gain(nsa_qk_g[l, 3]), _tile_gain(diff_qk_g[l, 0]), _tile_gain(diff_qk_g[l, 1]),
                           jnp.zeros((LANES,), F32), jnp.zeros((LANES,), F32)])
        cw = nsa_cmp_w[l]
        w2 = jnp.concatenate([cw[0], cw[0], cw[1], cw[1]], axis=1)
        wt = jnp.tile(jnp.swapaxes(cw, 1, 2), (1, 1, page // NSA_BLOCK))
        wt = jnp.broadcast_to(wt[:, None], (2, NSA_KV_HEADS, HEAD_DIM, page)).reshape(4 * HEAD_DIM, page)
        wts = [a_out[l].astype(BF16), nsa_out[l].astype(BF16), diff_out[l].astype(BF16), d_out[l].astype(BF16),
               w_out[l].astype(BF16)]
        conv_w = (conv_a_w[l], conv_a_b[l][None], conv_d_w[l], conv_d_b[l][None], d_norm_g[l][None], d_norm_b[l][None])
        lam_p, og = diff_lambda[l], diff_out_g[l][None]

        xp = ffn(xp, mod_p, 0, g[0:1], w1u, w1d, tm_p, tpg_p)
        zp = mix_in(xp, mod_p, g[1:2], wmix, tm_p, tpg_p)
        qn, qr, rows_p, win_p, dq, drows_p, ksd, vsd, dkb, dvb = qk_prep(
            zp, cos_p, sin_p, gains, gmat, tm_p, tpg_p, l, depth, drows_p)
        ua, ud, ca, cd = conv_branches(zp, zeros_a, zeros_d, *conv_w, nseq=bp, tm=tm_p, stride=1)
        cmp_p = nsa_compress(rows_p, w2)
        o_cmp, sel = nsa_cmp_select(qn, cmp_p, bp)
        o_slc = nsa_slc(qr, ksd, vsd, sel, emat_p, bp)
        o_win = nsa_win(qr, win_p, bp)
        uc = diff_attn(dq, dkb, dvb, lam_p, og, bp, lam_init)
        xp = merge_out(xp, mod_p, zp, ua, uc, ud, o_cmp, o_slc, o_win, gate_ex, *wts, tm=tm_p, tiles_per_group=tpg_p)
        xp = ffn(xp, mod_p, 6, g[2:3], w2u, w2d, tm_p, tpg_p)
        outs["nsa_p"].append(rows_p.reshape(bp, sp, 4, NSA_KV_HEADS, HEAD_DIM))
        wkeep = min(NSA_WINDOW, sp)
        outs["win_p"].append(win_p.reshape(bp, sp, 2, NSA_KV_HEADS, HEAD_DIM)[:, sp - wkeep:])
        outs["ca_p"].append(ca)
        outs["cd_p"].append(cd)

        xs = ffn(xs, mod_s, 0, g[0:1], w1u, w1d, tm_s, 1)
        zs = mix_in(xs, mod_s, g[1:2], wmix, tm_s, 1)
        qn, qr, rows_s, win_s, dq, drows_s = qk_prep(zs, cos_s, sin_s, gains, gmat, tm_s, 1, l, depth, drows_s)[:6]
        past_a = jnp.swapaxes(state_conv_a[l], 0, 1).reshape(1, (A_CONV - 1) * bs, A_WIDTH)
        past_d = jnp.swapaxes(state_conv_d[l], 0, 1).reshape(1, (D_CONV - 1) * bs, D_WIDTH)
        ua, ud, ca, cd = conv_branches(zs, past_a, past_d, *conv_w, nseq=1, tm=tm_s, stride=bs)
        rows_b = _to_seq_major(rows_s, bs, ns)
        new_t = _tokens_last(rows_s, bs, ns, (4, NSA_KV_HEADS, HEAD_DIM))
        win_new_t = _tokens_last(win_s, bs, ns, (2, NSA_KV_HEADS, HEAD_DIM))
        o_cmp, o_slc, o_win, win_out_t = nsa_sample(
            l, page_table, cache_nsa_t, new_t, _to_seq_major(qn, bs, ns, QPAD), _to_seq_major(qr, bs, ns, QPAD),
            win_state_t, win_new_t, wt, selm_s, ns, win_out_t)
        drows_b = _to_seq_major(drows_s[l].reshape(ns * bs, 2 * DIFF_HEADS * LANES), bs, ns, QPAD)
        uc = diff_sample(l, page_table, cache_diff2, drows_b, _to_seq_major(dq, bs, ns, QPAD), lam_p, og, ns, lam_init)
        xs = merge_out(xs, mod_s, zs, ua, _to_step_major(uc, ns).astype(BF16), ud, _to_step_major(o_cmp, ns),
                       _to_step_major(o_slc, ns), _to_step_major(o_win, ns), gate_ex, *wts, tm=tm_s, tiles_per_group=1)
        xs = ffn(xs, mod_s, 6, g[2:3], w2u, w2d, tm_s, 1)
        outs["nsa_s"].append(rows_b.reshape(bs, ns, 4, NSA_KV_HEADS, HEAD_DIM))
        outs["ca_s"].append(jnp.swapaxes(ca.reshape(A_CONV - 1, bs, A_WIDTH), 0, 1))
        outs["cd_s"].append(jnp.swapaxes(cd.reshape(D_CONV - 1, bs, D_WIDTH), 0, 1))

    y_p = xp.reshape(bp, sp, d)
    y_s = jnp.swapaxes(xs.reshape(ns, bs, d), 0, 1)
    st = {k: jnp.stack(v, axis=0) for k, v in outs.items()}
    diff_p = drows_p.reshape(depth, bp, sp, 2, DIFF_HEADS, 2 * HEAD_DIM)
    diff_s = jnp.swapaxes(drows_s.reshape(depth, ns, bs, 2, DIFF_HEADS, 2 * HEAD_DIM), 1, 2)
    win_s = jnp.moveaxis(win_out_t, -1, 2)
    return (y_p, y_s, st["nsa_p"], st["nsa_s"], diff_p, diff_s, st["win_p"], win_s,
            st["ca_p"], st["ca_s"], st["cd_p"], st["cd_s"])
```

```python
import functools
import math

import jax
import jax.numpy as jnp
from jax import lax
from jax.experimental import pallas as pl
from jax.experimental.pallas import tpu as pltpu

F32 = jnp.float32
BF16 = jnp.bfloat16

HEAD_DIM = 64
ROPE_THETA = 10000.0
NORM_EPS = 1e-6
A_WIDTH = 512
A_CONV = 3
NSA_HEADS = 8
NSA_KV_HEADS = 2
NSA_GROUP = NSA_HEADS // NSA_KV_HEADS
NSA_BLOCK = 64
NSA_TOPK = 16
NSA_WINDOW = 512
DIFF_HEADS = 4
D_WIDTH = 512
D_CONV = 31
N_BRANCH = 4
N_ADA = 9
LANES = 128
NEG = -1e30
ATT_SCALE = HEAD_DIM ** -0.5
Q_SCALE = ATT_SCALE * math.log2(math.e)
VMEM_LIMIT = 56 * 1024 * 1024

C_KV = 0
C_MERGE = 1024
C_GLU = 5120
C_AB = 6144
C_AC = 6656
C_AX = 7168
C_NQ = 7680
C_DQ = 8192
C_DK = 8704
C_DV = 9216
C_NG = 768
MIX_W = 9728
MIX_TN = 4864


def _cparams(sem):
    return pltpu.CompilerParams(dimension_semantics=sem, vmem_limit_bytes=VMEM_LIMIT)


def _sigmoid(x):
    return 1.0 / (1.0 + jnp.exp(-x))


def _silu(x):
    return x * _sigmoid(x)


def _rows_to(m, tm):
    r = m.shape[0]
    if r == 1 or r == tm:
        return m
    return jnp.concatenate([m] * (tm // r), axis=0)


def _modulate(x, g, shift, scale):
    ms = jnp.mean(x * x, axis=-1, keepdims=True)
    y = x * lax.rsqrt(ms + NORM_EPS) * g
    return y * (1.0 + scale) + shift


def _dot(a, b):
    return jnp.dot(a, b, preferred_element_type=F32)


def _dot_nt(a, b, precision=None):
    return lax.dot_general(a, b, (((1,), (1,)), ((), ())), preferred_element_type=F32, precision=precision)


def _lane_lo(shape):
    return (lax.broadcasted_iota(jnp.int32, shape, len(shape) - 1) % LANES) < HEAD_DIM


def _dup_halves(x, lo):
    xr = pltpu.roll(x, HEAD_DIM, 1)
    return jnp.where(lo, x, xr), jnp.where(lo, xr, x)


def _ada_kernel(c_ref, w_ref, b_ref, o_ref):
    s = _silu(c_ref[...])
    o_ref[...] = _dot(s.astype(BF16), w_ref[...].astype(BF16)) + b_ref[...]


def ada_proj(c_all, ada_w, ada_b):
    r, d = c_all.shape
    nl = ada_w.shape[0]
    return pl.pallas_call(
        _ada_kernel,
        grid=(nl, N_ADA),
        in_specs=[
            pl.BlockSpec((r, d), lambda l, k: (0, 0)),
            pl.BlockSpec((None, d, d), lambda l, k: (l, 0, k)),
            pl.BlockSpec((None, None, 1, d), lambda l, k: (l, k, 0, 0)),
        ],
        out_specs=pl.BlockSpec((None, None, r, d), lambda l, k: (l, k, 0, 0)),
        out_shape=jax.ShapeDtypeStruct((nl, N_ADA, r, d), F32),
        compiler_params=_cparams(("arbitrary", "arbitrary")),
        name="ada_proj",
    )(c_all, ada_w, ada_b.reshape(nl, N_ADA, 1, d))


def _mod_spec(mod, k, tiles_per_group):
    _, _, r, d = mod.shape
    return pl.BlockSpec((None, None, r, d), lambda i: (k, i // tiles_per_group, 0, 0))


def _ffn_kernel(x_ref, sh_ref, sc_ref, gt_ref, g_ref, wa_ref, wv_ref, wd_ref, o_ref):
    tm = x_ref.shape[0]
    x = x_ref[...]
    h = _modulate(x, g_ref[...], _rows_to(sh_ref[...], tm), _rows_to(sc_ref[...], tm)).astype(BF16)
    a = _dot(h, wa_ref[...])
    v = _dot(h, wv_ref[...])
    y = _dot((_silu(a) * v).astype(BF16), wd_ref[...])
    o_ref[...] = x + 0.5 * _rows_to(gt_ref[...], tm) * y


def ffn(x, mod, k0, g, w_up, w_down, tm, tiles_per_group):
    t, d = x.shape
    f = w_down.shape[0]
    return pl.pallas_call(
        _ffn_kernel,
        grid=(t // tm,),
        in_specs=[
            pl.BlockSpec((tm, d), lambda i: (i, 0)),
            _mod_spec(mod, k0, tiles_per_group),
            _mod_spec(mod, k0 + 1, tiles_per_group),
            _mod_spec(mod, k0 + 2, tiles_per_group),
            pl.BlockSpec((1, d), lambda i: (0, 0)),
            pl.BlockSpec((d, f), lambda i: (0, 0)),
            pl.BlockSpec((d, f), lambda i: (0, 1)),
            pl.BlockSpec((f, d), lambda i: (0, 0)),
        ],
        out_specs=pl.BlockSpec((tm, d), lambda i: (i, 0)),
        out_shape=jax.ShapeDtypeStruct((t, d), F32),
        compiler_params=_cparams(("arbitrary",)),
        name="ffn",
    )(x, mod, mod, mod, g, w_up, w_up, w_down)


def _mix_kernel(x_ref, sh_ref, sc_ref, g_ref, w_ref, o_ref):
    tm = x_ref.shape[0]
    h = _modulate(x_ref[...], g_ref[...], _rows_to(sh_ref[...], tm), _rows_to(sc_ref[...], tm))
    o_ref[...] = _dot(h.astype(BF16), w_ref[...])


def mix_in(x, mod, g, w_mix, tm, tiles_per_group, tn=MIX_TN):
    t, d = x.shape
    nw = w_mix.shape[1]
    _, _, r, _ = mod.shape

    def mod_spec(k):
        return pl.BlockSpec((None, None, r, d), lambda j, i: (k, i // tiles_per_group, 0, 0))

    return pl.pallas_call(
        _mix_kernel,
        grid=(nw // tn, t // tm),
        in_specs=[
            pl.BlockSpec((tm, d), lambda j, i: (i, 0)),
            mod_spec(3),
            mod_spec(4),
            pl.BlockSpec((1, d), lambda j, i: (0, 0)),
            pl.BlockSpec((d, tn), lambda j, i: (0, j)),
        ],
        out_specs=pl.BlockSpec((tm, tn), lambda j, i: (i, j)),
        out_shape=jax.ShapeDtypeStruct((t, nw), F32),
        compiler_params=_cparams(("arbitrary", "arbitrary")),
        name="mix_in",
    )(x, mod, mod, g, w_mix)


def _group_norm(x, gain, gmat):
    x2 = x * x
    hi = x2.astype(BF16)
    lo = (x2 - hi.astype(F32)).astype(BF16)
    ss = _dot(hi, gmat) + _dot(lo, gmat)
    return x * lax.rsqrt(ss * (1.0 / HEAD_DIM) + NORM_EPS) * gain


def _rope(x, cos, sin_signed, lo):
    sw = jnp.where(lo, pltpu.roll(x, LANES - HEAD_DIM // 2, 1), pltpu.roll(x, HEAD_DIM // 2, 1))
    return x * cos + sw * sin_signed


def _prep_kernel(kv_ref, nq_ref, dq_ref, dk_ref, dv_ref, cos_ref, sin_ref, gain_ref, gmat_ref,
                 qn_ref, qr_ref, rows_ref, win_ref, dqo_ref, drows_ref, ksd_ref, vsd_ref, dkb_ref, dvb_ref):
    cos = cos_ref[...]
    sin = sin_ref[...]
    gmat = gmat_ref[...]
    lo32 = (lax.broadcasted_iota(jnp.int32, cos.shape, 1) % HEAD_DIM) < HEAD_DIM // 2
    lo64 = _lane_lo(cos.shape)

    def gain(r):
        return gain_ref[r:r + 1, :]

    def piece(ref, p):
        return ref[:, p * LANES:(p + 1) * LANES]

    for p in range(4):
        qn = _group_norm(piece(nq_ref, p), gain(0), gmat)
        qn_ref[:, p * LANES:(p + 1) * LANES] = qn
        qr_ref[:, p * LANES:(p + 1) * LANES] = (_rope(qn, cos, sin, lo32) * Q_SCALE).astype(BF16)
    rows_ref[:, 0:LANES] = _group_norm(piece(kv_ref, 0), gain(1), gmat)
    rows_ref[:, LANES:2 * LANES] = piece(kv_ref, 1)
    k_slc = _rope(_group_norm(piece(kv_ref, 2), gain(2), gmat), cos, sin, lo32)
    v_slc = piece(kv_ref, 3)
    rows_ref[:, 2 * LANES:3 * LANES] = k_slc
    rows_ref[:, 3 * LANES:4 * LANES] = v_slc
    for hk, (kd, vd) in enumerate(zip(_dup_halves(k_slc, lo64), _dup_halves(v_slc, lo64))):
        ksd_ref[:, hk * LANES:(hk + 1) * LANES] = kd.astype(BF16)
        vsd_ref[:, hk * LANES:(hk + 1) * LANES] = vd.astype(BF16)
    win_ref[:, 0:LANES] = _rope(_group_norm(piece(kv_ref, 4), gain(3), gmat), cos, sin, lo32)
    win_ref[:, LANES:2 * LANES] = piece(kv_ref, 5)
    for p in range(4):
        dq = _rope(_group_norm(piece(dq_ref, p), gain(4), gmat), cos, sin, lo32)
        dqo_ref[:, p * LANES:(p + 1) * LANES] = (dq * Q_SCALE).astype(BF16)
        dk = _rope(_group_norm(piece(dk_ref, p), gain(5), gmat), cos, sin, lo32)
        drows_ref[:, p, :] = dk
        dkb_ref[:, p * LANES:(p + 1) * LANES] = dk.astype(BF16)
        drows_ref[:, DIFF_HEADS + p, :] = piece(dv_ref, p)
    dvb_ref[...] = dv_ref[...].astype(BF16)


def qk_prep(z, cos, sin, gains, gmat, tm, tiles_per_seq, layer, n_layers, drows_prev):
    t = z.shape[0]

    def zs(width, col):
        return pl.BlockSpec((tm, width), lambda i: (i, col // width))

    def os_(width):
        return pl.BlockSpec((tm, width), lambda i: (i, 0))

    tab = pl.BlockSpec((tm, LANES), lambda i: (i % tiles_per_seq, 0))
    outs = ((512, F32), (512, BF16), (512, F32), (256, F32), (512, BF16), None,
            (256, BF16), (256, BF16), (512, BF16), (512, BF16))
    drows_shape = jax.ShapeDtypeStruct((n_layers, t, 2 * DIFF_HEADS, LANES), F32)
    drows_spec = pl.BlockSpec((None, tm, 2 * DIFF_HEADS, LANES), lambda i: (layer, i, 0, 0))
    in_specs = [zs(768, C_KV), zs(512, C_NQ), zs(512, C_DQ), zs(512, C_DK), zs(512, C_DV), tab, tab,
                pl.BlockSpec((8, LANES), lambda i: (0, 0)), pl.BlockSpec((LANES, LANES), lambda i: (0, 0))]
    args = [z, z, z, z, z, cos, sin, gains, gmat]
    kern, aliases = _prep_kernel, {}
    if drows_prev is not None:
        in_specs.append(pl.BlockSpec(memory_space=pl.ANY))
        args.append(drows_prev)
        aliases = {len(args) - 1: 5}
        kern = _prep_kernel_aliased
    return pl.pallas_call(
        kern,
        grid=(t // tm,),
        in_specs=in_specs,
        out_specs=[drows_spec if o is None else os_(o[0]) for o in outs],
        out_shape=[drows_shape if o is None else jax.ShapeDtypeStruct((t, o[0]), o[1]) for o in outs],
        input_output_aliases=aliases,
        compiler_params=_cparams(("arbitrary",)),
        name="qk_prep",
    )(*args)


def _prep_kernel_aliased(*refs):
    _prep_kernel(*refs[:9], *refs[10:])


def _conv_kernel(ab_ref, ac_ref, ax_ref, glu_ref, pa_ref, pd_ref, wa_ref, ba_ref, wd_ref, bd_ref, lg_ref, lb_ref,
                 ua_ref, ud_ref, sa_ref, sd_ref, bufa, bufd, *, tm, ha, hd, stride, nt):
    i = pl.program_id(1)

    @pl.when(i == 0)
    def _():
        bufa[0:ha, :] = pa_ref[...]
        bufd[0:hd, :] = pd_ref[...]

    if nt > 1:
        @pl.when(i > 0)
        def _():
            bufa[0:ha, :] = bufa[tm:tm + ha, :]
            bufd[0:hd, :] = bufd[tm:tm + hd, :]

    bufa[ha:ha + tm, :] = ac_ref[...] * ax_ref[...]
    ya = ba_ref[...]
    for k in range(A_CONV):
        ya = ya + wa_ref[k:k + 1, :] * bufa[pl.ds(ha - (A_CONV - 1 - k) * stride, tm), :]
    ua_ref[...] = (ab_ref[...] * ya).astype(ua_ref.dtype)

    glu = glu_ref[...]
    bufd[hd:hd + tm, :] = glu[:, :D_WIDTH] * _sigmoid(glu[:, D_WIDTH:])
    yd = bd_ref[...]
    for k in range(D_CONV):
        yd = yd + wd_ref[k:k + 1, :] * bufd[pl.ds(hd - (D_CONV - 1 - k) * stride, tm), :]
    mu = jnp.mean(yd, axis=-1, keepdims=True)
    var = jnp.mean(jnp.square(yd - mu), axis=-1, keepdims=True)
    y = (yd - mu) * lax.rsqrt(var + NORM_EPS) * lg_ref[...] + lb_ref[...]
    ud_ref[...] = _silu(y).astype(ud_ref.dtype)

    @pl.when(i == nt - 1)
    def _():
        sa_ref[...] = bufa[ha + tm - (A_CONV - 1) * stride:ha + tm, :]
        sd_ref[...] = bufd[hd + tm - (D_CONV - 1) * stride:hd + tm, :]


def conv_branches(z, past_a, past_d, wa, ba, wd, bd, lg, lb, nseq, tm, stride):
    t = z.shape[0]
    nt = t // nseq // tm
    ha, hd = past_a.shape[1], past_d.shape[1]
    if nt > 1:
        assert tm >= ha and tm >= hd

    def zs(width, col):
        return pl.BlockSpec((tm, width), lambda s, i: (s * nt + i, col // width))

    def cs(shape):
        return pl.BlockSpec(shape, lambda s, i: (0, 0))

    sa_rows, sd_rows = (A_CONV - 1) * stride, (D_CONV - 1) * stride
    return pl.pallas_call(
        functools.partial(_conv_kernel, tm=tm, ha=ha, hd=hd, stride=stride, nt=nt),
        grid=(nseq, nt),
        in_specs=[zs(512, C_AB), zs(512, C_AC), zs(512, C_AX), zs(1024, C_GLU),
                  pl.BlockSpec((None, ha, A_WIDTH), lambda s, i: (s, 0, 0)),
                  pl.BlockSpec((None, hd, D_WIDTH), lambda s, i: (s, 0, 0)),
                  cs((A_CONV, A_WIDTH)), cs((1, A_WIDTH)), cs((D_CONV, D_WIDTH)), cs((1, D_WIDTH)),
                  cs((1, D_WIDTH)), cs((1, D_WIDTH))],
        out_specs=[pl.BlockSpec((tm, A_WIDTH), lambda s, i: (s * nt + i, 0)),
                   pl.BlockSpec((tm, D_WIDTH), lambda s, i: (s * nt + i, 0)),
                   pl.BlockSpec((None, sa_rows, A_WIDTH), lambda s, i: (s, 0, 0)),
                   pl.BlockSpec((None, sd_rows, D_WIDTH), lambda s, i: (s, 0, 0))],
        out_shape=[jax.ShapeDtypeStruct((t, A_WIDTH), BF16), jax.ShapeDtypeStruct((t, D_WIDTH), BF16),
                   jax.ShapeDtypeStruct((nseq, sa_rows, A_WIDTH), F32),
                   jax.ShapeDtypeStruct((nseq, sd_rows, D_WIDTH), F32)],
        scratch_shapes=[pltpu.VMEM((ha + tm, A_WIDTH), F32), pltpu.VMEM((hd + tm, D_WIDTH), F32)],
        compiler_params=_cparams(("arbitrary", "arbitrary")),
        name="conv_branches",
    )(z, z, z, z, past_a, past_d, wa, ba, wd, bd, lg, lb)


def _merge_kernel(x_ref, gt_ref, ua_ref, uc_ref, ud_ref, oc_ref, os_ref, ow_ref, ng_ref, ex_ref,
                  m0_ref, m1_ref, m2_ref, m3_ref, wa_ref, wb_ref, wc_ref, wd_ref, wo_ref, o_ref):
    tm = x_ref.shape[0]
    g = _sigmoid(ng_ref[...])
    g_hi = g.astype(BF16)
    g_lo = (g - g_hi.astype(F32)).astype(BF16)
    gx = _dot(g_hi, ex_ref[...]) + _dot(g_lo, ex_ref[...])
    ub = (gx[:, 0:512] * oc_ref[...] + gx[:, 512:1024] * os_ref[...] + gx[:, 1024:1536] * ow_ref[...]).astype(BF16)
    merged = _sigmoid(m0_ref[...]) * _dot(ua_ref[...], wa_ref[...])
    merged += _sigmoid(m1_ref[...]) * _dot(ub, wb_ref[...])
    merged += _sigmoid(m2_ref[...]) * _dot(uc_ref[...], wc_ref[...])
    merged += _sigmoid(m3_ref[...]) * _dot(ud_ref[...], wd_ref[...])
    y = _dot(merged.astype(BF16), wo_ref[...])
    o_ref[...] = x_ref[...] + _rows_to(gt_ref[...], tm) * y


def merge_out(x, mod, z, ua, uc, ud, o_cmp, o_slc, o_win, gate_ex, w_a, w_b, w_c, w_d, w_o, tm, tiles_per_group):
    t, d = x.shape

    def ts(width):
        return pl.BlockSpec((tm, width), lambda i: (i, 0))

    def zs(width, col):
        return pl.BlockSpec((tm, width), lambda i: (i, col // width))

    def ws(shape):
        return pl.BlockSpec(shape, lambda i: (0, 0))

    return pl.pallas_call(
        _merge_kernel,
        grid=(t // tm,),
        in_specs=[ts(d), _mod_spec(mod, 5, tiles_per_group), ts(512), ts(512), ts(512), ts(512), ts(512), ts(512),
                  zs(LANES, C_NG), ws(gate_ex.shape),
                  zs(1024, C_MERGE), zs(1024, C_MERGE + 1024), zs(1024, C_MERGE + 2048), zs(1024, C_MERGE + 3072),
                  ws((512, d)), ws((512, d)), ws((512, d)), ws((512, d)), ws((d, d))],
        out_specs=ts(d),
        out_shape=jax.ShapeDtypeStruct((t, d), F32),
        compiler_params=_cparams(("arbitrary",)),
        name="merge_out",
    )(x, mod, ua, uc, ud, o_cmp, o_slc, o_win, z, gate_ex, z, z, z, z, w_a, w_b, w_c, w_d, w_o)


def _compress_kernel(r_ref, w_ref, o_ref):
    x = r_ref[...]
    nb = x.shape[0] // NSA_BLOCK
    o_ref[...] = jnp.sum(x.reshape(nb, NSA_BLOCK, x.shape[1]) * w_ref[...][None], axis=1)


def nsa_compress(rows, w2, tm=512):
    t = rows.shape[0]
    return pl.pallas_call(
        _compress_kernel,
        grid=(t // tm,),
        in_specs=[pl.BlockSpec((tm, 256), lambda i: (i, 0)), pl.BlockSpec((NSA_BLOCK, 256), lambda i: (0, 0))],
        out_specs=pl.BlockSpec((tm // NSA_BLOCK, 256), lambda i: (i, 0)),
        out_shape=jax.ShapeDtypeStruct((t // NSA_BLOCK, 256), F32),
        compiler_params=_cparams(("arbitrary",)),
        name="nsa_compress",
    )(rows, w2)


def _masked_softmax(s, mask):
    s = jnp.where(mask, s, NEG)
    p = jnp.exp(s - jnp.max(s, axis=-1, keepdims=True))
    p = jnp.where(mask, p, 0.0)
    return p / jnp.maximum(jnp.sum(p, axis=-1, keepdims=True), 1e-30)


def _top_blocks(imp, n_sel):
    imp = imp.T
    idx = lax.broadcasted_iota(jnp.int32, imp.shape, 0)
    big = imp.shape[0]
    sel = jnp.zeros(imp.shape, F32)
    for _ in range(n_sel):
        m = jnp.max(imp, axis=0, keepdims=True)
        first = jnp.min(jnp.where(imp == m, idx, big), axis=0, keepdims=True)
        hit = idx == first
        sel = jnp.where(hit, 1.0, sel)
        imp = jnp.where(hit, -3.0, imp)
    return sel.T


def _top_blocks_by_rank(imp, n_sel):
    nb = imp.shape[1]
    idx = lax.broadcasted_iota(jnp.int32, imp.shape, 1)
    rank = jnp.zeros(imp.shape, F32)
    for d in range(1, nb):
        other = pltpu.roll(imp, d, 1)
        beats = (other > imp) | ((other == imp) & (idx >= d))
        rank = rank + jnp.where(beats, 1.0, 0.0)
    return jnp.where(rank < n_sel, 1.0, 0.0)


def _cmp_select(q, kdup, vdup, qpos, nb, n_sel):
    tq = q.shape[0]
    qlo = _lane_lo((tq, LANES))
    blk = lax.broadcasted_iota(jnp.int32, (1, nb), 1)
    complete = (blk + 1) * NSA_BLOCK - 1 <= qpos
    cur = qpos // NSA_BLOCK
    outs, sels = [], []
    for hk in range(NSA_KV_HEADS):
        imp = jnp.zeros((tq, nb), F32)
        vd = vdup[hk].astype(BF16)
        for a in (2 * hk, 2 * hk + 1):
            qp = q[:, a * LANES:(a + 1) * LANES]
            pair = []
            for par in range(2):
                qm = jnp.where(qlo if par == 0 else ~qlo, qp, 0.0)
                s = _dot_nt(qm, kdup[hk], precision=lax.Precision.HIGHEST)
                pc = _masked_softmax(s * ATT_SCALE, complete)
                imp = imp + pc
                pair.append(_dot(pc.astype(BF16), vd))
            outs.append(jnp.where(qlo, pair[0], pair[1]))
        forced = (blk == cur) | (blk == 0)
        imp = jnp.where(forced, NSA_GROUP + 1.0, imp)
        imp = jnp.where(blk <= cur, imp, -1.0)
        sels.append(_top_blocks(imp, n_sel))
    return jnp.concatenate(outs, axis=1), sels


def _cmpsel_kernel(q_ref, c_ref, o_ref, sel_ref, *, tq, n_sel):
    qpos = pl.program_id(1) * tq + lax.broadcasted_iota(jnp.int32, (tq, 1), 0)
    nb = c_ref.shape[0]
    lo = _lane_lo((nb, LANES))
    kdup = _dup_halves(c_ref[:, 0:LANES], lo)
    vdup = _dup_halves(c_ref[:, LANES:2 * LANES], lo)
    o, sels = _cmp_select(q_ref[...], kdup, vdup, qpos, nb, n_sel)
    o_ref[...] = o
    sel_ref[...] = jnp.concatenate(sels, axis=1).astype(sel_ref.dtype)


def nsa_cmp_select(q_n, cmp, nseq, tq=512):
    t = q_n.shape[0]
    nq = t // nseq // tq
    nb = cmp.shape[0] // nseq
    return pl.pallas_call(
        functools.partial(_cmpsel_kernel, tq=tq, n_sel=min(NSA_TOPK, nb)),
        grid=(nseq, nq),
        in_specs=[pl.BlockSpec((tq, 512), lambda b, i: (b * nq + i, 0)),
                  pl.BlockSpec((nb, 256), lambda b, i: (b, 0))],
        out_specs=[pl.BlockSpec((tq, 512), lambda b, i: (b * nq + i, 0)),
                   pl.BlockSpec((tq, 2 * nb), lambda b, i: (b * nq + i, 0))],
        out_shape=[jax.ShapeDtypeStruct((t, 512), F32), jax.ShapeDtypeStruct((t, 2 * nb), BF16)],
        compiler_params=_cparams(("arbitrary", "arbitrary")),
        name="nsa_cmp_select",
    )(q_n, cmp)


def _online_step(s, m, acc, v_ext, exp_dtype=F32):
    tk = s.shape[1]
    m_new = jnp.maximum(m, jnp.max(s, axis=-1, keepdims=True))
    x = s - jnp.concatenate([m_new] * (tk // LANES), axis=1)
    p = jnp.exp2(x.astype(exp_dtype)).astype(BF16)
    alpha = jnp.exp2(m - m_new)
    if acc.shape[1] != LANES:
        alpha = jnp.concatenate([alpha] * (acc.shape[1] // LANES), axis=1)
    return m_new, alpha * acc + _dot(p, v_ext)


GROWTH_LIMIT = 64.0


def _single_pass_step(s, ref, acc, growth, v_ext):
    tk = s.shape[1]
    p = jnp.exp2(s - jnp.concatenate([ref] * (tk // LANES), axis=1)).astype(BF16)
    row_max = jnp.max(s, axis=-1, keepdims=True)
    growth = jnp.maximum(growth, row_max - ref)
    ref_new = jnp.maximum(ref, row_max)
    alpha = jnp.exp2(ref - ref_new)
    if acc.shape[1] != LANES:
        alpha = jnp.concatenate([alpha] * (acc.shape[1] // LANES), axis=1)
    return ref_new, (acc + _dot(p, v_ext)) * alpha, growth


def _lane_bcast0(x):
    return jnp.broadcast_to(x[:, 0:1], x.shape)


def _slc_kernel(q_ref, k_ref, v_ref, sel_ref, e_ref, o_ref, *, tq, tk, nb):
    qi = pl.program_id(1)
    n_full = (qi * tq) // tk
    rows = NSA_GROUP * tq
    qlo = _lane_lo((tq, LANES))
    klo = _lane_lo((tk, LANES))
    zero = jnp.zeros((tq, LANES), BF16)
    qpos = qi * tq + lax.broadcasted_iota(jnp.int32, (tq, 1), 0)
    col = lax.broadcasted_iota(jnp.int32, (1, tk), 1)
    for hk in range(NSA_KV_HEADS):
        parts = []
        for c in range(NSA_GROUP):
            h = hk * NSA_GROUP + c
            qp = q_ref[:, (h // 2) * LANES:(h // 2 + 1) * LANES]
            parts.append(jnp.where(qlo, qp, zero) if h % 2 == 0 else jnp.where(qlo, zero, qp))
        qst = jnp.concatenate(parts, axis=0)
        sel = sel_ref[:, hk * nb:(hk + 1) * nb]

        def tile(j, diag, qst=qst, sel=sel, hk=hk):
            off = pl.multiple_of(j * tk, tk)
            k = k_ref[pl.ds(off, tk), hk * LANES:(hk + 1) * LANES]
            v = v_ref[pl.ds(off, tk), hk * LANES:(hk + 1) * LANES]
            bias = (_dot(sel, e_ref[j]) - 1.0) * (-NEG)
            if diag:
                bias = jnp.where(j * tk + col <= qpos, bias, NEG)
            s = _dot_nt(qst, k) + jnp.concatenate([bias] * NSA_GROUP, axis=0)
            return s, jnp.where(klo, v, jnp.ones_like(v))

        def finish(acc, hk=hk):
            o = acc / jnp.maximum(pltpu.roll(acc, HEAD_DIM, 1), 1e-30)
            for pair in range(NSA_GROUP // 2):
                a = hk * (NSA_GROUP // 2) + pair
                o_even = o[(2 * pair) * tq:(2 * pair + 1) * tq]
                o_odd = o[(2 * pair + 1) * tq:(2 * pair + 2) * tq]
                o_ref[:, a * LANES:(a + 1) * LANES] = jnp.where(qlo, o_even, pltpu.roll(o_odd, HEAD_DIM, 1))

        def body(j, carry, tile=tile):
            s, vx = tile(j, False)
            return _online_step(s, *carry, vx, exp_dtype=BF16)

        init = (jnp.full((rows, LANES), NEG, F32), jnp.zeros((rows, LANES), F32))
        m, acc = lax.fori_loop(0, n_full, body, init)
        s, vx = tile(n_full, True)
        finish(_online_step(s, m, acc, vx, exp_dtype=BF16)[1])


def nsa_slc(q_rot, ksd, vsd, sel, emat, nseq, tq=256):
    t = q_rot.shape[0]
    s = t // nseq
    nq = s // tq
    nk, nb, tk = emat.shape
    return pl.pallas_call(
        functools.partial(_slc_kernel, tq=tq, tk=tk, nb=nb),
        grid=(nseq, nq),
        in_specs=[pl.BlockSpec((tq, 512), lambda b, i: (b * nq + i, 0)),
                  pl.BlockSpec((s, 256), lambda b, i: (b, 0)),
                  pl.BlockSpec((s, 256), lambda b, i: (b, 0)),
                  pl.BlockSpec((tq, 2 * nb), lambda b, i: (b * nq + i, 0)),
                  pl.BlockSpec((nk, nb, tk), lambda b, i: (0, 0, 0))],
        out_specs=pl.BlockSpec((tq, 512), lambda b, i: (b * nq + i, 0)),
        out_shape=jax.ShapeDtypeStruct((t, 512), F32),
        compiler_params=_cparams(("arbitrary", "arbitrary")),
        name="nsa_slc",
    )(q_rot, ksd, vsd, sel, emat)


def _win_kernel(q_ref, wp_ref, wc_ref, o_ref, *, tq):
    i = pl.program_id(1)
    lo = _lane_lo((tq, LANES))
    qpos = i * tq + lax.broadcasted_iota(jnp.int32, (tq, 1), 0)
    chunks = []
    for off, ref in ((-tq, wp_ref), (0, wc_ref)):
        kpos = i * tq + off + lax.broadcasted_iota(jnp.int32, (1, tq), 1)
        bias = jnp.where((kpos >= 0) & (kpos <= qpos) & (qpos - kpos <= NSA_WINDOW), 0.0, NEG)
        kd = [k.astype(BF16) for k in _dup_halves(ref[:, 0:LANES], lo)]
        vx = [jnp.where(lo, v, 1.0).astype(BF16) for v in _dup_halves(ref[:, LANES:2 * LANES], lo)]
        chunks.append((bias, kd, vx))
    zero = jnp.zeros((tq, LANES), BF16)
    for a in range(NSA_HEADS // 2):
        pair = []
        for par in range(2):
            hk = (2 * a + par) // NSA_GROUP
            qp = q_ref[:, a * LANES:(a + 1) * LANES]
            qm = jnp.where(lo, qp, zero) if par == 0 else jnp.where(lo, zero, qp)
            ss = [_dot_nt(qm, kd[hk]) + bias for bias, kd, _ in chunks]
            m = jnp.maximum(jnp.max(ss[0], axis=-1, keepdims=True), jnp.max(ss[1], axis=-1, keepdims=True))
            acc = _dot(jnp.exp2(ss[0] - m).astype(BF16), chunks[0][2][hk])
            acc += _dot(jnp.exp2(ss[1] - m).astype(BF16), chunks[1][2][hk])
            pair.append(acc / jnp.maximum(pltpu.roll(acc, HEAD_DIM, 1), 1e-30))
        o_ref[:, a * LANES:(a + 1) * LANES] = jnp.where(lo, pair[0], pltpu.roll(pair[1], HEAD_DIM, 1))


def nsa_win(q_rot, win_rows, nseq, tq=NSA_WINDOW):
    t = q_rot.shape[0]
    nq = t // nseq // tq
    return pl.pallas_call(
        functools.partial(_win_kernel, tq=tq),
        grid=(nseq, nq),
        in_specs=[pl.BlockSpec((tq, 512), lambda b, i: (b * nq + i, 0)),
                  pl.BlockSpec((tq, 256), lambda b, i: (b * nq + jnp.maximum(i - 1, 0), 0)),
                  pl.BlockSpec((tq, 256), lambda b, i: (b * nq + i, 0))],
        out_specs=pl.BlockSpec((tq, 512), lambda b, i: (b * nq + i, 0)),
        out_shape=jax.ShapeDtypeStruct((t, 512), F32),
        compiler_params=_cparams(("arbitrary", "arbitrary")),
        name="nsa_win",
    )(q_rot, win_rows, win_rows)


def _diff_lambda(lam_ref, lam_init):
    lp = lam_ref[...]
    a = jnp.sum(lp[0:1] * lp[1:2], axis=-1, keepdims=True)
    b = jnp.sum(lp[2:3] * lp[3:4], axis=-1, keepdims=True)
    return jnp.exp(a) - jnp.exp(b) + lam_init


def _diff_finish(o1, o2, lam, g, lam_init):
    o = o1 - lam * o2
    ms = jnp.mean(o * o, axis=-1, keepdims=True)
    return o * lax.rsqrt(ms + NORM_EPS) * g * (1.0 - lam_init)


def _diff_kernel(q_ref, k_ref, v_ref, lam_ref, g_ref, o_ref, *, tq, tk, lam_init):
    qi = pl.program_id(2)
    n_full = (qi * tq) // tk
    lo = _lane_lo((tq, LANES))
    q = q_ref[...]
    zero = jnp.zeros_like(q)
    qst = jnp.concatenate([jnp.where(lo, q, zero), jnp.where(lo, zero, q)], axis=0)
    qpos = qi * tq + lax.broadcasted_iota(jnp.int32, (tq, 1), 0)
    col = lax.broadcasted_iota(jnp.int32, (1, tk), 1)
    bias = jnp.where(n_full * tk + col <= qpos, 0.0, NEG)
    bias = jnp.concatenate([bias, bias], axis=0)
    ones = jnp.ones((tk, LANES), BF16)

    def tile(j):
        off = pl.multiple_of(j * tk, tk)
        return _dot_nt(qst, k_ref[pl.ds(off, tk), :]), jnp.concatenate([v_ref[pl.ds(off, tk), :], ones], axis=1)

    def walk(step, init):
        def full(j, carry):
            s, vx = tile(j)
            return step(s, carry, vx)

        carry = lax.fori_loop(0, n_full, full, init)
        s, vx = tile(n_full)
        return step(s + bias, carry, vx)

    def finish(acc):
        o = acc[:, 0:LANES] / jnp.maximum(acc[:, LANES:2 * LANES], 1e-30)
        lam = _diff_lambda(lam_ref, lam_init)
        o_ref[...] = _diff_finish(o[0:tq], o[tq:2 * tq], lam, g_ref[...], lam_init).astype(o_ref.dtype)

    ref0 = _lane_bcast0(_dot_nt(qst, k_ref[0:LANES, :]))
    init = (ref0, jnp.zeros((2 * tq, 2 * LANES), F32), jnp.zeros((2 * tq, LANES), F32))
    _, acc, growth = walk(lambda s, carry, vx: _single_pass_step(s, *carry, vx), init)
    fast_ok = jnp.max(growth) <= GROWTH_LIMIT

    @pl.when(fast_ok)
    def _():
        finish(acc)

    @pl.when(jnp.logical_not(fast_ok))
    def _():
        init = (jnp.full((2 * tq, LANES), NEG, F32), jnp.zeros((2 * tq, 2 * LANES), F32))
        finish(walk(lambda s, carry, vx: _online_step(s, *carry, vx), init)[1])


def diff_attn(dq, dk, dv, lam_p, out_g, nseq, lam_init, tq=256, tk=1024):
    t = dq.shape[0]
    s = t // nseq
    nq = s // tq
    return pl.pallas_call(
        functools.partial(_diff_kernel, tq=tq, tk=tk, lam_init=lam_init),
        grid=(nseq, DIFF_HEADS, nq),
        in_specs=[pl.BlockSpec((tq, LANES), lambda b, h, i: (b * nq + i, h)),
                  pl.BlockSpec((s, LANES), lambda b, h, i: (b, h)),
                  pl.BlockSpec((s, LANES), lambda b, h, i: (b, h)),
                  pl.BlockSpec((4, HEAD_DIM), lambda b, h, i: (0, 0)),
                  pl.BlockSpec((1, LANES), lambda b, h, i: (0, 0))],
        out_specs=pl.BlockSpec((tq, LANES), lambda b, h, i: (b * nq + i, h)),
        out_shape=jax.ShapeDtypeStruct((t, 512), BF16),
        compiler_params=_cparams(("arbitrary", "arbitrary", "arbitrary")),
        name="diff_attn",
    )(dq, dk, dv, lam_p, out_g)


QPAD = 8
SEQS_PER_STEP = 2


def _pad_keys(x):
    return jnp.concatenate([x, jnp.zeros((LANES - x.shape[0], x.shape[1]), x.dtype)], axis=0)


def _nsa_sample_kernel(pt_ref, *refs, n_pages, past_len, n_new, n_sel, n_seq):
    pages = refs[:n_seq * n_pages]
    new_ref, qn_ref, qr_ref, wst_ref, wnew_ref, wt_ref, selm_ref = refs[n_seq * n_pages:n_seq * n_pages + 7]
    oc_ref, os_ref, ow_ref, wout_ref = refs[-4:]
    for s in range(n_seq):
        chunks = tuple(pages[s * n_pages:(s + 1) * n_pages]) + (new_ref.at[s],)
        _nsa_sample_one(chunks, qn_ref.at[s], qr_ref.at[s], wst_ref.at[s], wnew_ref.at[s], wt_ref, selm_ref,
                        oc_ref.at[s], os_ref.at[s], ow_ref.at[s], wout_ref.at[s],
                        n_pages=n_pages, past_len=past_len, n_new=n_new, n_sel=n_sel)


def _nsa_sample_one(chunks, qn_ref, qr_ref, wst_ref, wnew_ref, wt_ref, selm_ref, oc_ref, os_ref, ow_ref, wout_ref,
                    *, n_pages, past_len, n_new, n_sel):
    n_real = (past_len + n_new + NSA_BLOCK - 1) // NSA_BLOCK
    nbp = selm_ref.shape[1]
    page = chunks[0].shape[3]

    def twice(x):
        return jnp.concatenate([x, x], axis=0)

    wt = wt_ref[...]
    y = jnp.concatenate([ch[0:2].reshape(4 * HEAD_DIM, page) * wt for ch in chunks], axis=1)
    hi = y.astype(BF16)
    selm = selm_ref[...]
    cmp_t = _dot(hi, selm)
    yk = y[0:2 * HEAD_DIM]
    cmp_k = cmp_t[0:2 * HEAD_DIM] + _dot((yk - hi[0:2 * HEAD_DIM].astype(F32)).astype(BF16), selm)

    tok = lax.broadcasted_iota(jnp.int32, (QPAD, 1), 0)
    qpos = past_len + tok
    rows = NSA_GROUP * QPAD
    qpos_st = jnp.concatenate([qpos] * NSA_GROUP, axis=0)
    qlo = _lane_lo((QPAD, LANES))

    def stack_group(q_ref, hk, zero):
        parts = []
        for c in range(NSA_GROUP):
            h = hk * NSA_GROUP + c
            qp = q_ref[:, (h // 2) * LANES:(h // 2 + 1) * LANES]
            parts.append(jnp.where(qlo, qp, zero) if h % 2 == 0 else jnp.where(qlo, zero, qp))
        return jnp.concatenate(parts, axis=0)

    def emit(o_ref, per_hk):
        for a in range(NSA_HEADS // 2):
            hk, c = (2 * a) // NSA_GROUP, (2 * a) % NSA_GROUP
            o = per_hk[hk]
            o_ref[:, a * LANES:(a + 1) * LANES] = jnp.where(qlo, o[c * QPAD:(c + 1) * QPAD], o[(c + 1) * QPAD:(c + 2) * QPAD])

    blk = lax.broadcasted_iota(jnp.int32, (1, nbp), 1)
    complete = ((blk + 1) * NSA_BLOCK - 1 <= qpos_st) & (blk < n_real)
    cur = qpos // NSA_BLOCK
    per_hk, imps = [], []
    for hk in range(NSA_KV_HEADS):
        kd = twice(cmp_k[hk * HEAD_DIM:(hk + 1) * HEAD_DIM])
        vd = twice(cmp_t[(2 + hk) * HEAD_DIM:(3 + hk) * HEAD_DIM]).astype(BF16)
        s = jnp.dot(stack_group(qn_ref, hk, jnp.zeros((QPAD, LANES), F32)), kd,
                    precision=lax.Precision.HIGHEST, preferred_element_type=F32)
        pc = _masked_softmax(s * ATT_SCALE, complete)
        per_hk.append(_dot_nt(pc.astype(BF16), vd))
        imp = functools.reduce(lambda a, b: a + b, [pc[c * QPAD:(c + 1) * QPAD] for c in range(NSA_GROUP)])
        imp = jnp.where((blk == cur) | (blk == 0), NSA_GROUP + 1.0, imp)
        imp = jnp.where(blk <= cur, imp, -1.0)
        imps.append(jnp.where(blk < n_real, imp, -2.0))
    emit(oc_ref, per_hk)
    sel_all = _top_blocks_by_rank(jnp.concatenate(imps, axis=0), n_sel)
    sels = [sel_all[hk * QPAD:(hk + 1) * QPAD] for hk in range(NSA_KV_HEADS)]

    qst = [stack_group(qr_ref, hk, jnp.zeros((QPAD, LANES), BF16)) for hk in range(NSA_KV_HEADS)]
    key_new = lax.broadcasted_iota(jnp.int32, (1, LANES), 1)
    new_valid = (key_new < n_new) & (past_len + key_new <= qpos_st)

    def attend(kvs, biases, hk):
        k_all = jnp.concatenate([twice(k).astype(BF16) for k, _ in kvs], axis=1)
        v_all = jnp.concatenate([twice(v).astype(BF16) for _, v in kvs], axis=1)
        s = _dot(qst[hk], k_all) + jnp.concatenate(biases, axis=1)
        p = jnp.exp2(s - jnp.max(s, axis=-1, keepdims=True))
        p = p / jnp.maximum(jnp.sum(p, axis=-1, keepdims=True), 1e-30)
        return _dot_nt(p.astype(BF16), v_all)

    per_hk = []
    for hk in range(NSA_KV_HEADS):
        selx = _dot_nt(sels[hk].astype(BF16), selm)
        selx = jnp.concatenate([selx] * NSA_GROUP, axis=0)
        kvs, biases = [], []
        for j, ch in enumerate(chunks):
            kvs.append((ch[2, hk], ch[3, hk]))
            bias = (selx[:, j * page:(j + 1) * page] - 1.0) * (-NEG)
            biases.append(jnp.where(new_valid, bias, NEG) if j == n_pages else bias)
        per_hk.append(attend(kvs, biases, hk))
    emit(os_ref, per_hk)

    wbuf = wst_ref.shape[3]
    kpos = past_len - wbuf + lax.broadcasted_iota(jnp.int32, (1, wbuf), 1)
    bias_st = jnp.where((kpos >= 0) & (kpos <= qpos_st) & (qpos_st - kpos <= NSA_WINDOW), 0.0, NEG)
    bias_new = jnp.where(new_valid & (qpos_st - (past_len + key_new) <= NSA_WINDOW), 0.0, NEG)
    per_hk = []
    for hk in range(NSA_KV_HEADS):
        kvs = [(wst_ref[0, hk], wst_ref[1, hk]), (wnew_ref[0, hk], wnew_ref[1, hk])]
        per_hk.append(attend(kvs, [bias_st, bias_new], hk))
    emit(ow_ref, per_hk)

    lane = lax.broadcasted_iota(jnp.int32, (HEAD_DIM, wbuf), 1)
    for r in range(2):
        for hk in range(NSA_KV_HEADS):
            shifted = pltpu.roll(wst_ref[r, hk], wbuf - n_new, 1)
            tail = jnp.concatenate([jnp.zeros((HEAD_DIM, wbuf - LANES), F32),
                                    pltpu.roll(wnew_ref[r, hk], LANES - n_new, 1)], axis=1)
            wout_ref[r, hk] = jnp.where(lane >= wbuf - n_new, tail, shifted)


def nsa_sample(layer, page_table, cache_t, new_t, q_n, q_rot, win_t, win_new_t, wt, selm, n_new, wout_prev):
    nb_, n_pages = page_table.shape
    page = cache_t.shape[5]
    past_len = n_pages * page
    n_real = (past_len + n_new + NSA_BLOCK - 1) // NSA_BLOCK
    wbuf = win_t.shape[5]
    kv = (NSA_KV_HEADS, HEAD_DIM)

    n_seq = SEQS_PER_STEP

    def page_spec(s, j):
        return pl.BlockSpec((None, None, 4) + kv + (page,), lambda b, pt: (layer, pt[b * n_seq + s, j], 0, 0, 0, 0))

    def seq(shape):
        return pl.BlockSpec((n_seq,) + shape, lambda b, pt: (b,) + (0,) * len(shape))

    def const(shape):
        return pl.BlockSpec(shape, lambda b, pt: (0,) * len(shape))

    win_spec = pl.BlockSpec((None, n_seq, 2) + kv + (wbuf,), lambda b, pt: (layer, b, 0, 0, 0, 0))
    in_specs = [page_spec(s, j) for s in range(n_seq) for j in range(n_pages)] + [
        seq((4,) + kv + (page,)), seq((QPAD, 512)), seq((QPAD, 512)), win_spec,
        seq((2,) + kv + (LANES,)), const(wt.shape), const(selm.shape)]
    args = [page_table] + [cache_t] * (n_seq * n_pages) + [new_t, q_n, q_rot, win_t, win_new_t, wt, selm]
    aliases = {}
    if wout_prev is not None:
        in_specs.append(pl.BlockSpec(memory_space=pl.ANY))
        args.append(wout_prev)
        aliases = {len(args) - 1: 3}
    grid_spec = pltpu.PrefetchScalarGridSpec(
        num_scalar_prefetch=1,
        grid=(nb_ // n_seq,),
        in_specs=in_specs,
        out_specs=[seq((QPAD, 512)), seq((QPAD, 512)), seq((QPAD, 512)), win_spec],
    )
    return pl.pallas_call(
        functools.partial(_nsa_sample_kernel, n_pages=n_pages, past_len=past_len, n_new=n_new,
                          n_sel=min(NSA_TOPK, n_real), n_seq=n_seq),
        grid_spec=grid_spec,
        out_shape=[jax.ShapeDtypeStruct((nb_, QPAD, 512), F32)] * 3 + [jax.ShapeDtypeStruct(win_t.shape, F32)],
        input_output_aliases=aliases,
        compiler_params=_cparams(("arbitrary",)),
        name="nsa_sample",
    )(*args)


def _diff_sample_kernel(pt_ref, *refs, n_pages, past_len, n_new, lam_init, n_seq):
    new_ref, q_ref, lam_ref, g_ref, o_ref = refs[n_seq * n_pages:]
    for s in range(n_seq):
        _diff_sample_one(refs[s * n_pages:(s + 1) * n_pages], new_ref.at[s], q_ref.at[s], lam_ref, g_ref, o_ref.at[s],
                         n_pages=n_pages, n_new=n_new, lam_init=lam_init)


def _diff_sample_one(pages, new_ref, q_ref, lam_ref, g_ref, o_ref, *, n_pages, n_new, lam_init):
    lo = _lane_lo((QPAD, LANES))
    tok = lax.broadcasted_iota(jnp.int32, (2 * QPAD, 1), 0) % QPAD
    key_new = lax.broadcasted_iota(jnp.int32, (1, LANES), 1)
    new_valid = (key_new < n_new) & (key_new <= tok)
    lam = _diff_lambda(lam_ref, lam_init)
    q = q_ref[...]
    new = new_ref[...]
    zero = jnp.zeros((QPAD, LANES), BF16)
    page = pages[0].shape[0] // (2 * DIFF_HEADS)
    bias = jnp.concatenate([jnp.zeros((2 * QPAD, n_pages * page), F32), jnp.where(new_valid, 0.0, NEG)], axis=1)
    for h in range(DIFF_HEADS):
        qp = q[:, h * LANES:(h + 1) * LANES]
        qst = jnp.concatenate([jnp.where(lo, qp, zero), jnp.where(lo, zero, qp)], axis=0)
        ks = [pages[j][pl.ds(h, page, stride=2 * DIFF_HEADS), :] for j in range(n_pages)]
        ks.append(_pad_keys(new[:, h * LANES:(h + 1) * LANES]))
        vs = [pages[j][pl.ds(DIFF_HEADS + h, page, stride=2 * DIFF_HEADS), :] for j in range(n_pages)]
        vs.append(_pad_keys(new[:, (DIFF_HEADS + h) * LANES:(DIFF_HEADS + h + 1) * LANES]))
        k_all = jnp.concatenate([k.astype(BF16) for k in ks], axis=0)
        v_all = jnp.concatenate([v.astype(BF16) for v in vs], axis=0)
        s = _dot_nt(qst, k_all) + bias
        p = jnp.exp2(s - jnp.max(s, axis=-1, keepdims=True))
        p = p / jnp.maximum(jnp.sum(p, axis=-1, keepdims=True), 1e-30)
        o = _dot((p[0:QPAD] - lam * p[QPAD:2 * QPAD]).astype(BF16), v_all)
        ms = jnp.mean(o * o, axis=-1, keepdims=True)
        o_ref[:, h * LANES:(h + 1) * LANES] = (o * lax.rsqrt(ms + NORM_EPS) * g_ref[...] * (1.0 - lam_init)).astype(o_ref.dtype)


def diff_sample(layer, page_table, cache, new_rows, dq, lam_p, out_g, n_new, lam_init):
    nb_, n_pages = page_table.shape
    page = cache.shape[2] // (2 * DIFF_HEADS)

    n_seq = SEQS_PER_STEP

    def page_spec(s, j):
        return pl.BlockSpec((None, None, page * 2 * DIFF_HEADS, LANES),
                            lambda b, pt: (layer, pt[b * n_seq + s, j], 0, 0))

    grid_spec = pltpu.PrefetchScalarGridSpec(
        num_scalar_prefetch=1,
        grid=(nb_ // n_seq,),
        in_specs=[page_spec(s, j) for s in range(n_seq) for j in range(n_pages)] + [
            pl.BlockSpec((n_seq, QPAD, 1024), lambda b, pt: (b, 0, 0)),
            pl.BlockSpec((n_seq, QPAD, 512), lambda b, pt: (b, 0, 0)),
            pl.BlockSpec((4, HEAD_DIM), lambda b, pt: (0, 0)),
            pl.BlockSpec((1, LANES), lambda b, pt: (0, 0))],
        out_specs=pl.BlockSpec((n_seq, QPAD, 512), lambda b, pt: (b, 0, 0)),
    )
    return pl.pallas_call(
        functools.partial(_diff_sample_kernel, n_pages=n_pages, past_len=n_pages * page, n_new=n_new,
                          lam_init=lam_init, n_seq=n_seq),
        grid_spec=grid_spec,
        out_shape=jax.ShapeDtypeStruct((nb_, QPAD, 512), F32),
        compiler_params=_cparams(("arbitrary",)),
        name="diff_sample",
    )(page_table, *([cache] * (n_seq * n_pages)), new_rows, dq, lam_p, out_g)


def _mix_weight(w):
    d = w.shape[0]
    splits = (A_WIDTH, A_WIDTH, A_WIDTH, 512, 768, 3 * NSA_HEADS, 512, 512, 512, 2 * D_WIDTH, N_BRANCH * d)
    offs = [0]
    for s in splits:
        offs.append(offs[-1] + s)
    part = {n: w[:, offs[i]:offs[i + 1]] for i, n in enumerate(
        ("ab", "ac", "ax", "nq", "kv", "g", "dq", "dk", "dv", "glu", "merge"))}
    pad = jnp.zeros((d, C_MERGE - C_NG - 3 * NSA_HEADS), w.dtype)
    cols = [part["kv"], part["g"], pad, part["merge"], part["glu"], part["ab"], part["ac"], part["ax"], part["nq"],
            part["dq"], part["dk"], part["dv"]]
    out = jnp.concatenate(cols, axis=1).astype(BF16)
    assert out.shape[1] == MIX_W
    return out


def _gate_expansion():
    r = jnp.arange(LANES)[:, None]
    c = jnp.arange(3 * NSA_HEADS * HEAD_DIM)[None, :]
    branch, head = c // (NSA_HEADS * HEAD_DIM), (c % (NSA_HEADS * HEAD_DIM)) // HEAD_DIM
    return (r == head * 3 + branch).astype(BF16)


def _rope_tables(pos):
    half = HEAD_DIM // 2
    inv = ROPE_THETA ** (-jnp.arange(half, dtype=F32) / half)
    ang = pos.astype(F32)[:, None] * inv[None, :]
    cos, sin = jnp.cos(ang), jnp.sin(ang)
    return jnp.tile(jnp.concatenate([cos, cos], axis=1), (1, 2)), jnp.tile(jnp.concatenate([-sin, sin], axis=1), (1, 2))


def _tile_gain(g):
    return jnp.tile(g.astype(F32), LANES // HEAD_DIM)


def _expansion(n_tiles, nb_rows, tk):
    j = jnp.arange(n_tiles)[:, None, None]
    n = jnp.arange(nb_rows)[None, :, None]
    k = jnp.arange(tk)[None, None, :]
    return ((j * tk + k) // NSA_BLOCK == n).astype(BF16)


def _tokens_last(x, nb_, n_new, lead):
    y = jnp.moveaxis(x.reshape((n_new, nb_) + lead), 0, -1)
    return jnp.pad(y, [(0, 0)] * (y.ndim - 1) + [(0, LANES - n_new)])


def _to_seq_major(x, nb_, n_new, pad_to=None):
    y = jnp.swapaxes(x.reshape(n_new, nb_, x.shape[-1]), 0, 1)
    if pad_to is not None and pad_to > n_new:
        y = jnp.pad(y, ((0, 0), (0, pad_to - n_new), (0, 0)))
    return y


def _to_step_major(y, n_new):
    return jnp.swapaxes(y[:, :n_new], 0, 1).reshape(-1, y.shape[-1])


def kernel(x_prompt, x_sample, cache_nsa, cache_diff, state_nsa_win, state_conv_a, state_conv_d, page_table,
           c_prompt, c_sample, ada_w, ada_b, norm_g, ffn1_up, ffn1_down, ffn2_up, ffn2_down, w_mix_in,
           conv_a_w, conv_a_b, a_out, nsa_qk_g, nsa_cmp_w, nsa_out, diff_qk_g, diff_lambda, diff_out_g, diff_out,
           conv_d_w, conv_d_b, d_norm_g, d_norm_b, d_out, w_out):
    bp, sp, d = x_prompt.shape
    bs, ns, _ = x_sample.shape
    depth = ada_w.shape[0]
    n_pages = page_table.shape[1]
    page = cache_nsa.shape[2]
    past_len = n_pages * page
    wbuf = state_nsa_win.shape[2]

    rows = bp + bs
    rpad = -(-rows // 8) * 8
    c_all = jnp.pad(jnp.concatenate([c_prompt, c_sample], axis=0), ((0, rpad - rows), (0, 0)))
    ada = ada_proj(c_all, ada_w, ada_b)

    xp = x_prompt.reshape(bp * sp, d)
    xs = jnp.swapaxes(x_sample, 0, 1).reshape(ns * bs, d)
    tm_p, tm_s = 512, ns * bs
    tpg_p = sp // tm_p

    cos_p, sin_p = _rope_tables(jnp.arange(sp))
    cos_s, sin_s = _rope_tables(jnp.repeat(past_len + jnp.arange(ns), bs))
    gmat = (jnp.arange(LANES)[:, None] // HEAD_DIM == jnp.arange(LANES)[None, :] // HEAD_DIM).astype(BF16)
    gate_ex = _gate_expansion()
    tk_slc = 512
    emat_p = _expansion(sp // tk_slc, sp // NSA_BLOCK, tk_slc)
    n_real_s = (past_len + ns + NSA_BLOCK - 1) // NSA_BLOCK
    nbp_s = -(-n_real_s // LANES) * LANES
    selm_s = _expansion(1, nbp_s, past_len + page)[0].T
    cache_nsa_t = jnp.transpose(cache_nsa, (0, 1, 3, 4, 5, 2))
    win_state_t = jnp.transpose(state_nsa_win, (0, 1, 3, 4, 5, 2))
    cache_diff2 = cache_diff.reshape(depth, cache_diff.shape[1], page * 2 * DIFF_HEADS, 2 * HEAD_DIM)

    zeros_a = jnp.zeros((bp, 8, A_WIDTH), F32)
    zeros_d = jnp.zeros((bp, 32, D_WIDTH), F32)

    outs = {k: [] for k in ("nsa_p", "nsa_s", "win_p", "ca_p", "ca_s", "cd_p", "cd_s")}
    drows_p = drows_s = win_out_t = None
    for l in range(depth):
        lam_init = 0.8 - 0.6 * math.exp(-0.3 * l)
        mod_p = ada[l, :, 0:bp].reshape(N_ADA, bp, 1, d)
        mod_s = ada[l, :, bp:bp + bs].reshape(N_ADA, 1, bs, d)
        g = norm_g[l]
        w1u, w1d = ffn1_up[l].astype(BF16), ffn1_down[l].astype(BF16)
        w2u, w2d = ffn2_up[l].astype(BF16), ffn2_down[l].astype(BF16)
        wmix = _mix_weight(w_mix_in[l])
        gains = jnp.stack([_tile_gain(nsa_qk_g[l, 0]), _tile_gain(nsa_qk_g[l, 1]), _tile_gain(nsa_qk_g[l, 2]),
                           _tile_gain(nsa_qk_g[l, 3]), _tile_gain(diff_qk_g[l, 0]), _tile_gain(diff_qk_g[l, 1]),
                           jnp.zeros((LANES,), F32), jnp.zeros((LANES,), F32)])
        cw = nsa_cmp_w[l]
        w2 = jnp.concatenate([cw[0], cw[0], cw[1], cw[1]], axis=1)
        wt = jnp.tile(jnp.swapaxes(cw, 1, 2), (1, 1, page // NSA_BLOCK))
        wt = jnp.broadcast_to(wt[:, None], (2, NSA_KV_HEADS, HEAD_DIM, page)).reshape(4 * HEAD_DIM, page)
        wts = [a_out[l].astype(BF16), nsa_out[l].astype(BF16), diff_out[l].astype(BF16), d_out[l].astype(BF16),
               w_out[l].astype(BF16)]
        conv_w = (conv_a_w[l], conv_a_b[l][None], conv_d_w[l], conv_d_b[l][None], d_norm_g[l][None], d_norm_b[l][None])
        lam_p, og = diff_lambda[l], diff_out_g[l][None]

        xp = ffn(xp, mod_p, 0, g[0:1], w1u, w1d, tm_p, tpg_p)
        zp = mix_in(xp, mod_p, g[1:2], wmix, tm_p, tpg_p)
        qn, qr, rows_p, win_p, dq, drows_p, ksd, vsd, dkb, dvb = qk_prep(
            zp, cos_p, sin_p, gains, gmat, tm_p, tpg_p, l, depth, drows_p)
        ua, ud, ca, cd = conv_branches(zp, zeros_a, zeros_d, *conv_w, nseq=bp, tm=tm_p, stride=1)
        cmp_p = nsa_compress(rows_p, w2)
        o_cmp, sel = nsa_cmp_select(qn, cmp_p, bp)
        o_slc = nsa_slc(qr, ksd, vsd, sel, emat_p, bp)
        o_win = nsa_win(qr, win_p, bp)
        uc = diff_attn(dq, dkb, dvb, lam_p, og, bp, lam_init)
        xp = merge_out(xp, mod_p, zp, ua, uc, ud, o_cmp, o_slc, o_win, gate_ex, *wts, tm=tm_p, tiles_per_group=tpg_p)
        xp = ffn(xp, mod_p, 6, g[2:3], w2u, w2d, tm_p, tpg_p)
        outs["nsa_p"].append(rows_p.reshape(bp, sp, 4, NSA_KV_HEADS, HEAD_DIM))
        wkeep = min(NSA_WINDOW, sp)
        outs["win_p"].append(win_p.reshape(bp, sp, 2 * NSA_KV_HEADS * HEAD_DIM)[:, sp - wkeep:].reshape(
            bp, wkeep, 2, NSA_KV_HEADS, HEAD_DIM))
        outs["ca_p"].append(ca)
        outs["cd_p"].append(cd)

        xs = ffn(xs, mod_s, 0, g[0:1], w1u, w1d, tm_s, 1)
        zs = mix_in(xs, mod_s, g[1:2], wmix, tm_s, 1)
        qn, qr, rows_s, win_s, dq, drows_s = qk_prep(zs, cos_s, sin_s, gains, gmat, tm_s, 1, l, depth, drows_s)[:6]
        past_a = jnp.swapaxes(state_conv_a[l], 0, 1).reshape(1, (A_CONV - 1) * bs, A_WIDTH)
        past_d = jnp.swapaxes(state_conv_d[l], 0, 1).reshape(1, (D_CONV - 1) * bs, D_WIDTH)
        ua, ud, ca, cd = conv_branches(zs, past_a, past_d, *conv_w, nseq=1, tm=tm_s, stride=bs)
        rows_b = _to_seq_major(rows_s, bs, ns)
        new_t = _tokens_last(rows_s, bs, ns, (4, NSA_KV_HEADS, HEAD_DIM))
        win_new_t = _tokens_last(win_s, bs, ns, (2, NSA_KV_HEADS, HEAD_DIM))
        o_cmp, o_slc, o_win, win_out_t = nsa_sample(
            l, page_table, cache_nsa_t, new_t, _to_seq_major(qn, bs, ns, QPAD), _to_seq_major(qr, bs, ns, QPAD),
            win_state_t, win_new_t, wt, selm_s, ns, win_out_t)
        drows_b = _to_seq_major(drows_s[l].reshape(ns * bs, 2 * DIFF_HEADS * LANES), bs, ns, QPAD)
        uc = diff_sample(l, page_table, cache_diff2, drows_b, _to_seq_major(dq, bs, ns, QPAD), lam_p, og, ns, lam_init)
        xs = merge_out(xs, mod_s, zs, ua, _to_step_major(uc, ns).astype(BF16), ud, _to_step_major(o_cmp, ns),
                       _to_step_major(o_slc, ns), _to_step_major(o_win, ns), gate_ex, *wts, tm=tm_s, tiles_per_group=1)
        xs = ffn(xs, mod_s, 6, g[2:3], w2u, w2d, tm_s, 1)
        outs["nsa_s"].append(rows_b.reshape(bs, ns, 4, NSA_KV_HEADS, HEAD_DIM))
        outs["ca_s"].append(jnp.swapaxes(ca.reshape(A_CONV - 1, bs, A_WIDTH), 0, 1))
        outs["cd_s"].append(jnp.swapaxes(cd.reshape(D_CONV - 1, bs, D_WIDTH), 0, 1))

    y_p = xp.reshape(bp, sp, d)
    y_s = jnp.swapaxes(xs.reshape(ns, bs, d), 0, 1)
    st = {k: jnp.stack(v, axis=0) for k, v in outs.items()}
    diff_p = drows_p.reshape(depth, bp, sp, 2, DIFF_HEADS, 2 * HEAD_DIM)
    diff_s = jnp.swapaxes(drows_s.reshape(depth, ns, bs, 2, DIFF_HEADS, 2 * HEAD_DIM), 1, 2)
    win_s = jnp.moveaxis(win_out_t, -1, 2)
    return (y_p, y_s, st["nsa_p"], st["nsa_s"], diff_p, diff_s, st["win_p"], win_s,
            st["ca_p"], st["ca_s"], st["cd_p"], st["cd_s"])
```
